```python
import math
import jax, jax.numpy as jnp
from jax import lax
import numpy as np

D_MODEL = 1024
BATCH = 16
SEQ = 2048
DEPTH = 4
DEC_BATCH = 128
DEC_SEQ = 4
PAST_LEN = 8192
PAGE_SIZE = 128

N_MIXERS = 4
EPS = 1e-6
NEG_INF = -1e30
QUERY_BLOCK = 128

HQ_A = 16
HKV_A = 4
DH_A = D_MODEL // HQ_A
GRP_A = HQ_A // HKV_A
BLOCK_A = 256
TOPK_A = 3
ROWS_PROMPT_A = 128
ROWS_SAMPLE_A = 1
STEP_A = 64
SCALE_A = DH_A ** -0.5
NUM_BUCKETS = 32
MAX_DISTANCE = 128

HEADS_B = 16
NOPE_B = 64
ROPE_B = 32
V_B = 64
Q_LORA_B = D_MODEL // 4
KV_LORA_B = D_MODEL // 8
ROPE_THETA = 10000.0
SCALE_B = (NOPE_B + ROPE_B) ** -0.5

CHUNK_C = 128
DV_C = 2 * D_MODEL
GROUPS_C = 8
GW_C = DV_C // GROUPS_C

CONV_W = 3

N_GROUPS_E = 4
EXPERTS_PER_GROUP = 8
N_EXPERTS = N_GROUPS_E * EXPERTS_PER_GROUP
TOPK_E = 2
D_EXPERT = D_MODEL // 4
ROWS_E = 128
STEP_E = 4

kernel_name = 'hybrid_moba_mla_gmlp_conv_hmoe_step'


def rmsnorm(x, g):
    xf = x.astype(jnp.float32)
    y = xf * lax.rsqrt(jnp.mean(xf * xf, axis=-1, keepdims=True) + EPS)
    return (y * g.astype(jnp.float32)).astype(x.dtype)


def layernorm(x, g, b):
    xf = x.astype(jnp.float32)
    mu = jnp.mean(xf, axis=-1, keepdims=True)
    var = jnp.mean(jnp.square(xf - mu), axis=-1, keepdims=True)
    y = (xf - mu) * lax.rsqrt(var + EPS) * g.astype(jnp.float32) + b.astype(jnp.float32)
    return y.astype(x.dtype)


def ada_mod(c, w, b):
    m = (jax.nn.silu(c) @ w + b)[:, None, :]
    return jnp.split(m, 6, axis=-1)


def ada_norm(x, g, shift, scale):
    return rmsnorm(x, g) * (1 + scale) + shift


def t5_bucket(dist):
    n = jnp.maximum(dist, 0)
    max_exact = NUM_BUCKETS // 2
    ratio = jnp.log(jnp.maximum(n, 1).astype(jnp.float32) / max_exact) / math.log(MAX_DISTANCE / max_exact)
    large = jnp.minimum(max_exact + (ratio * (NUM_BUCKETS - max_exact)).astype(jnp.int32), NUM_BUCKETS - 1)
    return jnp.where(n < max_exact, n, large)


def partial_softmax(s):
    m = jnp.max(s, axis=-1)
    e = jnp.exp(s - m[..., None])
    return m, e, jnp.sum(e, axis=-1)


def gather_pages(cache, page_table):
    g = cache[page_table]
    return g.reshape(g.shape[0], g.shape[1] * g.shape[2], *g.shape[3:])


def to_blocks(x, n_blocks):
    pad = n_blocks * BLOCK_A - x.shape[1]
    if pad > 0:
        x = jnp.pad(x, [(0, 0), (0, pad)] + [(0, 0)] * (x.ndim - 2))
    return x.reshape(x.shape[0], n_blocks, BLOCK_A, *x.shape[2:])


def group_rows(seg, n_seg, rows, max_step):
    n = seg.shape[0]
    order = jnp.argsort(seg)
    seg_s = seg[order]
    counts = jnp.bincount(seg, length=n_seg)
    padded = (counts + rows - 1) // rows * rows
    pend = jnp.cumsum(padded)
    start = jnp.cumsum(counts) - counts
    dest = (pend - padded)[seg_s] + jnp.arange(n, dtype=jnp.int32) - start[seg_s]
    row_of_item = jnp.zeros((n,), jnp.int32).at[order].set(dest.astype(jnp.int32))
    n_blk = -(-(n + min(n_seg, n) * (rows - 1)) // rows)
    step = min(max_step, n_blk)
    n_blk = -(-n_blk // step) * step
    blk_seg = jnp.minimum(jnp.searchsorted(pend, jnp.arange(n_blk, dtype=jnp.int32) * rows, side='right'), n_seg - 1)
    return row_of_item, blk_seg.reshape(n_blk // step, step).astype(jnp.int32)


def moba_qkv(h, w_qkv):
    B, T, _ = h.shape
    q, k, v = jnp.split(h @ w_qkv, [HQ_A * DH_A, (HQ_A + HKV_A) * DH_A], axis=-1)
    return (q.reshape(B, T, HKV_A, GRP_A, DH_A), k.reshape(B, T, HKV_A, DH_A), v.reshape(B, T, HKV_A, DH_A))


def moba_selected(q, qpos, kb, vb, bias_hq, rows):
    B, T = q.shape[:2]
    nb = kb.shape[1]
    kmean = jnp.mean(kb.astype(jnp.float32), axis=2)
    gate = jnp.einsum('bthgd,bnhd->bthgn', q.astype(jnp.float32), kmean)
    fully_past = jnp.arange(nb)[None, :] < (qpos // BLOCK_A)[:, None]
    gate = jnp.where(fully_past[None, :, None, None, :], gate, NEG_INF)
    top_s, top_j = lax.top_k(gate, TOPK_A)
    valid = top_s > 0.5 * NEG_INF
    shape = top_j.shape
    b_ix = lax.broadcasted_iota(jnp.int32, shape, 0)
    h_ix = lax.broadcasted_iota(jnp.int32, shape, 2)
    n_seg = B * HKV_A * nb
    seg = ((b_ix * HKV_A + h_ix) * nb + top_j).reshape(-1)
    row_of_pair, blk_seg = group_rows(seg, n_seg, rows, STEP_A)
    n_steps, step = blk_seg.shape
    n_rows = n_steps * step * rows
    head = h_ix * GRP_A + lax.broadcasted_iota(jnp.int32, shape, 3)
    pos = jnp.broadcast_to(qpos[None, :, None, None, None], shape)
    q_pairs = jnp.broadcast_to(q[..., None, :], shape + (DH_A,)).reshape(-1, DH_A)
    row_q = jnp.zeros((n_rows, DH_A), q.dtype).at[row_of_pair].set(q_pairs)
    row_pos = jnp.zeros((n_rows,), jnp.int32).at[row_of_pair].set(pos.reshape(-1))
    row_head = jnp.zeros((n_rows,), jnp.int32).at[row_of_pair].set(head.reshape(-1))

    def attend_gathered(args):
        cs, rq, rp, rh = args
        cb, ch, cj = cs // (HKV_A * nb), (cs // nb) % HKV_A, cs % nb
        k_c = kb[cb, cj, :, ch, :]
        v_c = vb[cb, cj, :, ch, :]
        s = jnp.einsum('crd,csd->crs', rq, k_c, preferred_element_type=jnp.float32) * SCALE_A
        kpos = cj[:, None] * BLOCK_A + jnp.arange(BLOCK_A)[None, :]
        s = s + bias_hq[t5_bucket(rp[:, :, None] - kpos[:, None, :]), rh[:, :, None]].astype(jnp.float32)
        m, e, l = partial_softmax(s)
        return m, l, jnp.einsum('crs,csd->crd', e, v_c.astype(jnp.float32))

    m, l, acc = lax.map(attend_gathered, (blk_seg, row_q.reshape(n_steps, step, rows, DH_A),
                                          row_pos.reshape(n_steps, step, rows), row_head.reshape(n_steps, step, rows)))
    m = jnp.where(valid, m.reshape(-1)[row_of_pair].reshape(shape), NEG_INF)
    l = jnp.where(valid, l.reshape(-1)[row_of_pair].reshape(shape), 0.0)
    acc = jnp.where(valid[..., None], acc.reshape(-1, DH_A)[row_of_pair].reshape(shape + (DH_A,)), 0.0)
    return m, l, acc


def moba_own_prompt(q, kb, vb, bias_hq):
    B, S = q.shape[:2]
    nq = -(-S // BLOCK_A)
    qb = to_blocks(q, nq)
    dist = jnp.arange(BLOCK_A)[:, None] - jnp.arange(BLOCK_A)[None, :]
    bias = bias_hq[t5_bucket(dist)].reshape(BLOCK_A, BLOCK_A, HKV_A, GRP_A).transpose(0, 2, 3, 1)
    s = jnp.einsum('bnthgd,bnshd->bnthgs', qb, kb[:, :nq], preferred_element_type=jnp.float32) * SCALE_A
    s = jnp.where((dist >= 0)[:, None, None, :], s + bias.astype(jnp.float32), NEG_INF)
    m, e, l = partial_softmax(s)
    acc = jnp.einsum('bnthgs,bnshd->bnthgd', e, vb[:, :nq].astype(jnp.float32))
    fold = lambda a: a.reshape(B, nq * BLOCK_A, *a.shape[3:])[:, :S]
    return fold(m), fold(l), fold(acc)


def moba_own_sample(q, qpos, k_own, v_own, own_start, bias_hq):
    T = q.shape[1]
    dist = qpos[:, None] - (own_start + jnp.arange(BLOCK_A))[None, :]
    bias = bias_hq[t5_bucket(dist)].reshape(T, BLOCK_A, HKV_A, GRP_A).transpose(0, 2, 3, 1)
    s = jnp.einsum('bthgd,bshd->bthgs', q, k_own, preferred_element_type=jnp.float32) * SCALE_A
    s = jnp.where((dist >= 0)[:, None, None, :], s + bias.astype(jnp.float32), NEG_INF)
    m, e, l = partial_softmax(s)
    return m, l, jnp.einsum('bthgs,bshd->bthgd', e, v_own.astype(jnp.float32))


def merge_partials(m_sel, l_sel, acc_sel, m_own, l_own, acc_own):
    m = jnp.maximum(m_own, jnp.max(m_sel, axis=-1))
    w_sel = jnp.exp(m_sel - m[..., None])
    w_own = jnp.exp(m_own - m)
    num = acc_own * w_own[..., None] + jnp.einsum('bthgk,bthgkd->bthgd', w_sel, acc_sel)
    den = l_own * w_own + jnp.sum(l_sel * w_sel, axis=-1)
    o = num / den[..., None]
    return o.reshape(o.shape[0], o.shape[1], HQ_A * DH_A)


def moba_prompt(h, w_qkv, w_o, bias_hq):
    B, S, _ = h.shape
    q, k, v = moba_qkv(h, w_qkv)
    nb = max(-(-S // BLOCK_A), TOPK_A)
    kb, vb = to_blocks(k, nb), to_blocks(v, nb)
    sel = moba_selected(q, jnp.arange(S), kb, vb, bias_hq, ROWS_PROMPT_A)
    own = moba_own_prompt(q, kb, vb, bias_hq)
    o = merge_partials(*sel, *own).astype(h.dtype)
    return o @ w_o, k, v


def moba_sample(h, cache_k, cache_v, page_table, w_qkv, w_o, bias_hq):
    q, k, v = moba_qkv(h, w_qkv)
    T = q.shape[1]
    k_past, v_past = gather_pages(cache_k, page_table), gather_pages(cache_v, page_table)
    L = k_past.shape[1]
    nb = max(-(-L // BLOCK_A), TOPK_A)
    qpos = L + jnp.arange(T)
    sel = moba_selected(q, qpos, to_blocks(k_past, nb), to_blocks(v_past, nb), bias_hq, ROWS_SAMPLE_A)
    own_start = (L // BLOCK_A) * BLOCK_A
    fill = BLOCK_A - (L - own_start) - T
    own_rows = lambda past, new: jnp.pad(jnp.concatenate([past[:, own_start:], new], axis=1),
                                         ((0, 0), (0, fill), (0, 0), (0, 0)))
    own = moba_own_sample(q, qpos, own_rows(k_past, k), own_rows(v_past, v), own_start, bias_hq)
    o = merge_partials(*sel, *own).astype(h.dtype)
    return o @ w_o, k, v


def rope(x, pos):
    half = x.shape[-1] // 2
    inv = ROPE_THETA ** (-jnp.arange(half, dtype=jnp.float32) / half)
    ang = pos.astype(jnp.float32)[:, None] * inv[None, :]
    cos, sin = jnp.cos(ang)[:, None, :], jnp.sin(ang)[:, None, :]
    xf = x.astype(jnp.float32)
    x1, x2 = xf[..., :half], xf[..., half:]
    return jnp.concatenate([x1 * cos - x2 * sin, x2 * cos + x1 * sin], axis=-1).astype(x.dtype)


def mla_project(h, pos, w_in, q_norm_g, w_qb, kv_norm_g, w_kvb):
    B, T, _ = h.shape
    q_a, ckv, kpe = jnp.split(h @ w_in, [Q_LORA_B, Q_LORA_B + KV_LORA_B], axis=-1)
    q = (rmsnorm(q_a, q_norm_g) @ w_qb).reshape(B, T, HEADS_B, NOPE_B + ROPE_B)
    q_pe = rope(q[..., NOPE_B:], pos)
    q_lat = jnp.einsum('bthn,chn->bthc', q[..., :NOPE_B], w_kvb[:, :, :NOPE_B])
    ckv = rmsnorm(ckv, kv_norm_g)
    kpe = rope(kpe[:, :, None, :], pos)[:, :, 0, :]
    return q_lat, q_pe, ckv, kpe


def mla_attend(q_lat, q_pe, qpos, ckv, kpe):
    kpos = jnp.arange(ckv.shape[1])
    s = (jnp.einsum('bthc,bsc->bhts', q_lat, ckv, preferred_element_type=jnp.float32)
         + jnp.einsum('bthr,bsr->bhts', q_pe, kpe, preferred_element_type=jnp.float32)) * SCALE_B
    s = jnp.where(kpos[None, :] <= qpos[:, None], s, NEG_INF)
    p = jax.nn.softmax(s, axis=-1)
    return jnp.einsum('bhts,bsc->bthc', p.astype(ckv.dtype), ckv)


def mla_out(o_lat, w_kvb, w_o):
    B, T = o_lat.shape[:2]
    o = jnp.einsum('bthc,chv->bthv', o_lat, w_kvb[:, :, NOPE_B:])
    return o.reshape(B, T, HEADS_B * V_B) @ w_o


def mla_prompt(h, w_in, q_norm_g, w_qb, kv_norm_g, w_kvb, w_o):
    B, S, _ = h.shape
    q_lat, q_pe, ckv, kpe = mla_project(h, jnp.arange(S), w_in, q_norm_g, w_qb, kv_norm_g, w_kvb)
    nq = S // QUERY_BLOCK
    blocks = lambda a: a.reshape(B, nq, QUERY_BLOCK, *a.shape[2:]).swapaxes(0, 1)

    def attend_block(args):
        qi, ql, qp = args
        return mla_attend(ql, qp, qi * QUERY_BLOCK + jnp.arange(QUERY_BLOCK), ckv, kpe)

    o_lat = lax.map(attend_block, (jnp.arange(nq), blocks(q_lat), blocks(q_pe)))
    o_lat = o_lat.swapaxes(0, 1).reshape(B, S, HEADS_B, KV_LORA_B)
    return mla_out(o_lat, w_kvb, w_o), ckv, kpe


def mla_sample(h, cache_ckv, cache_kpe, page_table, w_in, q_norm_g, w_qb, kv_norm_g, w_kvb, w_o):
    T = h.shape[1]
    ckv_past, kpe_past = gather_pages(cache_ckv, page_table), gather_pages(cache_kpe, page_table)
    qpos = ckv_past.shape[1] + jnp.arange(T)
    q_lat, q_pe, ckv, kpe = mla_project(h, qpos, w_in, q_norm_g, w_qb, kv_norm_g, w_kvb)
    o_lat = mla_attend(q_lat, q_pe, qpos, jnp.concatenate([ckv_past, ckv], axis=1),
                       jnp.concatenate([kpe_past, kpe], axis=1))
    return mla_out(o_lat, w_kvb, w_o), ckv, kpe


def chunk_mix(h, w_in, ln_g, ln_b, w_s, b_s, w_out):
    B, T, _ = h.shape
    u, v = jnp.split(jax.nn.gelu(h @ w_in, approximate=False), 2, axis=-1)
    v = layernorm(v, ln_g, ln_b)
    L = min(T, CHUNK_C)
    w_causal = jnp.tril(w_s)[:, :L, :L]
    vg = v.reshape(B, T // L, L, GROUPS_C, GW_C)
    mixed = jnp.einsum('gij,bnjgc->bnigc', w_causal, vg) + b_s[:, :L].T[None, None, :, :, None]
    return (u * mixed.reshape(B, T, DV_C)) @ w_out, v


def short_conv(h, prefix, w_in, conv_w, w_out):
    T = h.shape[1]
    gate_b, gate_c, z = jnp.split(h @ w_in, 3, axis=-1)
    zp = jnp.concatenate([prefix.astype(z.dtype), gate_c * z], axis=1)
    y = conv_w[0] * zp[:, 0:T]
    for j in range(1, CONV_W):
        y = y + conv_w[j] * zp[:, j:j + T]
    return (gate_b * y) @ w_out, zp[:, -(CONV_W - 1):]


def hier_moe(h, w_group, b_group, w_expert, b_expert, w_gate_up, w_down):
    B, T, D = h.shape
    xt = h.reshape(-1, D)
    n_tok = xt.shape[0]
    lg = (xt @ w_group).astype(jnp.float32) + b_group.astype(jnp.float32)
    g_top = jnp.argmax(lg, axis=-1)
    p_g = jnp.take_along_axis(jax.nn.softmax(lg, axis=-1), g_top[:, None], axis=1)
    le = ((xt @ w_expert).astype(jnp.float32) + b_expert.astype(jnp.float32)).reshape(n_tok, N_GROUPS_E, EXPERTS_PER_GROUP)
    le = jnp.take_along_axis(le, g_top[:, None, None], axis=1)[:, 0]
    pe_top, e_top = lax.top_k(jax.nn.softmax(le, axis=-1), TOPK_E)
    gate = p_g * pe_top / jnp.sum(pe_top, axis=-1, keepdims=True)
    eid = (g_top[:, None] * EXPERTS_PER_GROUP + e_top).reshape(-1).astype(jnp.int32)
    row_of_item, blk_e = group_rows(eid, N_EXPERTS, ROWS_E, STEP_E)
    n_rows = blk_e.size * ROWS_E
    x_rows = jnp.zeros((n_rows, D), h.dtype).at[row_of_item].set(jnp.repeat(xt, TOPK_E, axis=0))

    def expert_blocks(args):
        e, xb = args
        a, b = jnp.split(jnp.einsum('crd,cdf->crf', xb, w_gate_up[e]), 2, axis=-1)
        return jnp.einsum('crf,cfd->crd', jax.nn.silu(a) * b, w_down[e])

    y_rows = lax.map(expert_blocks, (blk_e, x_rows.reshape(blk_e.shape + (ROWS_E, D)))).reshape(n_rows, D)
    y = jnp.einsum('nk,nkd->nd', gate.astype(h.dtype), y_rows[row_of_item].reshape(n_tok, TOPK_E, D))
    return y.reshape(B, T, D)


def setup_inputs(seed: int = 0) -> dict:
    key = jax.random.key(seed)
    ks = iter(jax.random.split(key, 64))

    def nrm(shape, scale=1.0):
        return jax.random.normal(next(ks), shape, jnp.float32) * scale

    D = D_MODEL
    n_pages = PAST_LEN // PAGE_SIZE
    n_pool = (DEC_BATCH * n_pages * 5) // 4
    page_table = jax.random.permutation(next(ks), n_pool)[:DEC_BATCH * n_pages].reshape(DEC_BATCH, n_pages).astype(jnp.int32)
    return {
        'x_prompt': nrm((BATCH, SEQ, D)),
        'x_sample': nrm((DEC_BATCH, DEC_SEQ, D)),
        'cache_moba_k': nrm((n_pool, PAGE_SIZE, HKV_A, DH_A)),
        'cache_moba_v': nrm((n_pool, PAGE_SIZE, HKV_A, DH_A)),
        'cache_mla_ckv': nrm((n_pool, PAGE_SIZE, KV_LORA_B)),
        'cache_mla_kpe': nrm((n_pool, PAGE_SIZE, ROPE_B)),
        'state_conv': nrm((DEC_BATCH, CONV_W - 1, D)),
        'page_table': page_table,
        'c_prompt': nrm((BATCH, D)),
        'c_sample': nrm((DEC_BATCH, D)),
        'norm_mix_g': 1.0 + nrm((DEPTH, D), 0.05),
        'norm_ffn_g': 1.0 + nrm((DEPTH, D), 0.05),
        'w_ada': nrm((DEPTH, D, 6 * D), 0.5 * D ** -0.5),
        'b_ada': nrm((DEPTH, 6 * D), 0.02),
        'rel_bias': nrm((NUM_BUCKETS, HQ_A), 0.3),
        'moba_w_qkv': nrm((D, (HQ_A + 2 * HKV_A) * DH_A), D ** -0.5),
        'moba_w_o': nrm((HQ_A * DH_A, D), (HQ_A * DH_A) ** -0.5),
        'mla_w_in': nrm((D, Q_LORA_B + KV_LORA_B + ROPE_B), D ** -0.5),
        'mla_q_norm_g': 1.0 + nrm((Q_LORA_B,), 0.05),
        'mla_w_qb': nrm((Q_LORA_B, HEADS_B * (NOPE_B + ROPE_B)), Q_LORA_B ** -0.5),
        'mla_kv_norm_g': 1.0 + nrm((KV_LORA_B,), 0.05),
        'mla_w_kvb': nrm((KV_LORA_B, HEADS_B, NOPE_B + V_B), KV_LORA_B ** -0.5),
        'mla_w_o': nrm((HEADS_B * V_B, D), (HEADS_B * V_B) ** -0.5),
        'chunk_w_in': nrm((D, 2 * DV_C), D ** -0.5),
        'chunk_ln_g': 1.0 + nrm((DV_C,), 0.05),
        'chunk_ln_b': nrm((DV_C,), 0.02),
        'chunk_w_s': nrm((GROUPS_C, CHUNK_C, CHUNK_C), CHUNK_C ** -0.5),
        'chunk_b_s': 1.0 + nrm((GROUPS_C, CHUNK_C), 0.02),
        'chunk_w_out': nrm((DV_C, D), DV_C ** -0.5),
        'conv_w_in': nrm((D, 3 * D), D ** -0.5),
        'conv_w': nrm((CONV_W, D), CONV_W ** -0.5),
        'conv_w_out': nrm((D, D), D ** -0.5),
        'moe_w_group': nrm((DEPTH, D, N_GROUPS_E), D ** -0.5),
        'moe_b_group': nrm((DEPTH, N_GROUPS_E), 0.01),
        'moe_w_expert': nrm((DEPTH, D, N_EXPERTS), D ** -0.5),
        'moe_b_expert': nrm((DEPTH, N_EXPERTS), 0.01),
        'moe_w_gate_up': nrm((DEPTH, N_EXPERTS, D, 2 * D_EXPERT), D ** -0.5),
        'moe_w_down': nrm((DEPTH, N_EXPERTS, D_EXPERT, D), D_EXPERT ** -0.5),
        'final_norm_g': 1.0 + nrm((D,), 0.05),
    }


def reference(x_prompt, x_sample, cache_moba_k, cache_moba_v, cache_mla_ckv, cache_mla_kpe, state_conv, page_table,
              c_prompt, c_sample, norm_mix_g, norm_ffn_g, w_ada, b_ada, rel_bias, moba_w_qkv, moba_w_o,
              mla_w_in, mla_q_norm_g, mla_w_qb, mla_kv_norm_g, mla_w_kvb, mla_w_o,
              chunk_w_in, chunk_ln_g, chunk_ln_b, chunk_w_s, chunk_b_s, chunk_w_out,
              conv_w_in, conv_w, conv_w_out, moe_w_group, moe_b_group, moe_w_expert, moe_b_expert,
              moe_w_gate_up, moe_w_down, final_norm_g):
    xp, xs = x_prompt, x_sample
    mla_w = (mla_w_in, mla_q_norm_g, mla_w_qb, mla_kv_norm_g, mla_w_kvb, mla_w_o)
    chunk_w = (chunk_w_in, chunk_ln_g, chunk_ln_b, chunk_w_s, chunk_b_s, chunk_w_out)
    for i in range(DEPTH):
        kind = i % N_MIXERS
        mp = ada_mod(c_prompt, w_ada[i], b_ada[i])
        ms = ada_mod(c_sample, w_ada[i], b_ada[i])
        hp = ada_norm(xp, norm_mix_g[i], mp[0], mp[1])
        hs = ada_norm(xs, norm_mix_g[i], ms[0], ms[1])
        if kind == 0:
            op, moba_k_prompt, moba_v_prompt = moba_prompt(hp, moba_w_qkv, moba_w_o, rel_bias)
            os_, moba_k_sample, moba_v_sample = moba_sample(hs, cache_moba_k, cache_moba_v, page_table,
                                                            moba_w_qkv, moba_w_o, rel_bias)
        elif kind == 1:
            op, mla_ckv_prompt, mla_kpe_prompt = mla_prompt(hp, *mla_w)
            os_, mla_ckv_sample, mla_kpe_sample = mla_sample(hs, cache_mla_ckv, cache_mla_kpe, page_table, *mla_w)
        elif kind == 2:
            op, _ = chunk_mix(hp, *chunk_w)
            os_, chunk_v_sample = chunk_mix(hs, *chunk_w)
        else:
            zero_prefix = jnp.zeros((hp.shape[0], CONV_W - 1, hp.shape[2]), hp.dtype)
            op, conv_prompt = short_conv(hp, zero_prefix, conv_w_in, conv_w, conv_w_out)
            os_, conv_sample = short_conv(hs, state_conv, conv_w_in, conv_w, conv_w_out)
        xp = xp + mp[2] * op
        xs = xs + ms[2] * os_
        moe_w = (moe_w_group[i], moe_b_group[i], moe_w_expert[i], moe_b_expert[i], moe_w_gate_up[i], moe_w_down[i])
        xp = xp + mp[5] * hier_moe(ada_norm(xp, norm_ffn_g[i], mp[3], mp[4]), *moe_w)
        xs = xs + ms[5] * hier_moe(ada_norm(xs, norm_ffn_g[i], ms[3], ms[4]), *moe_w)
    y_prompt = rmsnorm(xp, final_norm_g)
    y_sample = rmsnorm(xs, final_norm_g)
    return (y_prompt, y_sample, moba_k_prompt, moba_v_prompt, moba_k_sample, moba_v_sample,
            mla_ckv_prompt, mla_kpe_prompt, mla_ckv_sample, mla_kpe_sample, chunk_v_sample,
            conv_prompt, conv_sample)
```

```python
import functools
import math

import numpy as np
import jax
import jax.numpy as jnp
from jax import lax
from jax.experimental import pallas as pl
from jax.experimental.pallas import tpu as pltpu

F32, BF16, I32 = jnp.float32, jnp.bfloat16, jnp.int32
EPS = 1e-6
NEG_INF = -1e30
LANES = 128
SUBLANES = 8

HQ_A, HKV_A, GRP_A, DH_A = 16, 4, 4, 64
BLOCK_A, TOPK_A = 256, 3
NUM_BUCKETS, MAX_DISTANCE = 32, 128
HEADS_B, NOPE_B, ROPE_B, V_B = 16, 64, 32, 64
ROPE_THETA = 10000.0
CHUNK_C, GROUPS_C = 128, 8
CONV_W = 3
N_GROUPS_E, EXPERTS_PER_GROUP, TOPK_E = 4, 8, 2
N_EXPERTS = N_GROUPS_E * EXPERTS_PER_GROUP

NT_DIMS = (((1,), (1,)), ((), ()))
HIGHEST = lax.Precision.HIGHEST


def _cparams(sem, vmem_mb):
    return pltpu.CompilerParams(dimension_semantics=sem, vmem_limit_bytes=vmem_mb << 20)


def _dot(a, b):
    return jnp.dot(a.astype(BF16), b.astype(BF16), preferred_element_type=F32)


def _dot_nt(a, b):
    return lax.dot_general(a.astype(BF16), b.astype(BF16), NT_DIMS, preferred_element_type=F32)


def _rms(x, g):
    return x * lax.rsqrt(jnp.mean(x * x, axis=-1, keepdims=True) + EPS) * g


def _ada_norm(x, g, shift, scale):
    return _rms(x, g) * (1.0 + scale) + shift


def _mod_spec(m, ts):
    d = m.shape[-1]
    if m.shape[1] == 1:
        return pl.BlockSpec((1, 1, d), lambda b, j: (b, 0, 0))
    return pl.BlockSpec((1, ts, d), lambda b, j: (b, j, 0))


def _full_spec(a):
    nd = a.ndim
    return pl.BlockSpec(a.shape, lambda *_: (0,) * nd)


def _row_spec(a, ts):
    return pl.BlockSpec((1, ts) + a.shape[2:], lambda b, j: (b, j) + (0,) * (a.ndim - 2))


def _ada_kernel(c_ref, w_ref, b_ref, o_ref):
    c = c_ref[...]
    o_ref[0] = _dot(c * jax.nn.sigmoid(c), w_ref[0]) + b_ref[0]


def ada_mods(c_all, w_ada, b_ada):
    depth, d, n = w_ada.shape
    rows = c_all.shape[0]
    tn = 1536
    return pl.pallas_call(
        _ada_kernel,
        grid=(depth, n // tn),
        in_specs=[pl.BlockSpec((rows, d), lambda i, j: (0, 0)),
                  pl.BlockSpec((1, d, tn), lambda i, j: (i, 0, j)),
                  pl.BlockSpec((1, 1, tn), lambda i, j: (i, 0, j))],
        out_specs=pl.BlockSpec((1, rows, tn), lambda i, j: (i, 0, j)),
        out_shape=jax.ShapeDtypeStruct((depth, rows, n), F32),
        compiler_params=_cparams(("arbitrary", "arbitrary"), 40),
        name="ada_mods",
    )(c_all, w_ada, b_ada.reshape(depth, 1, n))


def _nmm_kernel(x_ref, g_ref, sh_ref, sc_ref, w_ref, *out_refs, offs):
    h = _ada_norm(x_ref[0], g_ref[...], sh_ref[0], sc_ref[0])
    y = _dot(h, w_ref[...])
    for o_ref, (lo, n) in zip(out_refs, offs):
        o_ref[0] = y[:, lo:lo + n].astype(o_ref.dtype)


def norm_mod_matmul(x, g, shift, scale, w, offs, dtypes, ts):
    b, s, d = x.shape
    outs = [jax.ShapeDtypeStruct((b, s, n), dt) for (_, n), dt in zip(offs, dtypes)]
    return pl.pallas_call(
        functools.partial(_nmm_kernel, offs=tuple(offs)),
        grid=(b, s // ts),
        in_specs=[_row_spec(x, ts), _full_spec(g), _mod_spec(shift, ts), _mod_spec(scale, ts), _full_spec(w)],
        out_specs=[pl.BlockSpec((1, ts, n), lambda bb, j: (bb, j, 0)) for _, n in offs],
        out_shape=outs,
        compiler_params=_cparams(("arbitrary", "arbitrary"), 48),
        name="norm_mod_matmul",
    )(x, g, shift, scale, w)


def _out_proj_kernel(x_ref, gate_ref, a_ref, *rest, n_heads):
    if n_heads:
        wv_ref, wo_ref, o_ref = rest
        a = a_ref[0]
        c = a.shape[1] // n_heads
        a = jnp.concatenate([_dot(a[:, h * c:(h + 1) * c], wv_ref[h]) for h in range(n_heads)], axis=1)
    else:
        wo_ref, o_ref = rest
        a = a_ref[0]
    o_ref[0] = x_ref[0] + gate_ref[0] * _dot(a, wo_ref[...])


def out_proj(x, gate, a, w_o, w_v=None, ts=512):
    b, s, d = x.shape
    ins = [x, gate, a] + ([w_v] if w_v is not None else []) + [w_o]
    specs = [_row_spec(x, ts), _mod_spec(gate, ts), _row_spec(a, ts)]
    specs += ([_full_spec(w_v)] if w_v is not None else []) + [_full_spec(w_o)]
    return pl.pallas_call(
        functools.partial(_out_proj_kernel, n_heads=0 if w_v is None else w_v.shape[0]),
        grid=(b, s // ts),
        in_specs=specs,
        out_specs=_row_spec(x, ts),
        out_shape=jax.ShapeDtypeStruct(x.shape, F32),
        compiler_params=_cparams(("arbitrary", "arbitrary"), 48),
        name="out_proj",
    )(*ins)


def _np_bucket(dist):
    n = np.maximum(dist, 0)
    max_exact = NUM_BUCKETS // 2
    ratio = np.log(np.maximum(n, 1).astype(np.float32) / max_exact) / math.log(MAX_DISTANCE / max_exact)
    large = np.minimum(max_exact + (ratio * (NUM_BUCKETS - max_exact)).astype(np.int32), NUM_BUCKETS - 1)
    return np.where(n < max_exact, n, large).astype(np.int32)


assert int(_np_bucket(np.array(BLOCK_A + 1))) == NUM_BUCKETS - 1


def _moba_prompt_kernel(far_ref, q_ref, k_ref, v_ref, bown_ref, badj_ref, o_ref,
                        kmbd_ref, mb_ref, m_ref, l_ref, acc_ref, *, nb):
    i = pl.program_id(1)
    bs = BLOCK_A
    scale = DH_A ** -0.5

    @pl.when(i == 0)
    def _():
        kmbd_ref[...] = jnp.zeros_like(kmbd_ref)
        for j in range(nb):
            km = jnp.mean(k_ref[0, j * bs:(j + 1) * bs, :], axis=0, keepdims=True)
            for hq in range(HQ_A):
                kvh = hq // GRP_A
                kmbd_ref[j * HQ_A + hq:j * HQ_A + hq + 1, hq * DH_A:(hq + 1) * DH_A] = km[:, kvh * DH_A:(kvh + 1) * DH_A]

    q = q_ref[0]
    gates = lax.dot_general(q, kmbd_ref[...], NT_DIMS, precision=HIGHEST, preferred_element_type=F32)
    lane = lax.broadcasted_iota(I32, gates.shape, 1)
    jidx = lane // HQ_A
    past = jidx < i
    gm = jnp.where(past, gates, NEG_INF)
    rank = jnp.zeros(gates.shape, I32)
    for k in range(1, nb):
        other = pltpu.roll(gm, HQ_A * k, axis=1)
        jo = jnp.where(jidx >= k, jidx - k, jidx - k + nb)
        beats = (other > gm) | ((other == gm) & (jo < jidx))
        rank = rank + beats.astype(I32)
    mb = jnp.where(past & (rank < TOPK_A), 0.0, NEG_INF).astype(F32)
    for j in range(nb):
        mb_ref[j] = mb[:, j * HQ_A:(j + 1) * HQ_A]

    def kv_block(j, h):
        lo = pl.multiple_of(j * bs, bs)
        return (k_ref[0, pl.ds(lo, bs), h * DH_A:(h + 1) * DH_A].astype(BF16),
                v_ref[0, pl.ds(lo, bs), h * DH_A:(h + 1) * DH_A].astype(BF16))

    for h in range(HKV_A):
        heads = [GRP_A * h + g for g in range(GRP_A)]
        q4 = (jnp.concatenate([q[:, hq * DH_A:(hq + 1) * DH_A] for hq in heads], axis=0) * scale).astype(BF16)

        k_own, v_own = kv_block(i, h)
        s = _dot_nt(q4, k_own) + bown_ref[GRP_A * h:GRP_A * (h + 1)].reshape(GRP_A * bs, bs)
        m0 = jnp.max(s, axis=1, keepdims=True)
        p = jnp.exp(s - m0)
        m_ref[...] = m0
        l_ref[...] = jnp.sum(p, axis=1, keepdims=True)
        acc_ref[...] = _dot(p, v_own)

        def update(j, bias):
            kj, vj = kv_block(j, h)
            sj = _dot_nt(q4, kj) + bias
            m_old = m_ref[...]
            m_new = jnp.maximum(m_old, jnp.max(sj, axis=1, keepdims=True))
            alpha = jnp.exp(m_old - m_new)
            pj = jnp.exp(sj - m_new)
            l_ref[...] = alpha * l_ref[...] + jnp.sum(pj, axis=1, keepdims=True)
            acc_ref[...] = alpha * acc_ref[...] + _dot(pj, vj)
            m_ref[...] = m_new

        def mask_col(j, extra):
            mj = mb_ref[j]
            return jnp.concatenate([mj[:, hq:hq + 1] + (extra[n] if extra else 0.0)
                                    for n, hq in enumerate(heads)], axis=0)

        @pl.when(i >= 1)
        def _():
            badj = badj_ref[GRP_A * h:GRP_A * (h + 1)].reshape(GRP_A * bs, bs)
            update(i - 1, badj + mask_col(i - 1, None))

        far = [far_ref[hq] for hq in heads]

        def far_body(j, c):
            update(j, mask_col(j, far))
            return c

        lax.fori_loop(0, jnp.maximum(i - 1, 0), far_body, 0)

        o = acc_ref[...] * (1.0 / l_ref[...])
        for g, hq in enumerate(heads):
            o_ref[0, :, hq * DH_A:(hq + 1) * DH_A] = o[g * bs:(g + 1) * bs].astype(o_ref.dtype)


def moba_prompt_attention(q, k, v, rel_bias):
    b, s, _ = q.shape
    nb = s // BLOCK_A
    assert s % BLOCK_A == 0 and nb * HQ_A == LANES and nb >= TOPK_A
    t = np.arange(BLOCK_A)[:, None]
    sk = np.arange(BLOCK_A)[None, :]
    bown = jnp.where(jnp.asarray(t >= sk)[None], rel_bias[_np_bucket(t - sk)].transpose(2, 0, 1), NEG_INF)
    badj = rel_bias[_np_bucket(BLOCK_A + t - sk)].transpose(2, 0, 1)
    far = rel_bias[NUM_BUCKETS - 1]
    kv_spec = pl.BlockSpec((1, s, HKV_A * DH_A), lambda bb, i: (bb, 0, 0))
    return pl.pallas_call(
        functools.partial(_moba_prompt_kernel, nb=nb),
        grid=(b, nb),
        in_specs=[pl.BlockSpec(memory_space=pltpu.SMEM),
                  pl.BlockSpec((1, BLOCK_A, HQ_A * DH_A), lambda bb, i: (bb, i, 0)),
                  kv_spec, kv_spec, _full_spec(bown), _full_spec(badj)],
        out_specs=pl.BlockSpec((1, BLOCK_A, HQ_A * DH_A), lambda bb, i: (bb, i, 0)),
        out_shape=jax.ShapeDtypeStruct((b, s, HQ_A * DH_A), BF16),
        scratch_shapes=[pltpu.VMEM((nb * HQ_A, HQ_A * DH_A), F32),
                        pltpu.VMEM((nb, BLOCK_A, HQ_A), F32),
                        pltpu.VMEM((GRP_A * BLOCK_A, 1), F32),
                        pltpu.VMEM((GRP_A * BLOCK_A, 1), F32),
                        pltpu.VMEM((GRP_A * BLOCK_A, DH_A), F32)],
        compiler_params=_cparams(("arbitrary", "arbitrary"), 48),
        name="moba_prompt_attention",
    )(far, q, k, v, bown, badj)


def _page_copies(cache_hbm, buf, sem, pt_ref, bb, slot, n_pages):
    return [pltpu.make_async_copy(cache_hbm.at[pt_ref[bb * n_pages + p]], buf.at[slot, p], sem.at[slot])
            for p in range(n_pages)]


def _pages_wait(cache_hbm, buf, sem, slot, n_pages):
    pltpu.make_async_copy(cache_hbm.at[pl.ds(0, n_pages)], buf.at[slot], sem.at[slot]).wait()


def _softmax_update(carry, s, v):
    m_old, l_old, acc = carry
    m_new = jnp.maximum(m_old, jnp.max(s, axis=1, keepdims=True))
    alpha = jnp.exp(m_old - m_new)
    p = jnp.exp(s - m_new)
    return m_new, alpha * l_old + jnp.sum(p, axis=1, keepdims=True), alpha * acc + _dot(p, v)


def _new_token_partials(qs, kn, vn, bnew, n_new):
    qf = qs.astype(BF16).astype(F32)
    kf = kn.astype(BF16).astype(F32)
    vf = vn.astype(BF16).astype(F32)
    cols = [jnp.sum(qf * kf[s:s + 1, :], axis=1, keepdims=True) + bnew[:, s:s + 1] for s in range(n_new)]
    m = functools.reduce(jnp.maximum, cols)
    ps = [jnp.exp(c - m) for c in cols]
    l = functools.reduce(jnp.add, ps)
    acc = functools.reduce(jnp.add, [p.astype(BF16).astype(F32) * vf[s:s + 1, :] for s, p in enumerate(ps)])
    return m, l, acc


def _moba_sample_kernel(pt_ref, q_ref, kn_ref, vn_ref, far_ref, blast_ref, bnew_ref, ck_hbm, cv_hbm, o_ref,
                        kbuf, vbuf, ksem, vsem, qbd_ref, km_ref, *, n_pages, n_new):
    b = pl.program_id(0)
    slot = b % 2
    nb = n_pages * kbuf.shape[2] // BLOCK_A
    ppb = n_pages // nb
    scale = DH_A ** -0.5

    def fetch(bb, sl):
        for c in _page_copies(ck_hbm, kbuf, ksem, pt_ref, bb, sl, n_pages):
            c.start()
        for c in _page_copies(cv_hbm, vbuf, vsem, pt_ref, bb, sl, n_pages):
            c.start()

    @pl.when(b == 0)
    def _():
        km_ref[...] = jnp.zeros_like(km_ref)
        fetch(0, 0)

    @pl.when(b + 1 < pl.num_programs(0))
    def _():
        fetch(b + 1, 1 - slot)

    q = q_ref[0]
    qbd_ref[...] = jnp.zeros_like(qbd_ref)
    for hq in range(HQ_A):
        kvh = hq // GRP_A
        qbd_ref[hq * n_new:(hq + 1) * n_new, kvh * DH_A:(kvh + 1) * DH_A] = q[:, hq * DH_A:(hq + 1) * DH_A]
    qbd = qbd_ref[...]
    qs = qbd * scale

    carry = _new_token_partials(qs, kn_ref[0], vn_ref[0], bnew_ref[...], n_new)

    _pages_wait(ck_hbm, kbuf, ksem, slot, n_pages)

    def key_block(buf, j):
        blk = buf[slot, pl.ds(j * ppb, ppb)]
        return blk.reshape(BLOCK_A, blk.shape[-1])

    def km_body(j, c):
        km_ref[pl.ds(j, 1), :] = jnp.sum(key_block(kbuf, j), axis=0, keepdims=True) * (1.0 / BLOCK_A)
        return c

    lax.fori_loop(0, nb, km_body, 0)

    gates = lax.dot_general(qbd, km_ref[...], NT_DIMS, precision=HIGHEST, preferred_element_type=F32)
    lane = lax.broadcasted_iota(I32, gates.shape, 1)
    gcur = jnp.where(lane < nb, gates, NEG_INF)
    sel = jnp.zeros(gates.shape, jnp.bool_)
    for _ in range(TOPK_A):
        mx = jnp.max(gcur, axis=1, keepdims=True)
        idx = jnp.min(jnp.where(gcur == mx, lane, LANES), axis=1, keepdims=True)
        pick = lane == idx
        sel = sel | (pick & (mx > 0.5 * NEG_INF))
        gcur = jnp.where(pick, -3e38, gcur)
    selbias = jnp.where(sel, 0.0, NEG_INF).astype(F32)

    def sel_col(j):
        return jnp.sum(jnp.where(lane == j, selbias, 0.0), axis=1, keepdims=True)

    _pages_wait(cv_hbm, vbuf, vsem, slot, n_pages)
    far = far_ref[...]

    def far_body(j, c):
        s = _dot_nt(qs, key_block(kbuf, j)) + (sel_col(j) + far)
        return _softmax_update(c, s, key_block(vbuf, j))

    carry = lax.fori_loop(0, nb - 1, far_body, carry)
    s = _dot_nt(qs, key_block(kbuf, nb - 1)) + (blast_ref[...] + sel_col(nb - 1))
    _, l, acc = _softmax_update(carry, s, key_block(vbuf, nb - 1))

    o = acc * (1.0 / l)
    for hq in range(HQ_A):
        kvh = hq // GRP_A
        o_ref[0, :, hq * DH_A:(hq + 1) * DH_A] = (
            o[hq * n_new:(hq + 1) * n_new, kvh * DH_A:(kvh + 1) * DH_A].astype(o_ref.dtype))


def moba_sample_attention(q, k_new, v_new, cache_k, cache_v, page_table, rel_bias):
    db, t_new, _ = q.shape
    n_pages = page_table.shape[1]
    page = cache_k.shape[1]
    past = n_pages * page
    assert past % BLOCK_A == 0 and BLOCK_A % page == 0 and t_new <= BLOCK_A
    nb = past // BLOCK_A
    assert TOPK_A <= nb <= LANES
    tt = np.arange(t_new)[:, None]
    last = rel_bias[_np_bucket(BLOCK_A + tt - np.arange(BLOCK_A)[None, :])]
    blast = last.transpose(2, 0, 1).reshape(HQ_A * t_new, BLOCK_A)
    ts = np.arange(t_new)[None, :]
    bnew = jnp.where(jnp.asarray(tt >= ts)[None], rel_bias[_np_bucket(tt - ts)].transpose(2, 0, 1), NEG_INF)
    bnew = bnew.reshape(HQ_A * t_new, t_new)
    far = jnp.repeat(rel_bias[NUM_BUCKETS - 1], t_new).reshape(HQ_A * t_new, 1)
    rows = HQ_A * t_new
    kw = HKV_A * DH_A
    grid_spec = pltpu.PrefetchScalarGridSpec(
        num_scalar_prefetch=1,
        grid=(db,),
        in_specs=[pl.BlockSpec((1, t_new, HQ_A * DH_A), lambda b, pt: (b, 0, 0)),
                  pl.BlockSpec((1, t_new, kw), lambda b, pt: (b, 0, 0)),
                  pl.BlockSpec((1, t_new, kw), lambda b, pt: (b, 0, 0)),
                  pl.BlockSpec((rows, 1), lambda b, pt: (0, 0)),
                  pl.BlockSpec((rows, BLOCK_A), lambda b, pt: (0, 0)),
                  pl.BlockSpec((rows, t_new), lambda b, pt: (0, 0)),
                  pl.BlockSpec(memory_space=pl.ANY),
                  pl.BlockSpec(memory_space=pl.ANY)],
        out_specs=pl.BlockSpec((1, t_new, HQ_A * DH_A), lambda b, pt: (b, 0, 0)),
        scratch_shapes=[pltpu.VMEM((2, n_pages, page, kw), F32),
                        pltpu.VMEM((2, n_pages, page, kw), F32),
                        pltpu.SemaphoreType.DMA((2,)),
                        pltpu.SemaphoreType.DMA((2,)),
                        pltpu.VMEM((rows, kw), F32),
                        pltpu.VMEM((LANES, kw), F32)])
    return pl.pallas_call(
        functools.partial(_moba_sample_kernel, n_pages=n_pages, n_new=t_new),
        grid_spec=grid_spec,
        out_shape=jax.ShapeDtypeStruct((db, t_new, HQ_A * DH_A), BF16),
        compiler_params=_cparams(("arbitrary",), 56),
        name="moba_sample_attention",
    )(page_table.reshape(-1), q, k_new, v_new, far, blast, bnew, cache_k, cache_v)


def _mla_in_kernel(x_ref, g_ref, sh_ref, sc_ref, win_ref, qg_ref, wqb_ref, kvg_ref, wn_ref,
                   cq_ref, sq_ref, q_ref, kv_ref, ckv_ref, kpe_ref):
    nope_w = HEADS_B * NOPE_B
    rope_w = HEADS_B * ROPE_B
    lat = kv_ref.shape[-1] - ROPE_B
    qa_w = qg_ref.shape[-1]
    scale = (NOPE_B + ROPE_B) ** -0.5
    h = _ada_norm(x_ref[0], g_ref[...], sh_ref[0], sc_ref[0])
    y = _dot(h, win_ref[...])
    qn = _rms(y[:, :qa_w], qg_ref[...])
    ckv = _rms(y[:, qa_w:qa_w + lat], kvg_ref[...])
    cos = cq_ref[0]
    sin = sq_ref[0]
    k_lo = qa_w + lat
    kpe = y[:, k_lo:k_lo + ROPE_B] * cos[:, :ROPE_B] + y[:, k_lo + LANES:k_lo + LANES + ROPE_B] * sin[:, :ROPE_B]
    ckv_ref[0] = ckv
    kpe_ref[0] = kpe
    kv_ref[0, :, :lat] = ckv.astype(BF16)
    kv_ref[0, :, lat:] = kpe.astype(BF16)
    qq = _dot(qn, wqb_ref[...])
    q_pe = (qq[:, nope_w:nope_w + rope_w] * cos + qq[:, nope_w + rope_w:] * sin) * scale
    for hd in range(HEADS_B):
        q_lat = _dot(qq[:, hd * NOPE_B:(hd + 1) * NOPE_B], wn_ref[hd]) * scale
        q_ref[0, hd, :, :lat] = q_lat.astype(BF16)
        q_ref[0, hd, :, lat:] = q_pe[:, hd * ROPE_B:(hd + 1) * ROPE_B].astype(BF16)


def _rope_swap_cols(w, n_rope):
    half = n_rope // 2
    return jnp.concatenate([w[..., half:], w[..., :half]], axis=-1)


def mla_in(x, g, shift, scale, w_in, q_norm_g, w_qb, kv_norm_g, w_kvb, pos, ts):
    b, s, d = x.shape
    qa_w = q_norm_g.shape[0]
    lat = kv_norm_g.shape[0]
    kpe_w = w_in[:, qa_w + lat:]
    pad = jnp.zeros((d, LANES - ROPE_B), w_in.dtype)
    w_in_ext = jnp.concatenate([w_in[:, :qa_w + lat], kpe_w, pad, _rope_swap_cols(kpe_w, ROPE_B), pad], axis=1).astype(BF16)
    wq = w_qb.reshape(qa_w, HEADS_B, NOPE_B + ROPE_B)
    wq_rope = wq[:, :, NOPE_B:]
    w_qb_ext = jnp.concatenate([wq[:, :, :NOPE_B].reshape(qa_w, -1), wq_rope.reshape(qa_w, -1),
                                _rope_swap_cols(wq_rope, ROPE_B).reshape(qa_w, -1)], axis=1).astype(BF16)
    w_nope = w_kvb[:, :, :NOPE_B].transpose(1, 2, 0).astype(BF16)
    half = ROPE_B // 2
    inv = ROPE_THETA ** (-jnp.arange(half, dtype=F32) / half)
    ang = pos.astype(F32)[..., None] * inv
    cos, sin = jnp.cos(ang), jnp.sin(ang)
    cos_t = jnp.tile(jnp.concatenate([cos, cos], axis=-1), (1, 1, HEADS_B))
    sin_t = jnp.tile(jnp.concatenate([-sin, sin], axis=-1), (1, 1, HEADS_B))
    outs = [jax.ShapeDtypeStruct((b, HEADS_B, s, lat + ROPE_B), BF16),
            jax.ShapeDtypeStruct((b, s, lat + ROPE_B), BF16),
            jax.ShapeDtypeStruct((b, s, lat), F32),
            jax.ShapeDtypeStruct((b, s, ROPE_B), F32)]
    qg = q_norm_g.reshape(1, -1)
    kvg = kv_norm_g.reshape(1, -1)
    return pl.pallas_call(
        _mla_in_kernel,
        grid=(b, s // ts),
        in_specs=[_row_spec(x, ts), _full_spec(g), _mod_spec(shift, ts), _mod_spec(scale, ts),
                  _full_spec(w_in_ext), _full_spec(qg), _full_spec(w_qb_ext), _full_spec(kvg), _full_spec(w_nope),
                  _row_spec(cos_t, ts), _row_spec(sin_t, ts)],
        out_specs=[pl.BlockSpec((1, HEADS_B, ts, lat + ROPE_B), lambda bb, j: (bb, 0, j, 0)),
                   pl.BlockSpec((1, ts, lat + ROPE_B), lambda bb, j: (bb, j, 0)),
                   pl.BlockSpec((1, ts, lat), lambda bb, j: (bb, j, 0)),
                   pl.BlockSpec((1, ts, ROPE_B), lambda bb, j: (bb, j, 0))],
        out_shape=outs,
        compiler_params=_cparams(("arbitrary", "arbitrary"), 48),
        name="mla_in",
    )(x, g, shift, scale, w_in_ext, qg, w_qb_ext, kvg, w_nope, cos_t, sin_t)


def _mla_prompt_kernel(q_ref, kv_ref, o_ref, m_ref, l_ref, acc_ref, *, tq, hc):
    i = pl.program_id(1)
    lat = acc_ref.shape[-1]
    rows = hc * tq
    row_t = lax.broadcasted_iota(I32, (rows, tq), 0) % tq
    col_s = lax.broadcasted_iota(I32, (rows, tq), 1)
    causal = col_s <= row_t

    for c in range(HEADS_B // hc):
        q = q_ref[0, c * hc:(c + 1) * hc].reshape(rows, q_ref.shape[-1])
        m_ref[...] = jnp.full(m_ref.shape, NEG_INF, F32)
        l_ref[...] = jnp.zeros_like(l_ref)
        acc_ref[...] = jnp.zeros_like(acc_ref)

        def update(j, masked):
            kv = kv_ref[0, pl.ds(pl.multiple_of(j * tq, tq), tq), :]
            s = lax.dot_general(q, kv, NT_DIMS, preferred_element_type=F32)
            if masked:
                s = jnp.where(causal, s, NEG_INF)
            m_old = m_ref[...]
            m_new = jnp.maximum(m_old, jnp.max(s, axis=1, keepdims=True))
            alpha = jnp.exp(m_old - m_new)
            p = jnp.exp(s - m_new)
            l_ref[...] = alpha * l_ref[...] + jnp.sum(p, axis=1, keepdims=True)
            acc_ref[...] = alpha * acc_ref[...] + jnp.dot(p.astype(BF16), kv[:, :lat], preferred_element_type=F32)
            m_ref[...] = m_new

        def body(j, cc):
            update(j, False)
            return cc

        update(i, True)
        lax.fori_loop(0, i, body, 0)
        o = acc_ref[...] * (1.0 / l_ref[...])
        for hl in range(hc):
            hd = c * hc + hl
            o_ref[0, :, hd * lat:(hd + 1) * lat] = o[hl * tq:(hl + 1) * tq].astype(o_ref.dtype)


def mla_prompt_attention(q, kv, tq=256, hc=4):
    b, nh, s, w = q.shape
    lat = w - ROPE_B
    return pl.pallas_call(
        functools.partial(_mla_prompt_kernel, tq=tq, hc=hc),
        grid=(b, s // tq),
        in_specs=[pl.BlockSpec((1, nh, tq, w), lambda bb, i: (bb, 0, i, 0)),
                  pl.BlockSpec((1, s, w), lambda bb, i: (bb, 0, 0))],
        out_specs=pl.BlockSpec((1, tq, nh * lat), lambda bb, i: (bb, i, 0)),
        out_shape=jax.ShapeDtypeStruct((b, s, nh * lat), BF16),
        scratch_shapes=[pltpu.VMEM((hc * tq, 1), F32), pltpu.VMEM((hc * tq, 1), F32),
                        pltpu.VMEM((hc * tq, lat), F32)],
        compiler_params=_cparams(("arbitrary", "arbitrary"), 48),
        name="mla_prompt_attention",
    )(q, kv)


def _mla_sample_kernel(pt_ref, q_ref, cn_ref, pn_ref, bnew_ref, cc_hbm, cp_hbm, o_ref,
                       cbuf, pbuf, csem, psem, *, n_pages, n_new, chunk_pages):
    b = pl.program_id(0)
    slot = b % 2
    lat = cbuf.shape[-1]
    page = cbuf.shape[2]

    def fetch(bb, sl):
        for c in _page_copies(cc_hbm, cbuf, csem, pt_ref, bb, sl, n_pages):
            c.start()
        for c in _page_copies(cp_hbm, pbuf, psem, pt_ref, bb, sl, n_pages):
            c.start()

    @pl.when(b == 0)
    def _():
        fetch(0, 0)

    @pl.when(b + 1 < pl.num_programs(0))
    def _():
        fetch(b + 1, 1 - slot)

    q = q_ref[0]
    ql, qp = q[:, :lat], q[:, lat:]
    qf, qpf = ql.astype(F32), qp.astype(F32)
    cn = cn_ref[0].astype(BF16).astype(F32)
    pn = pn_ref[0].astype(BF16).astype(F32)
    bnew = bnew_ref[...]
    cols = [jnp.sum(qf * cn[s:s + 1, :], axis=1, keepdims=True) + jnp.sum(qpf * pn[s:s + 1, :], axis=1, keepdims=True)
            + bnew[:, s:s + 1] for s in range(n_new)]
    m = functools.reduce(jnp.maximum, cols)
    ps = [jnp.exp(c - m) for c in cols]
    l = functools.reduce(jnp.add, ps)
    acc = functools.reduce(jnp.add, [p.astype(BF16).astype(F32) * cn[s:s + 1, :] for s, p in enumerate(ps)])

    _pages_wait(cc_hbm, cbuf, csem, slot, n_pages)
    _pages_wait(cp_hbm, pbuf, psem, slot, n_pages)

    def body(j, carry):
        ck = cbuf[slot, pl.ds(j * chunk_pages, chunk_pages)].reshape(chunk_pages * page, lat)
        pk = pbuf[slot, pl.ds(j * chunk_pages, chunk_pages)].reshape(chunk_pages * page, pbuf.shape[-1])
        s = _dot_nt(ql, ck) + _dot_nt(qp, pk)
        return _softmax_update(carry, s, ck)

    _, l, acc = lax.fori_loop(0, n_pages // chunk_pages, body, (m, l, acc))
    o_ref[0] = (acc * (1.0 / l)).astype(o_ref.dtype)


def mla_sample_attention(q, ckv_new, kpe_new, cache_ckv, cache_kpe, page_table):
    db, rows, w = q.shape
    t_new = ckv_new.shape[1]
    lat = ckv_new.shape[2]
    n_pages = page_table.shape[1]
    page = cache_ckv.shape[1]
    tt = np.arange(t_new)[:, None]
    ts = np.arange(t_new)[None, :]
    bnew = jnp.asarray(np.tile(np.where(tt >= ts, 0.0, NEG_INF).astype(np.float32), (rows // t_new, 1)))
    grid_spec = pltpu.PrefetchScalarGridSpec(
        num_scalar_prefetch=1,
        grid=(db,),
        in_specs=[pl.BlockSpec((1, rows, w), lambda b, pt: (b, 0, 0)),
                  pl.BlockSpec((1, t_new, lat), lambda b, pt: (b, 0, 0)),
                  pl.BlockSpec((1, t_new, ROPE_B), lambda b, pt: (b, 0, 0)),
                  pl.BlockSpec((rows, t_new), lambda b, pt: (0, 0)),
                  pl.BlockSpec(memory_space=pl.ANY),
                  pl.BlockSpec(memory_space=pl.ANY)],
        out_specs=pl.BlockSpec((1, rows, lat), lambda b, pt: (b, 0, 0)),
        scratch_shapes=[pltpu.VMEM((2, n_pages, page, lat), F32),
                        pltpu.VMEM((2, n_pages, page, ROPE_B), F32),
                        pltpu.SemaphoreType.DMA((2,)),
                        pltpu.SemaphoreType.DMA((2,))])
    return pl.pallas_call(
        functools.partial(_mla_sample_kernel, n_pages=n_pages, n_new=t_new, chunk_pages=2),
        grid_spec=grid_spec,
        out_shape=jax.ShapeDtypeStruct((db, rows, lat), BF16),
        compiler_params=_cparams(("arbitrary",), 40),
        name="mla_sample_attention",
    )(page_table.reshape(-1), q, ckv_new, kpe_new, bnew, cache_ckv, cache_kpe)


def _gmlp_kernel(x_ref, g_ref, sh_ref, sc_ref, gate_ref, win_ref, lng_ref, lnb_ref, wmix_ref, bmix_ref, wout_ref,
                 o_ref, *v_out):
    dv = lng_ref.shape[-1]
    gw = dv // GROUPS_C
    ts = x_ref.shape[1]
    x = x_ref[0]
    h = _ada_norm(x, g_ref[...], sh_ref[0], sc_ref[0])
    y = _dot(h, win_ref[...])
    a = 0.5 * y * (1.0 + lax.erf(y * (1.0 / math.sqrt(2.0))))
    u, v = a[:, :dv], a[:, dv:]
    mu = jnp.mean(v, axis=-1, keepdims=True)
    vc = v - mu
    var = jnp.mean(vc * vc, axis=-1, keepdims=True)
    v = vc * lax.rsqrt(var + EPS) * lng_ref[...] + lnb_ref[...]
    if v_out:
        v_out[0][0] = v
    r = lax.broadcasted_iota(I32, (CHUNK_C, CHUNK_C), 0)
    c = lax.broadcasted_iota(I32, (CHUNK_C, CHUNK_C), 1)
    tril = r >= c
    wm = [jnp.where(tril, wmix_ref[gi], 0.0).astype(BF16) for gi in range(GROUPS_C)]
    vb = v.astype(BF16)
    outs = []
    for ci in range(ts // CHUNK_C):
        rows = slice(ci * CHUNK_C, (ci + 1) * CHUNK_C)
        mixed = jnp.concatenate(
            [jnp.dot(wm[gi], vb[rows, gi * gw:(gi + 1) * gw], preferred_element_type=F32) for gi in range(GROUPS_C)],
            axis=1) + bmix_ref[...]
        outs.append(_dot(u[rows] * mixed, wout_ref[...]))
    o_ref[0] = x + gate_ref[0] * jnp.concatenate(outs, axis=0)


def gmlp(x, g, shift, scale, gate, w_in, ln_g, ln_b, wmix, bmix, w_out, want_v, ts=256):
    b, s, d = x.shape
    dv = ln_g.shape[0]
    lng, lnb = ln_g.reshape(1, dv), ln_b.reshape(1, dv)
    outs = [jax.ShapeDtypeStruct(x.shape, F32)]
    out_specs = [_row_spec(x, ts)]
    if want_v:
        outs.append(jax.ShapeDtypeStruct((b, s, dv), F32))
        out_specs.append(pl.BlockSpec((1, ts, dv), lambda bb, j: (bb, j, 0)))
    res = pl.pallas_call(
        _gmlp_kernel,
        grid=(b, s // ts),
        in_specs=[_row_spec(x, ts), _full_spec(g), _mod_spec(shift, ts), _mod_spec(scale, ts), _mod_spec(gate, ts),
                  _full_spec(w_in), _full_spec(lng), _full_spec(lnb), _full_spec(wmix), _full_spec(bmix),
                  _full_spec(w_out)],
        out_specs=out_specs,
        out_shape=outs,
        compiler_params=_cparams(("arbitrary", "arbitrary"), 56),
        name="gmlp",
    )(x, g, shift, scale, gate, w_in, lng, lnb, wmix, bmix, w_out)
    return res if want_v else (res[0], None)


def _conv_kernel(x_ref, g_ref, sh_ref, sc_ref, gate_ref, win_ref, cw_ref, wout_ref, *rest, period):
    ts, d = x_ref.shape[1], x_ref.shape[2]
    if period:
        f1_ref, f2_ref, o_ref, zc_ref, zp_ref = rest
    else:
        o_ref, st_ref, zp_ref = rest
    x = x_ref[0]
    h = _ada_norm(x, g_ref[...], sh_ref[0], sc_ref[0])
    y = _dot(h, win_ref[...])
    gate_b, zc = y[:, :d], y[:, d:2 * d] * y[:, 2 * d:]
    if period:
        zp_ref[0:SUBLANES] = jnp.zeros((SUBLANES, d), F32)
    else:
        @pl.when(pl.program_id(1) == 0)
        def _():
            zp_ref[0:SUBLANES] = jnp.zeros((SUBLANES, d), F32)
    zp_ref[SUBLANES:] = zc
    r1 = zp_ref[SUBLANES - 1:SUBLANES - 1 + ts]
    r2 = zp_ref[SUBLANES - 2:SUBLANES - 2 + ts]
    if period:
        t = lax.broadcasted_iota(I32, (ts, 1), 0) % period
        r1 = jnp.where(t >= 1, r1, f1_ref[0])
        r2 = jnp.where(t >= 2, r2, f2_ref[0])
        zc_ref[0] = zc
    else:
        st_ref[0] = zc[ts - (CONV_W - 1):]
        zp_ref[0:SUBLANES] = zc[ts - SUBLANES:]
    cw = cw_ref[...]
    yc = cw[0:1] * r2 + cw[1:2] * r1 + cw[2:3] * zc
    o_ref[0] = x + gate_ref[0] * _dot(gate_b * yc, wout_ref[...])


def short_conv(x, g, shift, scale, gate, w_in, conv_w, w_out, fix=None, period=0, ts=512):
    b, s, d = x.shape
    assert CONV_W == 3
    ins = [x, g, shift, scale, gate, w_in, conv_w, w_out]
    specs = [_row_spec(x, ts), _full_spec(g), _mod_spec(shift, ts), _mod_spec(scale, ts), _mod_spec(gate, ts),
             _full_spec(w_in), _full_spec(conv_w), _full_spec(w_out)]
    if period:
        ins += list(fix)
        specs += [_row_spec(x, ts), _row_spec(x, ts)]
        outs = [jax.ShapeDtypeStruct(x.shape, F32), jax.ShapeDtypeStruct(x.shape, F32)]
        out_specs = [_row_spec(x, ts), _row_spec(x, ts)]
    else:
        outs = [jax.ShapeDtypeStruct(x.shape, F32), jax.ShapeDtypeStruct((b, CONV_W - 1, d), F32)]
        out_specs = [_row_spec(x, ts), pl.BlockSpec((1, CONV_W - 1, d), lambda bb, j: (bb, 0, 0))]
    return pl.pallas_call(
        functools.partial(_conv_kernel, period=period),
        grid=(b, s // ts),
        in_specs=specs,
        out_specs=out_specs,
        out_shape=outs,
        scratch_shapes=[pltpu.VMEM((SUBLANES + ts, d), F32)],
        compiler_params=_cparams(("arbitrary", "arbitrary"), 56),
        name="short_conv",
    )(*ins)


def _router_kernel(x_ref, g_ref, sh_ref, sc_ref, whi_ref, wlo_ref, bias_ref, h_ref, route_ref):
    ts = x_ref.shape[1]
    h = _ada_norm(x_ref[0], g_ref[...], sh_ref[0], sc_ref[0])
    for s in range(SUBLANES):
        h_ref[pl.ds(s, ts, stride=SUBLANES), :] = h[:, s * LANES:(s + 1) * LANES]
    h_hi = h.astype(BF16)
    h_lo = (h - h_hi.astype(F32)).astype(BF16)
    logits = (jnp.dot(h_hi, whi_ref[...], preferred_element_type=F32)
              + jnp.dot(h_lo, whi_ref[...], preferred_element_type=F32)
              + jnp.dot(h_hi, wlo_ref[...], preferred_element_type=F32)) + bias_ref[...]
    lane = lax.broadcasted_iota(I32, logits.shape, 1)
    low = -3e38
    is_g = lane < N_GROUPS_E
    lg = jnp.where(is_g, logits, low)
    mg = jnp.max(lg, axis=1, keepdims=True)
    g_top = jnp.min(jnp.where(lg == mg, lane, LANES), axis=1, keepdims=True)
    p_g = 1.0 / jnp.sum(jnp.where(is_g, jnp.exp(lg - mg), 0.0), axis=1, keepdims=True)
    e_lo = N_GROUPS_E + g_top * EXPERTS_PER_GROUP
    le = jnp.where((lane >= e_lo) & (lane < e_lo + EXPERTS_PER_GROUP), logits, low)
    m1 = jnp.max(le, axis=1, keepdims=True)
    i1 = jnp.min(jnp.where(le == m1, lane, LANES), axis=1, keepdims=True)
    le2 = jnp.where(lane == i1, low, le)
    m2 = jnp.max(le2, axis=1, keepdims=True)
    i2 = jnp.min(jnp.where(le2 == m2, lane, LANES), axis=1, keepdims=True)
    e2 = jnp.exp(m2 - m1)
    w1 = p_g / (1.0 + e2)
    w2 = p_g * e2 / (1.0 + e2)
    route = jnp.where(lane == 0, (i1 - N_GROUPS_E).astype(F32),
                      jnp.where(lane == 1, (i2 - N_GROUPS_E).astype(F32),
                                jnp.where(lane == 2, w1, jnp.where(lane == 3, w2, 0.0))))
    route_ref[...] = route


def moe_router(x, g, shift, scale, w_group, b_group, w_expert, b_expert, ts=512):
    b, s, d = x.shape
    n = b * s
    w = jnp.concatenate([w_group, w_expert], axis=1)
    w = jnp.pad(w, ((0, 0), (0, LANES - w.shape[1])))
    w_hi = w.astype(BF16)
    w_lo = (w - w_hi.astype(F32)).astype(BF16)
    bias = jnp.pad(jnp.concatenate([b_group, b_expert]), (0, LANES - N_GROUPS_E - N_EXPERTS)).reshape(1, LANES)
    nj = s // ts
    return pl.pallas_call(
        _router_kernel,
        grid=(b, nj),
        in_specs=[_row_spec(x, ts), _full_spec(g), _mod_spec(shift, ts), _mod_spec(scale, ts),
                  _full_spec(w_hi), _full_spec(w_lo), _full_spec(bias)],
        out_specs=[pl.BlockSpec((ts * SUBLANES, LANES), lambda bb, j: (bb * nj + j, 0)),
                   pl.BlockSpec((ts, LANES), lambda bb, j: (bb * nj + j, 0))],
        out_shape=[jax.ShapeDtypeStruct((n * SUBLANES, LANES), F32), jax.ShapeDtypeStruct((n, LANES), F32)],
        compiler_params=_cparams(("arbitrary", "arbitrary"), 40),
        name="moe_router",
    )(x, g, shift, scale, w_hi, w_lo, bias)


def _row_copy(src_hbm, t, buf, slot, r, sem):
    return pltpu.make_async_copy(src_hbm.at[pl.ds(pl.multiple_of(t * SUBLANES, SUBLANES), SUBLANES), :],
                                 buf.at[slot, pl.ds(r * SUBLANES, SUBLANES), :], sem.at[slot])


def _rows_wait(src_hbm, buf, slot, sem):
    pltpu.make_async_copy(src_hbm.at[pl.ds(0, buf.shape[1]), :], buf.at[slot], sem.at[slot]).wait()


def _gathered_rows(buf, slot, n_rows, start, step):
    return jnp.concatenate([buf[slot, pl.ds(start * SUBLANES + s, n_rows, stride=step * SUBLANES), :]
                            for s in range(SUBLANES)], axis=1)


def _expert_kernel(be_ref, cur_ref, nxt_ref, h_hbm, wgu_ref, wd_ref, y_ref, xbuf, sem, *, rows):
    i = pl.program_id(0)
    slot = i % 2

    def fetch(src_ref, sl):
        for r in range(rows):
            _row_copy(h_hbm, src_ref[0, 0, r], xbuf, sl, r, sem).start()

    @pl.when(i == 0)
    def _():
        fetch(cur_ref, 0)

    @pl.when(i + 1 < pl.num_programs(0))
    def _():
        fetch(nxt_ref, 1 - slot)

    _rows_wait(h_hbm, xbuf, slot, sem)
    x = _gathered_rows(xbuf, slot, rows, 0, 1)
    ab = _dot(x, wgu_ref[0])
    f = ab.shape[1] // 2
    a, bb = ab[:, :f], ab[:, f:]
    y = _dot(a * jax.nn.sigmoid(a) * bb, wd_ref[0])
    for s in range(SUBLANES):
        y_ref[pl.ds(s, rows, stride=SUBLANES), :] = y[:, s * LANES:(s + 1) * LANES]


def moe_experts(h_rows, blk_e, src_token, w_gate_up, w_down, rows):
    n_blk = blk_e.shape[0]
    ne, d, f2 = w_gate_up.shape
    src3 = src_token.reshape(n_blk, 1, rows)
    smem_blk = lambda f: pl.BlockSpec((1, 1, rows), f, memory_space=pltpu.SMEM)
    grid_spec = pltpu.PrefetchScalarGridSpec(
        num_scalar_prefetch=1,
        grid=(n_blk,),
        in_specs=[smem_blk(lambda i, be: (i, 0, 0)),
                  smem_blk(lambda i, be: (jnp.minimum(i + 1, n_blk - 1), 0, 0)),
                  pl.BlockSpec(memory_space=pl.ANY),
                  pl.BlockSpec((1, d, f2), lambda i, be: (be[i], 0, 0)),
                  pl.BlockSpec((1, f2 // 2, d), lambda i, be: (be[i], 0, 0))],
        out_specs=pl.BlockSpec((rows * SUBLANES, LANES), lambda i, be: (i, 0)),
        scratch_shapes=[pltpu.VMEM((2, rows * SUBLANES, LANES), F32), pltpu.SemaphoreType.DMA((2,))])
    return pl.pallas_call(
        functools.partial(_expert_kernel, rows=rows),
        grid_spec=grid_spec,
        out_shape=jax.ShapeDtypeStruct((n_blk * rows * SUBLANES, LANES), F32),
        compiler_params=_cparams(("arbitrary",), 40),
        name="moe_experts",
    )(blk_e, src3, src3, h_rows, w_gate_up, w_down)


def _combine_kernel(cur_ref, nxt_ref, x_ref, gate_ref, route_ref, y_hbm, *rest, ts, final):
    if final:
        fg_ref, o_ref, ybuf, sem = rest
    else:
        o_ref, ybuf, sem = rest
    i = pl.program_id(0) * pl.num_programs(1) + pl.program_id(1)
    n = pl.num_programs(0) * pl.num_programs(1)
    slot = i % 2

    def fetch(src_ref, sl):
        for r in range(TOPK_E * ts):
            _row_copy(y_hbm, src_ref[0, 0, r], ybuf, sl, r, sem).start()

    @pl.when(i == 0)
    def _():
        fetch(cur_ref, 0)

    @pl.when(i + 1 < n)
    def _():
        fetch(nxt_ref, 1 - slot)

    _rows_wait(y_hbm, ybuf, slot, sem)
    route = route_ref[...]
    y = (route[:, 2:3] * _gathered_rows(ybuf, slot, ts, 0, TOPK_E)
         + route[:, 3:4] * _gathered_rows(ybuf, slot, ts, 1, TOPK_E))
    out = x_ref[0] + gate_ref[0] * y
    if final:
        out = _rms(out, fg_ref[...])
    o_ref[0] = out


def moe_combine(x, gate, route, y_rows, row_of_item, final_g=None, ts=128):
    b, s, d = x.shape
    nj = s // ts
    n_tiles = b * nj
    rows3 = row_of_item.reshape(n_tiles, 1, TOPK_E * ts)
    smem_blk = lambda f: pl.BlockSpec((1, 1, TOPK_E * ts), f, memory_space=pltpu.SMEM)
    ins = [rows3, rows3, x, gate, route, y_rows]
    specs = [smem_blk(lambda bb, j: (bb * nj + j, 0, 0)),
             smem_blk(lambda bb, j: (jnp.minimum(bb * nj + j + 1, n_tiles - 1), 0, 0)),
             _row_spec(x, ts), _mod_spec(gate, ts),
             pl.BlockSpec((ts, LANES), lambda bb, j: (bb * nj + j, 0)),
             pl.BlockSpec(memory_space=pl.ANY)]
    if final_g is not None:
        ins.append(final_g)
        specs.append(_full_spec(final_g))
    return pl.pallas_call(
        functools.partial(_combine_kernel, ts=ts, final=final_g is not None),
        grid=(b, nj),
        in_specs=specs,
        out_specs=_row_spec(x, ts),
        out_shape=jax.ShapeDtypeStruct(x.shape, F32),
        scratch_shapes=[pltpu.VMEM((2, TOPK_E * ts * SUBLANES, LANES), F32), pltpu.SemaphoreType.DMA((2,))],
        compiler_params=_cparams(("arbitrary", "arbitrary"), 40),
        name="moe_combine",
    )(*ins)


def _route_layout(eid, rows):
    flat = eid.reshape(-1)
    n_items = flat.shape[0]
    onehot = (flat[:, None] == jnp.arange(N_EXPERTS, dtype=I32)[None, :]).astype(I32)
    csum = jnp.cumsum(onehot, axis=0)
    rank = jnp.take_along_axis(csum, flat[:, None], axis=1)[:, 0] - 1
    counts = csum[-1]
    padded = (counts + rows - 1) // rows * rows
    pend = jnp.cumsum(padded)
    row_of_item = (pend - padded)[flat] + rank
    n_blk = -(-(n_items + min(N_EXPERTS, n_items) * (rows - 1)) // rows)
    blk_e = jnp.minimum(jnp.searchsorted(pend, jnp.arange(n_blk, dtype=I32) * rows, side='right'), N_EXPERTS - 1)
    src = jnp.zeros((n_blk * rows,), I32).at[row_of_item].set(jnp.arange(n_items, dtype=I32) // TOPK_E)
    return row_of_item.astype(I32), blk_e.astype(I32), src.reshape(n_blk, rows)


def hier_moe(x, g, shift, scale, gate, w_group, b_group, w_expert, b_expert, w_gate_up, w_down, rows,
             final_g=None, ts_router=512, ts_combine=128):
    h_rows, route = moe_router(x, g, shift, scale, w_group, b_group, w_expert, b_expert, ts=ts_router)
    eid = route[:, :TOPK_E].astype(I32)
    row_of_item, blk_e, src = _route_layout(eid, rows)
    y_rows = moe_experts(h_rows, blk_e, src, w_gate_up, w_down, rows)
    return moe_combine(x, gate, route, y_rows, row_of_item, final_g=final_g, ts=ts_combine)


def kernel(x_prompt, x_sample, cache_moba_k, cache_moba_v, cache_mla_ckv, cache_mla_kpe, state_conv, page_table, c_prompt, c_sample, norm_mix_g, norm_ffn_g, w_ada, b_ada, rel_bias, moba_w_qkv, moba_w_o, mla_w_in, mla_q_norm_g, mla_w_qb, mla_kv_norm_g, mla_w_kvb, mla_w_o, chunk_w_in, chunk_ln_g, chunk_ln_b, chunk_w_s, chunk_b_s, chunk_w_out, conv_w_in, conv_w, conv_w_out, moe_w_group, moe_b_group, moe_w_expert, moe_b_expert, moe_w_gate_up, moe_w_down, final_norm_g):
    bp, sp, d = x_prompt.shape
    db, t_new, _ = x_sample.shape
    ns = db * t_new
    depth = w_ada.shape[0]
    n_pages = page_table.shape[1]
    page = cache_moba_k.shape[1]
    past = n_pages * page

    mods = ada_mods(jnp.concatenate([c_prompt, c_sample], axis=0), w_ada, b_ada)

    def mod_p(i, k):
        return mods[i, :bp, k * d:(k + 1) * d].reshape(bp, 1, d)

    def mod_s(i, k):
        return jnp.repeat(mods[i, bp:, k * d:(k + 1) * d], t_new, axis=0).reshape(1, ns, d)

    xp = x_prompt
    xs = x_sample.reshape(1, ns, d)
    bf = lambda w: w.astype(BF16)
    outs = {}

    for i in range(depth):
        kind = i % 4
        g_mix = norm_mix_g[i].reshape(1, d)
        g_ffn = norm_ffn_g[i].reshape(1, d)
        mp = [mod_p(i, k) for k in range(6)]
        ms = [mod_s(i, k) for k in range(6)]
        if kind == 0:
            nq, nkv = HQ_A * DH_A, HKV_A * DH_A
            offs = [(0, nq), (nq, nkv), (nq + nkv, nkv)]
            w_qkv, w_o = bf(moba_w_qkv), bf(moba_w_o)
            q, k, v = norm_mod_matmul(xp, g_mix, mp[0], mp[1], w_qkv, offs, [F32] * 3, ts=512)
            o = moba_prompt_attention(q, k, v, rel_bias)
            xp = out_proj(xp, mp[2], o, w_o)
            outs['moba_k_prompt'] = k.reshape(bp, sp, HKV_A, DH_A)
            outs['moba_v_prompt'] = v.reshape(bp, sp, HKV_A, DH_A)
            q, k, v = norm_mod_matmul(xs, g_mix, ms[0], ms[1], w_qkv, offs, [F32] * 3, ts=ns)
            o = moba_sample_attention(q.reshape(db, t_new, nq), k.reshape(db, t_new, nkv), v.reshape(db, t_new, nkv),
                                      cache_moba_k.reshape(-1, page, nkv), cache_moba_v.reshape(-1, page, nkv),
                                      page_table, rel_bias)
            xs = out_proj(xs, ms[2], o.reshape(1, ns, nq), w_o, ts=ns)
            outs['moba_k_sample'] = k.reshape(db, t_new, HKV_A, DH_A)
            outs['moba_v_sample'] = v.reshape(db, t_new, HKV_A, DH_A)
        elif kind == 1:
            w_v = bf(mla_w_kvb[:, :, NOPE_B:].transpose(1, 0, 2))
            w_o = bf(mla_w_o)
            proj = (mla_w_in, mla_q_norm_g, mla_w_qb, mla_kv_norm_g, mla_w_kvb)
            pos_p = jnp.broadcast_to(jnp.arange(sp, dtype=I32)[None], (bp, sp))
            qh, kv, ckv, kpe = mla_in(xp, g_mix, mp[0], mp[1], *proj, pos_p, ts=256)
            o_lat = mla_prompt_attention(qh, kv)
            xp = out_proj(xp, mp[2], o_lat, w_o, w_v=w_v, ts=256)
            outs['mla_ckv_prompt'], outs['mla_kpe_prompt'] = ckv, kpe
            pos_s = jnp.tile(past + jnp.arange(t_new, dtype=I32), db)[None]
            qh, _, ckv, kpe = mla_in(xs, g_mix, ms[0], ms[1], *proj, pos_s, ts=ns)
            lat = ckv.shape[-1]
            qs = qh.reshape(HEADS_B, db, t_new, lat + ROPE_B).transpose(1, 0, 2, 3).reshape(db, HEADS_B * t_new, -1)
            o_lat = mla_sample_attention(qs, ckv.reshape(db, t_new, lat), kpe.reshape(db, t_new, ROPE_B),
                                         cache_mla_ckv, cache_mla_kpe, page_table)
            o_lat = o_lat.reshape(db, HEADS_B, t_new, lat).transpose(0, 2, 1, 3).reshape(1, ns, HEADS_B * lat)
            xs = out_proj(xs, ms[2], o_lat, w_o, w_v=w_v, ts=ns)
            outs['mla_ckv_sample'] = ckv.reshape(db, t_new, lat)
            outs['mla_kpe_sample'] = kpe.reshape(db, t_new, ROPE_B)
        elif kind == 2:
            w_in, w_out = bf(chunk_w_in), bf(chunk_w_out)
            gw = chunk_ln_g.shape[0] // GROUPS_C
            bmix_p = jnp.repeat(chunk_b_s.T, gw, axis=1)
            xp, _ = gmlp(xp, g_mix, mp[0], mp[1], mp[2], w_in, chunk_ln_g, chunk_ln_b, chunk_w_s, bmix_p, w_out,
                         want_v=False)
            reps = CHUNK_C // t_new
            eye = jnp.eye(reps, dtype=F32)
            wmix_s = jnp.einsum('ab,gij->gaibj', eye, chunk_w_s[:, :t_new, :t_new]).reshape(GROUPS_C, CHUNK_C, CHUNK_C)
            bmix_s = jnp.repeat(jnp.tile(chunk_b_s[:, :t_new].T, (reps, 1)), gw, axis=1)
            xs, v_s = gmlp(xs, g_mix, ms[0], ms[1], ms[2], w_in, chunk_ln_g, chunk_ln_b, wmix_s, bmix_s, w_out,
                           want_v=True)
            outs['chunk_v_sample'] = v_s.reshape(db, t_new, -1)
        else:
            w_in, w_out = bf(conv_w_in), bf(conv_w_out)
            xp, st = short_conv(xp, g_mix, mp[0], mp[1], mp[2], w_in, conv_w, w_out)
            outs['conv_prompt'] = st
            f1 = jnp.repeat(state_conv[:, 1], t_new, axis=0).reshape(1, ns, d)
            f2 = jnp.tile(state_conv, (1, t_new // (CONV_W - 1), 1)).reshape(1, ns, d)
            xs, zc = short_conv(xs, g_mix, ms[0], ms[1], ms[2], w_in, conv_w, w_out, fix=(f1, f2), period=t_new,
                                ts=ns)
            outs['conv_sample'] = zc.reshape(db, t_new, d)[:, t_new - (CONV_W - 1):]
        fg = final_norm_g.reshape(1, d) if i == depth - 1 else None
        moe_w = (moe_w_group[i], moe_b_group[i], moe_w_expert[i], moe_b_expert[i],
                 bf(moe_w_gate_up[i]), bf(moe_w_down[i]))
        xp = hier_moe(xp, g_ffn, mp[3], mp[4], mp[5], *moe_w, rows=256, final_g=fg)
        xs = hier_moe(xs, g_ffn, ms[3], ms[4], ms[5], *moe_w, rows=128, final_g=fg, ts_router=ns)

    return (xp, xs.reshape(db, t_new, d),
            outs['moba_k_prompt'], outs['moba_v_prompt'], outs['moba_k_sample'], outs['moba_v_sample'],
            outs['mla_ckv_prompt'], outs['mla_kpe_prompt'], outs['mla_ckv_sample'], outs['mla_kpe_sample'],
            outs['chunk_v_sample'], outs['conv_prompt'], outs['conv_sample'])
```

```python
import functools
import math

import numpy as np
import jax
import jax.numpy as jnp
from jax import lax
from jax.experimental import pallas as pl
from jax.experimental.pallas import tpu as pltpu

F32, BF16, I32 = jnp.float32, jnp.bfloat16, jnp.int32
EPS = 1e-6
NEG_INF = -1e30
LANES = 128
SUBLANES = 8

HQ_A, HKV_A, GRP_A, DH_A = 16, 4, 4, 64
BLOCK_A, TOPK_A = 256, 3
NUM_BUCKETS, MAX_DISTANCE = 32, 128
HEADS_B, NOPE_B, ROPE_B, V_B = 16, 64, 32, 64
ROPE_THETA = 10000.0
CHUNK_C, GROUPS_C = 128, 8
CONV_W = 3
N_GROUPS_E, EXPERTS_PER_GROUP, TOPK_E = 4, 8, 2
N_EXPERTS = N_GROUPS_E * EXPERTS_PER_GROUP

NT_DIMS = (((1,), (1,)), ((), ()))
HIGHEST = lax.Precision.HIGHEST


def _cparams(sem, vmem_mb):
    return pltpu.CompilerParams(dimension_semantics=sem, vmem_limit_bytes=vmem_mb << 20)


def _dot(a, b):
    return jnp.dot(a.astype(BF16), b.astype(BF16), preferred_element_type=F32)


def _dot_nt(a, b):
    return lax.dot_general(a.astype(BF16), b.astype(BF16), NT_DIMS, preferred_element_type=F32)


def _rms(x, g):
    return x * lax.rsqrt(jnp.mean(x * x, axis=-1, keepdims=True) + EPS) * g


def _ada_norm(x, g, shift, scale):
    return _rms(x, g) * (1.0 + scale) + shift


def _mod_spec(m, ts):
    d = m.shape[-1]
    if m.shape[1] == 1:
        return pl.BlockSpec((1, 1, d), lambda b, j: (b, 0, 0))
    return pl.BlockSpec((1, ts, d), lambda b, j: (b, j, 0))


def _full_spec(a):
    nd = a.ndim
    return pl.BlockSpec(a.shape, lambda *_: (0,) * nd)


def _row_spec(a, ts):
    return pl.BlockSpec((1, ts) + a.shape[2:], lambda b, j: (b, j) + (0,) * (a.ndim - 2))


def _ada_kernel(c_ref, w_ref, b_ref, o_ref):
    c = c_ref[...]
    o_ref[0] = _dot(c * jax.nn.sigmoid(c), w_ref[0]) + b_ref[0]


def ada_mods(c_all, w_ada, b_ada):
    depth, d, n = w_ada.shape
    rows = c_all.shape[0]
    tn = 1536
    return pl.pallas_call(
        _ada_kernel,
        grid=(depth, n // tn),
        in_specs=[pl.BlockSpec((rows, d), lambda i, j: (0, 0)),
                  pl.BlockSpec((1, d, tn), lambda i, j: (i, 0, j)),
                  pl.BlockSpec((1, 1, tn), lambda i, j: (i, 0, j))],
        out_specs=pl.BlockSpec((1, rows, tn), lambda i, j: (i, 0, j)),
        out_shape=jax.ShapeDtypeStruct((depth, rows, n), F32),
        compiler_params=_cparams(("arbitrary", "arbitrary"), 40),
        name="ada_mods",
    )(c_all, w_ada, b_ada.reshape(depth, 1, n))


def _nmm_kernel(x_ref, g_ref, sh_ref, sc_ref, w_ref, *out_refs, offs):
    h = _ada_norm(x_ref[0], g_ref[...], sh_ref[0], sc_ref[0])
    y = _dot(h, w_ref[...])
    for o_ref, (lo, n) in zip(out_refs, offs):
        o_ref[0] = y[:, lo:lo + n].astype(o_ref.dtype)


def norm_mod_matmul(x, g, shift, scale, w, offs, dtypes, ts):
    b, s, d = x.shape
    outs = [jax.ShapeDtypeStruct((b, s, n), dt) for (_, n), dt in zip(offs, dtypes)]
    return pl.pallas_call(
        functools.partial(_nmm_kernel, offs=tuple(offs)),
        grid=(b, s // ts),
        in_specs=[_row_spec(x, ts), _full_spec(g), _mod_spec(shift, ts), _mod_spec(scale, ts), _full_spec(w)],
        out_specs=[pl.BlockSpec((1, ts, n), lambda bb, j: (bb, j, 0)) for _, n in offs],
        out_shape=outs,
        compiler_params=_cparams(("arbitrary", "arbitrary"), 48),
        name="norm_mod_matmul",
    )(x, g, shift, scale, w)


def _out_proj_kernel(x_ref, gate_ref, a_ref, *rest, n_heads):
    if n_heads:
        wv_ref, wo_ref, o_ref = rest
        a = a_ref[0]
        c = a.shape[1] // n_heads
        a = jnp.concatenate([_dot(a[:, h * c:(h + 1) * c], wv_ref[h]) for h in range(n_heads)], axis=1)
    else:
        wo_ref, o_ref = rest
        a = a_ref[0]
    o_ref[0] = x_ref[0] + gate_ref[0] * _dot(a, wo_ref[...])


def out_proj(x, gate, a, w_o, w_v=None, ts=512):
    b, s, d = x.shape
    ins = [x, gate, a] + ([w_v] if w_v is not None else []) + [w_o]
    specs = [_row_spec(x, ts), _mod_spec(gate, ts), _row_spec(a, ts)]
    specs += ([_full_spec(w_v)] if w_v is not None else []) + [_full_spec(w_o)]
    return pl.pallas_call(
        functools.partial(_out_proj_kernel, n_heads=0 if w_v is None else w_v.shape[0]),
        grid=(b, s // ts),
        in_specs=specs,
        out_specs=_row_spec(x, ts),
        out_shape=jax.ShapeDtypeStruct(x.shape, F32),
        compiler_params=_cparams(("arbitrary", "arbitrary"), 48),
        name="out_proj",
    )(*ins)


def _np_bucket(dist):
    n = np.maximum(dist, 0)
    max_exact = NUM_BUCKETS // 2
    ratio = np.log(np.maximum(n, 1).astype(np.float32) / max_exact) / math.log(MAX_DISTANCE / max_exact)
    large = np.minimum(max_exact + (ratio * (NUM_BUCKETS - max_exact)).astype(np.int32), NUM_BUCKETS - 1)
    return np.where(n < max_exact, n, large).astype(np.int32)


assert int(_np_bucket(np.array(BLOCK_A + 1))) == NUM_BUCKETS - 1


def _halves_max(s):
    return jnp.maximum(s[:, :LANES], s[:, LANES:])


def _two_pass_finish(s_ref, mx_ref, l_ref, acc_ref, n_chunks, value_chunk):
    m = jnp.max(mx_ref[...], axis=1, keepdims=True)
    mx_ref[...] = jnp.broadcast_to(m, mx_ref.shape)
    l_ref[...] = jnp.zeros_like(l_ref)
    acc_ref[...] = jnp.zeros_like(acc_ref)

    def body(j, c):
        mb = mx_ref[...]
        p = jnp.exp(s_ref[j] - jnp.concatenate([mb, mb], axis=1))
        l_ref[...] += p[:, :LANES] + p[:, LANES:]
        acc_ref[...] += jnp.dot(p.astype(BF16), value_chunk(j), preferred_element_type=F32)
        return c

    lax.fori_loop(0, n_chunks, body, 0)
    return acc_ref[...] * (1.0 / jnp.sum(l_ref[...], axis=1, keepdims=True))


def _moba_prompt_kernel(far_ref, q_ref, k_ref, v_ref, bown_ref, badj_ref, o_ref,
                        kmbd_ref, mb_ref, s_ref, mx_ref, l_ref, acc_ref, *, nb):
    i = pl.program_id(1)
    bs = BLOCK_A
    scale = DH_A ** -0.5

    @pl.when(i == 0)
    def _():
        kmbd_ref[...] = jnp.zeros_like(kmbd_ref)
        for j in range(nb):
            km = jnp.mean(k_ref[0, j * bs:(j + 1) * bs, :], axis=0, keepdims=True)
            for hq in range(HQ_A):
                kvh = hq // GRP_A
                kmbd_ref[j * HQ_A + hq:j * HQ_A + hq + 1, hq * DH_A:(hq + 1) * DH_A] = km[:, kvh * DH_A:(kvh + 1) * DH_A]

    q = q_ref[0]
    gates = lax.dot_general(q, kmbd_ref[...], NT_DIMS, precision=HIGHEST, preferred_element_type=F32)
    lane = lax.broadcasted_iota(I32, gates.shape, 1)
    jidx = lane // HQ_A
    past = jidx < i
    gm = jnp.where(past, gates, NEG_INF)
    rank = jnp.zeros(gates.shape, I32)
    for k in range(1, nb):
        other = pltpu.roll(gm, HQ_A * k, axis=1)
        jo = jnp.where(jidx >= k, jidx - k, jidx - k + nb)
        beats = (other > gm) | ((other == gm) & (jo < jidx))
        rank = rank + beats.astype(I32)
    mb = jnp.where(past & (rank < TOPK_A), 0.0, NEG_INF).astype(F32)
    for j in range(nb):
        mb_ref[j] = mb[:, j * HQ_A:(j + 1) * HQ_A]

    def kv_block(j, h):
        lo = pl.multiple_of(j * bs, bs)
        return (k_ref[0, pl.ds(lo, bs), h * DH_A:(h + 1) * DH_A].astype(BF16),
                v_ref[0, pl.ds(lo, bs), h * DH_A:(h + 1) * DH_A].astype(BF16))

    for h in range(HKV_A):
        heads = [GRP_A * h + g for g in range(GRP_A)]
        q4 = (jnp.concatenate([q[:, hq * DH_A:(hq + 1) * DH_A] for hq in heads], axis=0) * scale).astype(BF16)

        def mask_col(j, extra):
            mj = mb_ref[j]
            return jnp.concatenate([mj[:, hq:hq + 1] + (extra[n] if extra else 0.0)
                                    for n, hq in enumerate(heads)], axis=0)

        s = _dot_nt(q4, kv_block(i, h)[0]) + bown_ref[GRP_A * h:GRP_A * (h + 1)].reshape(GRP_A * bs, bs)
        s_ref[i] = s
        mx_ref[...] = _halves_max(s)

        def scores(j, bias):
            sj = _dot_nt(q4, kv_block(j, h)[0]) + bias
            s_ref[j] = sj
            mx_ref[...] = jnp.maximum(mx_ref[...], _halves_max(sj))

        @pl.when(i >= 1)
        def _():
            badj = badj_ref[GRP_A * h:GRP_A * (h + 1)].reshape(GRP_A * bs, bs)
            scores(i - 1, badj + mask_col(i - 1, None))

        far = [far_ref[hq] for hq in heads]

        def far_body(j, c):
            scores(j, mask_col(j, far))
            return c

        lax.fori_loop(0, jnp.maximum(i - 1, 0), far_body, 0)

        o = _two_pass_finish(s_ref, mx_ref, l_ref, acc_ref, i + 1, lambda j: kv_block(j, h)[1])
        for g, hq in enumerate(heads):
            o_ref[0, :, hq * DH_A:(hq + 1) * DH_A] = o[g * bs:(g + 1) * bs].astype(o_ref.dtype)


def moba_prompt_attention(q, k, v, rel_bias):
    b, s, _ = q.shape
    nb = s // BLOCK_A
    assert s % BLOCK_A == 0 and nb * HQ_A == LANES and nb >= TOPK_A
    t = np.arange(BLOCK_A)[:, None]
    sk = np.arange(BLOCK_A)[None, :]
    bown = jnp.where(jnp.asarray(t >= sk)[None], rel_bias[_np_bucket(t - sk)].transpose(2, 0, 1), NEG_INF)
    badj = rel_bias[_np_bucket(BLOCK_A + t - sk)].transpose(2, 0, 1)
    far = rel_bias[NUM_BUCKETS - 1]
    kv_spec = pl.BlockSpec((1, s, HKV_A * DH_A), lambda bb, i: (bb, 0, 0))
    return pl.pallas_call(
        functools.partial(_moba_prompt_kernel, nb=nb),
        grid=(b, nb),
        in_specs=[pl.BlockSpec(memory_space=pltpu.SMEM),
                  pl.BlockSpec((1, BLOCK_A, HQ_A * DH_A), lambda bb, i: (bb, i, 0)),
                  kv_spec, kv_spec, _full_spec(bown), _full_spec(badj)],
        out_specs=pl.BlockSpec((1, BLOCK_A, HQ_A * DH_A), lambda bb, i: (bb, i, 0)),
        out_shape=jax.ShapeDtypeStruct((b, s, HQ_A * DH_A), BF16),
        scratch_shapes=[pltpu.VMEM((nb * HQ_A, HQ_A * DH_A), F32),
                        pltpu.VMEM((nb, BLOCK_A, HQ_A), F32),
                        pltpu.VMEM((nb, GRP_A * BLOCK_A, BLOCK_A), F32),
                        pltpu.VMEM((GRP_A * BLOCK_A, LANES), F32),
                        pltpu.VMEM((GRP_A * BLOCK_A, LANES), F32),
                        pltpu.VMEM((GRP_A * BLOCK_A, DH_A), F32)],
        compiler_params=_cparams(("arbitrary", "arbitrary"), 48),
        name="moba_prompt_attention",
    )(far, q, k, v, bown, badj)


def _page_copies(cache_hbm, buf, sem, pt_ref, bb, slot, n_pages):
    return [pltpu.make_async_copy(cache_hbm.at[pt_ref[bb * n_pages + p]], buf.at[slot, p], sem.at[slot])
            for p in range(n_pages)]


def _pages_wait(cache_hbm, buf, sem, slot, n_pages):
    pltpu.make_async_copy(cache_hbm.at[pl.ds(0, n_pages)], buf.at[slot], sem.at[slot]).wait()


def _new_token_partials(qs, kn, vn, bnew, n_new):
    qf = qs.astype(BF16).astype(F32)
    kf = kn.astype(BF16).astype(F32)
    vf = vn.astype(BF16).astype(F32)
    cols = [jnp.sum(qf * kf[s:s + 1, :], axis=1, keepdims=True) + bnew[:, s:s + 1] for s in range(n_new)]
    m = functools.reduce(jnp.maximum, cols)
    ps = [jnp.exp(c - m) for c in cols]
    l = functools.reduce(jnp.add, ps)
    acc = functools.reduce(jnp.add, [p.astype(BF16).astype(F32) * vf[s:s + 1, :] for s, p in enumerate(ps)])
    return m, l, acc


def _moba_sample_kernel(pt_ref, q_ref, kn_ref, vn_ref, far_ref, blast_ref, bnew_ref, ck_hbm, cv_hbm, o_ref,
                        kbuf, vbuf, ksem, vsem, qbd_ref, km_ref, *, n_pages, n_new):
    b = pl.program_id(0)
    slot = b % 2
    nb = n_pages * kbuf.shape[2] // BLOCK_A
    ppb = n_pages // nb
    scale = DH_A ** -0.5

    def fetch(bb, sl):
        for c in _page_copies(ck_hbm, kbuf, ksem, pt_ref, bb, sl, n_pages):
            c.start()
        for c in _page_copies(cv_hbm, vbuf, vsem, pt_ref, bb, sl, n_pages):
            c.start()

    @pl.when(b == 0)
    def _():
        km_ref[...] = jnp.zeros_like(km_ref)
        fetch(0, 0)

    @pl.when(b + 1 < pl.num_programs(0))
    def _():
        fetch(b + 1, 1 - slot)

    q = q_ref[0]
    qbd_ref[...] = jnp.zeros_like(qbd_ref)
    for hq in range(HQ_A):
        kvh = hq // GRP_A
        qbd_ref[hq * n_new:(hq + 1) * n_new, kvh * DH_A:(kvh + 1) * DH_A] = q[:, hq * DH_A:(hq + 1) * DH_A]
    qbd = qbd_ref[...]
    qs = qbd * scale

    carry = _new_token_partials(qs, kn_ref[0], vn_ref[0], bnew_ref[...], n_new)

    _pages_wait(ck_hbm, kbuf, ksem, slot, n_pages)

    def key_block(buf, j):
        blk = buf[slot, pl.ds(j * ppb, ppb)]
        return blk.reshape(BLOCK_A, blk.shape[-1])

    def km_body(j, c):
        km_ref[pl.ds(j, 1), :] = jnp.sum(key_block(kbuf, j), axis=0, keepdims=True) * (1.0 / BLOCK_A)
        return c

    lax.fori_loop(0, nb, km_body, 0)

    gates = lax.dot_general(qbd, km_ref[...], NT_DIMS, precision=HIGHEST, preferred_element_type=F32)
    lane = lax.broadcasted_iota(I32, gates.shape, 1)
    gcur = jnp.where(lane < nb, gates, NEG_INF)
    sel = jnp.zeros(gates.shape, jnp.bool_)
    for _ in range(TOPK_A):
        mx = jnp.max(gcur, axis=1, keepdims=True)
        idx = jnp.min(jnp.where(gcur == mx, lane, LANES), axis=1, keepdims=True)
        pick = lane == idx
        sel = sel | (pick & (mx > 0.5 * NEG_INF))
        gcur = jnp.where(pick, -3e38, gcur)
    selbias = jnp.where(sel, 0.0, NEG_INF).astype(F32)

    _pages_wait(cv_hbm, vbuf, vsem, slot, n_pages)
    far = far_ref[...]
    rows = qbd.shape[0]
    n_chunks = 4 if nb % 4 == 0 else 1
    bpc = nb // n_chunks
    qsb = qs.astype(BF16)

    def chunk(buf, c):
        blk = buf[slot, c * bpc * ppb:(c + 1) * bpc * ppb]
        return blk.reshape(bpc * BLOCK_A, blk.shape[-1]).astype(BF16)

    def block_bias(j):
        col = selbias[:, j:j + 1]
        if j == nb - 1:
            return blast_ref[...] + col
        return jnp.broadcast_to(col + far, (rows, BLOCK_A))

    scores = [_dot_nt(qsb, chunk(kbuf, c))
              + jnp.concatenate([block_bias(c * bpc + jj) for jj in range(bpc)], axis=1) for c in range(n_chunks)]
    m0, l0, acc0 = carry
    m = functools.reduce(jnp.maximum, [m0] + [jnp.max(s, axis=1, keepdims=True) for s in scores])
    alpha = jnp.exp(m0 - m)
    l = alpha * l0
    acc = alpha * acc0
    for c, s in enumerate(scores):
        p = jnp.exp(s - m)
        l = l + jnp.sum(p, axis=1, keepdims=True)
        acc = acc + jnp.dot(p.astype(BF16), chunk(vbuf, c), preferred_element_type=F32)

    o = acc * (1.0 / l)
    for hq in range(HQ_A):
        kvh = hq // GRP_A
        o_ref[0, :, hq * DH_A:(hq + 1) * DH_A] = (
            o[hq * n_new:(hq + 1) * n_new, kvh * DH_A:(kvh + 1) * DH_A].astype(o_ref.dtype))


def moba_sample_attention(q, k_new, v_new, cache_k, cache_v, page_table, rel_bias):
    db, t_new, _ = q.shape
    n_pages = page_table.shape[1]
    page = cache_k.shape[1]
    past = n_pages * page
    assert past % BLOCK_A == 0 and BLOCK_A % page == 0 and t_new <= BLOCK_A
    nb = past // BLOCK_A
    assert TOPK_A <= nb <= LANES
    tt = np.arange(t_new)[:, None]
    last = rel_bias[_np_bucket(BLOCK_A + tt - np.arange(BLOCK_A)[None, :])]
    blast = last.transpose(2, 0, 1).reshape(HQ_A * t_new, BLOCK_A)
    ts = np.arange(t_new)[None, :]
    bnew = jnp.where(jnp.asarray(tt >= ts)[None], rel_bias[_np_bucket(tt - ts)].transpose(2, 0, 1), NEG_INF)
    bnew = bnew.reshape(HQ_A * t_new, t_new)
    far = jnp.repeat(rel_bias[NUM_BUCKETS - 1], t_new).reshape(HQ_A * t_new, 1)
    rows = HQ_A * t_new
    kw = HKV_A * DH_A
    grid_spec = pltpu.PrefetchScalarGridSpec(
        num_scalar_prefetch=1,
        grid=(db,),
        in_specs=[pl.BlockSpec((1, t_new, HQ_A * DH_A), lambda b, pt: (b, 0, 0)),
                  pl.BlockSpec((1, t_new, kw), lambda b, pt: (b, 0, 0)),
                  pl.BlockSpec((1, t_new, kw), lambda b, pt: (b, 0, 0)),
                  pl.BlockSpec((rows, 1), lambda b, pt: (0, 0)),
                  pl.BlockSpec((rows, BLOCK_A), lambda b, pt: (0, 0)),
                  pl.BlockSpec((rows, t_new), lambda b, pt: (0, 0)),
                  pl.BlockSpec(memory_space=pl.ANY),
                  pl.BlockSpec(memory_space=pl.ANY)],
        out_specs=pl.BlockSpec((1, t_new, HQ_A * DH_A), lambda b, pt: (b, 0, 0)),
        scratch_shapes=[pltpu.VMEM((2, n_pages, page, kw), F32),
                        pltpu.VMEM((2, n_pages, page, kw), F32),
                        pltpu.SemaphoreType.DMA((2,)),
                        pltpu.SemaphoreType.DMA((2,)),
                        pltpu.VMEM((rows, kw), F32),
                        pltpu.VMEM((LANES, kw), F32)])
    return pl.pallas_call(
        functools.partial(_moba_sample_kernel, n_pages=n_pages, n_new=t_new),
        grid_spec=grid_spec,
        out_shape=jax.ShapeDtypeStruct((db, t_new, HQ_A * DH_A), BF16),
        compiler_params=_cparams(("arbitrary",), 56),
        name="moba_sample_attention",
    )(page_table.reshape(-1), q, k_new, v_new, far, blast, bnew, cache_k, cache_v)


def _mla_in_kernel(x_ref, g_ref, sh_ref, sc_ref, win_ref, qg_ref, wqb_ref, kvg_ref, wn_ref,
                   cq_ref, sq_ref, q_ref, kv_ref, ckv_ref, kpe_ref):
    nope_w = HEADS_B * NOPE_B
    rope_w = HEADS_B * ROPE_B
    lat = kv_ref.shape[-1] - ROPE_B
    qa_w = qg_ref.shape[-1]
    scale = (NOPE_B + ROPE_B) ** -0.5
    h = _ada_norm(x_ref[0], g_ref[...], sh_ref[0], sc_ref[0])
    y = _dot(h, win_ref[...])
    qn = _rms(y[:, :qa_w], qg_ref[...])
    ckv = _rms(y[:, qa_w:qa_w + lat], kvg_ref[...])
    cos = cq_ref[0]
    sin = sq_ref[0]
    k_lo = qa_w + lat
    kpe = y[:, k_lo:k_lo + ROPE_B] * cos[:, :ROPE_B] + y[:, k_lo + LANES:k_lo + LANES + ROPE_B] * sin[:, :ROPE_B]
    ckv_ref[0] = ckv
    kpe_ref[0] = kpe
    kv_ref[0, :, :lat] = ckv.astype(BF16)
    kv_ref[0, :, lat:] = kpe.astype(BF16)
    qq = _dot(qn, wqb_ref[...])
    q_pe = (qq[:, nope_w:nope_w + rope_w] * cos + qq[:, nope_w + rope_w:] * sin) * scale
    for hd in range(HEADS_B):
        q_lat = _dot(qq[:, hd * NOPE_B:(hd + 1) * NOPE_B], wn_ref[hd]) * scale
        q_ref[0, hd, :, :lat] = q_lat.astype(BF16)
        q_ref[0, hd, :, lat:] = q_pe[:, hd * ROPE_B:(hd + 1) * ROPE_B].astype(BF16)


def _rope_swap_cols(w, n_rope):
    half = n_rope // 2
    return jnp.concatenate([w[..., half:], w[..., :half]], axis=-1)


def mla_in(x, g, shift, scale, w_in, q_norm_g, w_qb, kv_norm_g, w_kvb, pos, ts):
    b, s, d = x.shape
    qa_w = q_norm_g.shape[0]
    lat = kv_norm_g.shape[0]
    kpe_w = w_in[:, qa_w + lat:]
    pad = jnp.zeros((d, LANES - ROPE_B), w_in.dtype)
    w_in_ext = jnp.concatenate([w_in[:, :qa_w + lat], kpe_w, pad, _rope_swap_cols(kpe_w, ROPE_B), pad], axis=1).astype(BF16)
    wq = w_qb.reshape(qa_w, HEADS_B, NOPE_B + ROPE_B)
    wq_rope = wq[:, :, NOPE_B:]
    w_qb_ext = jnp.concatenate([wq[:, :, :NOPE_B].reshape(qa_w, -1), wq_rope.reshape(qa_w, -1),
                                _rope_swap_cols(wq_rope, ROPE_B).reshape(qa_w, -1)], axis=1).astype(BF16)
    w_nope = w_kvb[:, :, :NOPE_B].transpose(1, 2, 0).astype(BF16)
    half = ROPE_B // 2
    inv = ROPE_THETA ** (-jnp.arange(half, dtype=F32) / half)
    ang = pos.astype(F32)[..., None] * inv
    cos, sin = jnp.cos(ang), jnp.sin(ang)
    cos_t = jnp.tile(jnp.concatenate([cos, cos], axis=-1), (1, 1, HEADS_B))
    sin_t = jnp.tile(jnp.concatenate([-sin, sin], axis=-1), (1, 1, HEADS_B))
    outs = [jax.ShapeDtypeStruct((b, HEADS_B, s, lat + ROPE_B), BF16),
            jax.ShapeDtypeStruct((b, s, lat + ROPE_B), BF16),
            jax.ShapeDtypeStruct((b, s, lat), F32),
            jax.ShapeDtypeStruct((b, s, ROPE_B), F32)]
    qg = q_norm_g.reshape(1, -1)
    kvg = kv_norm_g.reshape(1, -1)
    return pl.pallas_call(
        _mla_in_kernel,
        grid=(b, s // ts),
        in_specs=[_row_spec(x, ts), _full_spec(g), _mod_spec(shift, ts), _mod_spec(scale, ts),
                  _full_spec(w_in_ext), _full_spec(qg), _full_spec(w_qb_ext), _full_spec(kvg), _full_spec(w_nope),
                  _row_spec(cos_t, ts), _row_spec(sin_t, ts)],
        out_specs=[pl.BlockSpec((1, HEADS_B, ts, lat + ROPE_B), lambda bb, j: (bb, 0, j, 0)),
                   pl.BlockSpec((1, ts, lat + ROPE_B), lambda bb, j: (bb, j, 0)),
                   pl.BlockSpec((1, ts, lat), lambda bb, j: (bb, j, 0)),
                   pl.BlockSpec((1, ts, ROPE_B), lambda bb, j: (bb, j, 0))],
        out_shape=outs,
        compiler_params=_cparams(("arbitrary", "arbitrary"), 48),
        name="mla_in",
    )(x, g, shift, scale, w_in_ext, qg, w_qb_ext, kvg, w_nope, cos_t, sin_t)


def _mla_prompt_kernel(q_ref, kv_ref, o_ref, s_ref, mx_ref, l_ref, acc_ref, *, tq, hc):
    i = pl.program_id(1)
    lat = acc_ref.shape[-1]
    rows = hc * tq
    row_t = lax.broadcasted_iota(I32, (rows, tq), 0) % tq
    col_s = lax.broadcasted_iota(I32, (rows, tq), 1)
    causal = col_s <= row_t

    def kv_chunk(j):
        return kv_ref[0, pl.ds(pl.multiple_of(j * tq, tq), tq), :]

    for c in range(HEADS_B // hc):
        q = q_ref[0, c * hc:(c + 1) * hc].reshape(rows, q_ref.shape[-1])

        s = jnp.where(causal, lax.dot_general(q, kv_chunk(i), NT_DIMS, preferred_element_type=F32), NEG_INF)
        s_ref[i] = s
        mx_ref[...] = _halves_max(s)

        def body(j, cc):
            sj = lax.dot_general(q, kv_chunk(j), NT_DIMS, preferred_element_type=F32)
            s_ref[j] = sj
            mx_ref[...] = jnp.maximum(mx_ref[...], _halves_max(sj))
            return cc

        lax.fori_loop(0, i, body, 0)
        o = _two_pass_finish(s_ref, mx_ref, l_ref, acc_ref, i + 1, lambda j: kv_chunk(j)[:, :lat])
        for hl in range(hc):
            hd = c * hc + hl
            o_ref[0, :, hd * lat:(hd + 1) * lat] = o[hl * tq:(hl + 1) * tq].astype(o_ref.dtype)


def mla_prompt_attention(q, kv, hc=4):
    b, nh, s, w = q.shape
    lat = w - ROPE_B
    tq = 2 * LANES
    assert s % tq == 0
    return pl.pallas_call(
        functools.partial(_mla_prompt_kernel, tq=tq, hc=hc),
        grid=(b, s // tq),
        in_specs=[pl.BlockSpec((1, nh, tq, w), lambda bb, i: (bb, 0, i, 0)),
                  pl.BlockSpec((1, s, w), lambda bb, i: (bb, 0, 0))],
        out_specs=pl.BlockSpec((1, tq, nh * lat), lambda bb, i: (bb, i, 0)),
        out_shape=jax.ShapeDtypeStruct((b, s, nh * lat), BF16),
        scratch_shapes=[pltpu.VMEM((s // tq, hc * tq, tq), F32),
                        pltpu.VMEM((hc * tq, LANES), F32), pltpu.VMEM((hc * tq, LANES), F32),
                        pltpu.VMEM((hc * tq, lat), F32)],
        compiler_params=_cparams(("arbitrary", "arbitrary"), 48),
        name="mla_prompt_attention",
    )(q, kv)


def _mla_sample_kernel(pt_ref, q_ref, cn_ref, pn_ref, bnew_ref, cc_hbm, cp_hbm, o_ref,
                       cbuf, pbuf, csem, psem, *, n_pages, n_new, chunk_pages):
    b = pl.program_id(0)
    slot = b % 2
    lat = cbuf.shape[-1]
    page = cbuf.shape[2]

    def fetch(bb, sl):
        for c in _page_copies(cc_hbm, cbuf, csem, pt_ref, bb, sl, n_pages):
            c.start()
        for c in _page_copies(cp_hbm, pbuf, psem, pt_ref, bb, sl, n_pages):
            c.start()

    @pl.when(b == 0)
    def _():
        fetch(0, 0)

    @pl.when(b + 1 < pl.num_programs(0))
    def _():
        fetch(b + 1, 1 - slot)

    q = q_ref[0]
    ql, qp = q[:, :lat], q[:, lat:]
    qf, qpf = ql.astype(F32), qp.astype(F32)
    cn = cn_ref[0].astype(BF16).astype(F32)
    pn = pn_ref[0].astype(BF16).astype(F32)
    bnew = bnew_ref[...]
    cols = [jnp.sum(qf * cn[s:s + 1, :], axis=1, keepdims=True) + jnp.sum(qpf * pn[s:s + 1, :], axis=1, keepdims=True)
            + bnew[:, s:s + 1] for s in range(n_new)]
    _pages_wait(cc_hbm, cbuf, csem, slot, n_pages)
    _pages_wait(cp_hbm, pbuf, psem, slot, n_pages)

    def chunk(buf, c):
        blk = buf[slot, c * chunk_pages:(c + 1) * chunk_pages]
        return blk.reshape(chunk_pages * page, blk.shape[-1]).astype(BF16)

    n_chunks = n_pages // chunk_pages
    scores = [_dot_nt(ql, chunk(cbuf, c)) + _dot_nt(qp, chunk(pbuf, c)) for c in range(n_chunks)]
    m = functools.reduce(jnp.maximum, cols + [jnp.max(s, axis=1, keepdims=True) for s in scores])
    ps = [jnp.exp(c - m) for c in cols]
    l = functools.reduce(jnp.add, ps)
    acc = functools.reduce(jnp.add, [p.astype(BF16).astype(F32) * cn[s:s + 1, :] for s, p in enumerate(ps)])
    for c, s in enumerate(scores):
        p = jnp.exp(s - m)
        l = l + jnp.sum(p, axis=1, keepdims=True)
        acc = acc + jnp.dot(p.astype(BF16), chunk(cbuf, c), preferred_element_type=F32)
    o_ref[0] = (acc * (1.0 / l)).astype(o_ref.dtype)


def mla_sample_attention(q, ckv_new, kpe_new, cache_ckv, cache_kpe, page_table):
    db, rows, w = q.shape
    t_new = ckv_new.shape[1]
    lat = ckv_new.shape[2]
    n_pages = page_table.shape[1]
    page = cache_ckv.shape[1]
    tt = np.arange(t_new)[:, None]
    ts = np.arange(t_new)[None, :]
    bnew = jnp.asarray(np.tile(np.where(tt >= ts, 0.0, NEG_INF).astype(np.float32), (rows // t_new, 1)))
    grid_spec = pltpu.PrefetchScalarGridSpec(
        num_scalar_prefetch=1,
        grid=(db,),
        in_specs=[pl.BlockSpec((1, rows, w), lambda b, pt: (b, 0, 0)),
                  pl.BlockSpec((1, t_new, lat), lambda b, pt: (b, 0, 0)),
                  pl.BlockSpec((1, t_new, ROPE_B), lambda b, pt: (b, 0, 0)),
                  pl.BlockSpec((rows, t_new), lambda b, pt: (0, 0)),
                  pl.BlockSpec(memory_space=pl.ANY),
                  pl.BlockSpec(memory_space=pl.ANY)],
        out_specs=pl.BlockSpec((1, rows, lat), lambda b, pt: (b, 0, 0)),
        scratch_shapes=[pltpu.VMEM((2, n_pages, page, lat), F32),
                        pltpu.VMEM((2, n_pages, page, ROPE_B), F32),
                        pltpu.SemaphoreType.DMA((2,)),
                        pltpu.SemaphoreType.DMA((2,))])
    return pl.pallas_call(
        functools.partial(_mla_sample_kernel, n_pages=n_pages, n_new=t_new,
                          chunk_pages=n_pages // 4 if n_pages % 4 == 0 else n_pages),
        grid_spec=grid_spec,
        out_shape=jax.ShapeDtypeStruct((db, rows, lat), BF16),
        compiler_params=_cparams(("arbitrary",), 40),
        name="mla_sample_attention",
    )(page_table.reshape(-1), q, ckv_new, kpe_new, bnew, cache_ckv, cache_kpe)


def _gmlp_kernel(x_ref, g_ref, sh_ref, sc_ref, gate_ref, win_ref, lng_ref, lnb_ref, wmix_ref, bmix_ref, wout_ref,
                 o_ref, *v_out):
    dv = lng_ref.shape[-1]
    gw = dv // GROUPS_C
    ts = x_ref.shape[1]
    x = x_ref[0]
    h = _ada_norm(x, g_ref[...], sh_ref[0], sc_ref[0])
    y = _dot(h, win_ref[...])
    a = 0.5 * y * (1.0 + lax.erf(y * (1.0 / math.sqrt(2.0))))
    u, v = a[:, :dv], a[:, dv:]
    mu = jnp.mean(v, axis=-1, keepdims=True)
    vc = v - mu
    var = jnp.mean(vc * vc, axis=-1, keepdims=True)
    v = vc * lax.rsqrt(var + EPS) * lng_ref[...] + lnb_ref[...]
    if v_out:
        v_out[0][0] = v
    r = lax.broadcasted_iota(I32, (CHUNK_C, CHUNK_C), 0)
    c = lax.broadcasted_iota(I32, (CHUNK_C, CHUNK_C), 1)
    tril = r >= c
    wm = [jnp.where(tril, wmix_ref[gi], 0.0).astype(BF16) for gi in range(GROUPS_C)]
    vb = v.astype(BF16)
    outs = []
    for ci in range(ts // CHUNK_C):
        rows = slice(ci * CHUNK_C, (ci + 1) * CHUNK_C)
        mixed = jnp.concatenate(
            [jnp.dot(wm[gi], vb[rows, gi * gw:(gi + 1) * gw], preferred_element_type=F32) for gi in range(GROUPS_C)],
            axis=1) + bmix_ref[...]
        outs.append(_dot(u[rows] * mixed, wout_ref[...]))
    o_ref[0] = x + gate_ref[0] * jnp.concatenate(outs, axis=0)


def gmlp(x, g, shift, scale, gate, w_in, ln_g, ln_b, wmix, bmix, w_out, want_v, ts=256):
    b, s, d = x.shape
    dv = ln_g.shape[0]
    lng, lnb = ln_g.reshape(1, dv), ln_b.reshape(1, dv)
    outs = [jax.ShapeDtypeStruct(x.shape, F32)]
    out_specs = [_row_spec(x, ts)]
    if want_v:
        outs.append(jax.ShapeDtypeStruct((b, s, dv), F32))
        out_specs.append(pl.BlockSpec((1, ts, dv), lambda bb, j: (bb, j, 0)))
    res = pl.pallas_call(
        _gmlp_kernel,
        grid=(b, s // ts),
        in_specs=[_row_spec(x, ts), _full_spec(g), _mod_spec(shift, ts), _mod_spec(scale, ts), _mod_spec(gate, ts),
                  _full_spec(w_in), _full_spec(lng), _full_spec(lnb), _full_spec(wmix), _full_spec(bmix),
                  _full_spec(w_out)],
        out_specs=out_specs,
        out_shape=outs,
        compiler_params=_cparams(("arbitrary", "arbitrary"), 56),
        name="gmlp",
    )(x, g, shift, scale, gate, w_in, lng, lnb, wmix, bmix, w_out)
    return res if want_v else (res[0], None)


def _conv_kernel(x_ref, g_ref, sh_ref, sc_ref, gate_ref, win_ref, cw_ref, wout_ref, *rest, period):
    ts, d = x_ref.shape[1], x_ref.shape[2]
    if period:
        f1_ref, f2_ref, o_ref, zc_ref, zp_ref = rest
    else:
        o_ref, st_ref, zp_ref = rest
    x = x_ref[0]
    h = _ada_norm(x, g_ref[...], sh_ref[0], sc_ref[0])
    y = _dot(h, win_ref[...])
    gate_b, zc = y[:, :d], y[:, d:2 * d] * y[:, 2 * d:]
    if period:
        zp_ref[0:SUBLANES] = jnp.zeros((SUBLANES, d), F32)
    else:
        @pl.when(pl.program_id(1) == 0)
        def _():
            zp_ref[0:SUBLANES] = jnp.zeros((SUBLANES, d), F32)
    zp_ref[SUBLANES:] = zc
    r1 = zp_ref[SUBLANES - 1:SUBLANES - 1 + ts]
    r2 = zp_ref[SUBLANES - 2:SUBLANES - 2 + ts]
    if period:
        t = lax.broadcasted_iota(I32, (ts, 1), 0) % period
        r1 = jnp.where(t >= 1, r1, f1_ref[0])
        r2 = jnp.where(t >= 2, r2, f2_ref[0])
        zc_ref[0] = zc
    else:
        st_ref[0] = zc[ts - (CONV_W - 1):]
        zp_ref[0:SUBLANES] = zc[ts - SUBLANES:]
    cw = cw_ref[...]
    yc = cw[0:1] * r2 + cw[1:2] * r1 + cw[2:3] * zc
    o_ref[0] = x + gate_ref[0] * _dot(gate_b * yc, wout_ref[...])


def short_conv(x, g, shift, scale, gate, w_in, conv_w, w_out, fix=None, period=0, ts=512):
    b, s, d = x.shape
    assert CONV_W == 3
    ins = [x, g, shift, scale, gate, w_in, conv_w, w_out]
    specs = [_row_spec(x, ts), _full_spec(g), _mod_spec(shift, ts), _mod_spec(scale, ts), _mod_spec(gate, ts),
             _full_spec(w_in), _full_spec(conv_w), _full_spec(w_out)]
    if period:
        ins += list(fix)
        specs += [_row_spec(x, ts), _row_spec(x, ts)]
        outs = [jax.ShapeDtypeStruct(x.shape, F32), jax.ShapeDtypeStruct(x.shape, F32)]
        out_specs = [_row_spec(x, ts), _row_spec(x, ts)]
    else:
        outs = [jax.ShapeDtypeStruct(x.shape, F32), jax.ShapeDtypeStruct((b, CONV_W - 1, d), F32)]
        out_specs = [_row_spec(x, ts), pl.BlockSpec((1, CONV_W - 1, d), lambda bb, j: (bb, 0, 0))]
    return pl.pallas_call(
        functools.partial(_conv_kernel, period=period),
        grid=(b, s // ts),
        in_specs=specs,
        out_specs=out_specs,
        out_shape=outs,
        scratch_shapes=[pltpu.VMEM((SUBLANES + ts, d), F32)],
        compiler_params=_cparams(("arbitrary", "arbitrary"), 56),
        name="short_conv",
    )(*ins)


def _router_kernel(x_ref, g_ref, sh_ref, sc_ref, whi_ref, wlo_ref, bias_ref, ltri_ref, route_ref, cnt_ref, run_ref):
    first = (pl.program_id(0) == 0) & (pl.program_id(1) == 0)

    @pl.when(first)
    def _():
        run_ref[...] = jnp.zeros_like(run_ref)

    h = _ada_norm(x_ref[0], g_ref[...], sh_ref[0], sc_ref[0])
    h_hi = h.astype(BF16)
    h_lo = (h - h_hi.astype(F32)).astype(BF16)
    logits = (jnp.dot(h_hi, whi_ref[...], preferred_element_type=F32)
              + jnp.dot(h_lo, whi_ref[...], preferred_element_type=F32)
              + jnp.dot(h_hi, wlo_ref[...], preferred_element_type=F32)) + bias_ref[...]
    lane = lax.broadcasted_iota(I32, logits.shape, 1)
    low = -3e38
    is_g = lane < N_GROUPS_E
    lg = jnp.where(is_g, logits, low)
    mg = jnp.max(lg, axis=1, keepdims=True)
    g_top = jnp.min(jnp.where(lg == mg, lane, LANES), axis=1, keepdims=True)
    p_g = 1.0 / jnp.sum(jnp.where(is_g, jnp.exp(lg - mg), 0.0), axis=1, keepdims=True)
    e_lo = N_GROUPS_E + g_top * EXPERTS_PER_GROUP
    le = jnp.where((lane >= e_lo) & (lane < e_lo + EXPERTS_PER_GROUP), logits, low)
    m1 = jnp.max(le, axis=1, keepdims=True)
    i1 = jnp.min(jnp.where(le == m1, lane, LANES), axis=1, keepdims=True)
    le2 = jnp.where(lane == i1, low, le)
    m2 = jnp.max(le2, axis=1, keepdims=True)
    i2 = jnp.min(jnp.where(le2 == m2, lane, LANES), axis=1, keepdims=True)
    e2 = jnp.exp(m2 - m1)
    w1 = p_g / (1.0 + e2)
    w2 = p_g * e2 / (1.0 + e2)
    e1 = i1 - N_GROUPS_E
    e2i = i2 - N_GROUPS_E
    oh1 = (lane == e1).astype(F32)
    oh2 = (lane == e2i).astype(F32)
    cnt = oh1 + oh2
    base = jnp.dot(ltri_ref[...], cnt.astype(BF16), preferred_element_type=F32) + run_ref[...]
    r1 = jnp.sum(oh1 * base, axis=1, keepdims=True)
    r2 = jnp.sum(oh2 * base, axis=1, keepdims=True)
    run = run_ref[...] + jnp.sum(cnt, axis=0, keepdims=True)
    run_ref[...] = run
    cnt_ref[...] = run
    vals = (e1.astype(F32), e2i.astype(F32), w1, w2, r1, r2)
    route = jnp.zeros(logits.shape, F32)
    for n, val in enumerate(vals):
        route = jnp.where(lane == n, val, route)
    route_ref[...] = route


def moe_router(x, g, shift, scale, w_group, b_group, w_expert, b_expert, ts=512):
    b, s, d = x.shape
    n = b * s
    w = jnp.concatenate([w_group, w_expert], axis=1)
    w = jnp.pad(w, ((0, 0), (0, LANES - w.shape[1])))
    w_hi = w.astype(BF16)
    w_lo = (w - w_hi.astype(F32)).astype(BF16)
    bias = jnp.pad(jnp.concatenate([b_group, b_expert]), (0, LANES - N_GROUPS_E - N_EXPERTS)).reshape(1, LANES)
    ltri = jnp.asarray(np.tril(np.ones((ts, ts), np.float32), -1), BF16)
    nj = s // ts
    return pl.pallas_call(
        _router_kernel,
        grid=(b, nj),
        in_specs=[_row_spec(x, ts), _full_spec(g), _mod_spec(shift, ts), _mod_spec(scale, ts),
                  _full_spec(w_hi), _full_spec(w_lo), _full_spec(bias), _full_spec(ltri)],
        out_specs=[pl.BlockSpec((ts, LANES), lambda bb, j: (bb * nj + j, 0)),
                   pl.BlockSpec((1, LANES), lambda bb, j: (0, 0))],
        out_shape=[jax.ShapeDtypeStruct((n, LANES), F32), jax.ShapeDtypeStruct((1, LANES), F32)],
        scratch_shapes=[pltpu.VMEM((1, LANES), F32)],
        compiler_params=_cparams(("arbitrary", "arbitrary"), 40),
        name="moe_router",
    )(x, g, shift, scale, w_hi, w_lo, bias, ltri)


def _dispatch_kernel(seg_ref, pos_ref, x_ref, g_ref, sh_ref, sc_ref, xs_hbm, hbuf, zero_ref, sem, zsem, *, ts):
    i = pl.program_id(0) * pl.num_programs(1) + pl.program_id(1)
    n = pl.num_programs(0) * pl.num_programs(1)
    slot = i % 2
    n_seg = seg_ref.shape[0] // 2

    def row_tile(p):
        return xs_hbm.at[pl.ds(pl.multiple_of(p * SUBLANES, SUBLANES), SUBLANES), :]

    def slot_wait(sl):
        for _ in range(TOPK_E):
            pltpu.make_async_copy(hbuf.at[sl], xs_hbm.at[pl.ds(0, ts * SUBLANES), :], sem.at[sl]).wait()

    @pl.when(i == 0)
    def _():
        zero_ref[...] = jnp.zeros_like(zero_ref)
        for phase in ("start", "wait"):
            def seg_body(sg, c):
                def row_body(r, cc):
                    cp = pltpu.make_async_copy(zero_ref, row_tile(seg_ref[sg] + r), zsem.at[0])
                    cp.start() if phase == "start" else cp.wait()
                    return cc
                return lax.fori_loop(0, seg_ref[n_seg + sg], row_body, c)
            lax.fori_loop(0, n_seg, seg_body, 0)

    @pl.when(i >= 2)
    def _():
        slot_wait(slot)

    h = _ada_norm(x_ref[0], g_ref[...], sh_ref[0], sc_ref[0])
    for s in range(SUBLANES):
        hbuf[slot, pl.ds(s, ts, stride=SUBLANES), :] = h[:, s * LANES:(s + 1) * LANES]
    for r in range(ts):
        for k in range(TOPK_E):
            pltpu.make_async_copy(hbuf.at[slot, pl.ds(r * SUBLANES, SUBLANES), :],
                                  row_tile(pos_ref[0, 0, TOPK_E * r + k]), sem.at[slot]).start()

    @pl.when(i == n - 1)
    def _():
        slot_wait(slot)

        @pl.when(n >= 2)
        def _():
            slot_wait(1 - slot)


def moe_dispatch(x, g, shift, scale, pos, seg, n_rows, ts=256):
    b, s, d = x.shape
    assert d == SUBLANES * LANES
    ts = min(ts, s)
    nj = s // ts
    pos3 = pos.reshape(b * nj, 1, TOPK_E * ts)
    grid_spec = pltpu.PrefetchScalarGridSpec(
        num_scalar_prefetch=1,
        grid=(b, nj),
        in_specs=[pl.BlockSpec((1, 1, TOPK_E * ts), lambda bb, j, sg: (bb * nj + j, 0, 0), memory_space=pltpu.SMEM),
                  pl.BlockSpec((1, ts, d), lambda bb, j, sg: (bb, j, 0)),
                  pl.BlockSpec(g.shape, lambda bb, j, sg: (0, 0)),
                  (pl.BlockSpec((1, 1, d), lambda bb, j, sg: (bb, 0, 0)) if shift.shape[1] == 1
                   else pl.BlockSpec((1, ts, d), lambda bb, j, sg: (bb, j, 0))),
                  (pl.BlockSpec((1, 1, d), lambda bb, j, sg: (bb, 0, 0)) if scale.shape[1] == 1
                   else pl.BlockSpec((1, ts, d), lambda bb, j, sg: (bb, j, 0)))],
        out_specs=pl.BlockSpec(memory_space=pl.ANY),
        scratch_shapes=[pltpu.VMEM((2, ts * SUBLANES, LANES), F32), pltpu.VMEM((SUBLANES, LANES), F32),
                        pltpu.SemaphoreType.DMA((2,)), pltpu.SemaphoreType.DMA((1,))])
    return pl.pallas_call(
        functools.partial(_dispatch_kernel, ts=ts),
        grid_spec=grid_spec,
        out_shape=jax.ShapeDtypeStruct((n_rows * SUBLANES, LANES), F32),
        compiler_params=_cparams(("arbitrary", "arbitrary"), 40),
        name="moe_dispatch",
    )(seg, pos3, x, g, shift, scale)


def _row_copy(src_hbm, t, buf, slot, r, sem):
    return pltpu.make_async_copy(src_hbm.at[pl.ds(pl.multiple_of(t * SUBLANES, SUBLANES), SUBLANES), :],
                                 buf.at[slot, pl.ds(r * SUBLANES, SUBLANES), :], sem.at[slot])


def _rows_wait(src_hbm, buf, slot, sem):
    pltpu.make_async_copy(src_hbm.at[pl.ds(0, buf.shape[1]), :], buf.at[slot], sem.at[slot]).wait()


def _gathered_rows(buf, slot, n_rows, start, step):
    return jnp.concatenate([buf[slot, pl.ds(start * SUBLANES + s, n_rows, stride=step * SUBLANES), :]
                            for s in range(SUBLANES)], axis=1)


def _expert_kernel(be_ref, x_ref, wgu_ref, wd_ref, y_ref, *, rows):
    x = jnp.concatenate([x_ref[pl.ds(s, rows, stride=SUBLANES), :] for s in range(SUBLANES)], axis=1)
    ab = _dot(x, wgu_ref[0])
    f = ab.shape[1] // 2
    a, bb = ab[:, :f], ab[:, f:]
    y = _dot(a * jax.nn.sigmoid(a) * bb, wd_ref[0])
    for s in range(SUBLANES):
        y_ref[pl.ds(s, rows, stride=SUBLANES), :] = y[:, s * LANES:(s + 1) * LANES]


def moe_experts(x_rows, blk_e, w_gate_up, w_down, rows):
    n_blk = blk_e.shape[0]
    ne, d, f2 = w_gate_up.shape
    grid_spec = pltpu.PrefetchScalarGridSpec(
        num_scalar_prefetch=1,
        grid=(n_blk,),
        in_specs=[pl.BlockSpec((rows * SUBLANES, LANES), lambda i, be: (i, 0)),
                  pl.BlockSpec((1, d, f2), lambda i, be: (be[i], 0, 0)),
                  pl.BlockSpec((1, f2 // 2, d), lambda i, be: (be[i], 0, 0))],
        out_specs=pl.BlockSpec((rows * SUBLANES, LANES), lambda i, be: (i, 0)))
    return pl.pallas_call(
        functools.partial(_expert_kernel, rows=rows),
        grid_spec=grid_spec,
        out_shape=jax.ShapeDtypeStruct(x_rows.shape, F32),
        compiler_params=_cparams(("arbitrary",), 40),
        name="moe_experts",
    )(blk_e, x_rows, w_gate_up, w_down)


def _combine_kernel(cur_ref, nxt_ref, x_ref, gate_ref, route_ref, y_hbm, *rest, ts, final):
    if final:
        fg_ref, o_ref, ybuf, sem = rest
    else:
        o_ref, ybuf, sem = rest
    i = pl.program_id(0) * pl.num_programs(1) + pl.program_id(1)
    n = pl.num_programs(0) * pl.num_programs(1)
    slot = i % 2

    def fetch(src_ref, sl):
        for r in range(TOPK_E * ts):
            _row_copy(y_hbm, src_ref[0, 0, r], ybuf, sl, r, sem).start()

    @pl.when(i == 0)
    def _():
        fetch(cur_ref, 0)

    @pl.when(i + 1 < n)
    def _():
        fetch(nxt_ref, 1 - slot)

    _rows_wait(y_hbm, ybuf, slot, sem)
    route = route_ref[...]
    y = (route[:, 2:3] * _gathered_rows(ybuf, slot, ts, 0, TOPK_E)
         + route[:, 3:4] * _gathered_rows(ybuf, slot, ts, 1, TOPK_E))
    out = x_ref[0] + gate_ref[0] * y
    if final:
        out = _rms(out, fg_ref[...])
    o_ref[0] = out


def moe_combine(x, gate, route, y_rows, row_of_item, final_g=None, ts=128):
    b, s, d = x.shape
    nj = s // ts
    n_tiles = b * nj
    rows3 = row_of_item.reshape(n_tiles, 1, TOPK_E * ts)
    smem_blk = lambda f: pl.BlockSpec((1, 1, TOPK_E * ts), f, memory_space=pltpu.SMEM)
    ins = [rows3, rows3, x, gate, route, y_rows]
    specs = [smem_blk(lambda bb, j: (bb * nj + j, 0, 0)),
             smem_blk(lambda bb, j: (jnp.minimum(bb * nj + j + 1, n_tiles - 1), 0, 0)),
             _row_spec(x, ts), _mod_spec(gate, ts),
             pl.BlockSpec((ts, LANES), lambda bb, j: (bb * nj + j, 0)),
             pl.BlockSpec(memory_space=pl.ANY)]
    if final_g is not None:
        ins.append(final_g)
        specs.append(_full_spec(final_g))
    return pl.pallas_call(
        functools.partial(_combine_kernel, ts=ts, final=final_g is not None),
        grid=(b, nj),
        in_specs=specs,
        out_specs=_row_spec(x, ts),
        out_shape=jax.ShapeDtypeStruct(x.shape, F32),
        scratch_shapes=[pltpu.VMEM((2, TOPK_E * ts * SUBLANES, LANES), F32), pltpu.SemaphoreType.DMA((2,))],
        compiler_params=_cparams(("arbitrary", "arbitrary"), 40),
        name="moe_combine",
    )(*ins)


def _route_layout(route, counts, rows):
    n_items = route.shape[0] * TOPK_E
    eid = route[:, :TOPK_E].astype(I32)
    rank = route[:, 2 * TOPK_E:3 * TOPK_E].astype(I32)
    counts = counts[0, :N_EXPERTS].astype(I32)
    padded = (counts + rows - 1) // rows * rows
    pend = jnp.cumsum(padded)
    pstart = pend - padded
    experts = jnp.arange(N_EXPERTS, dtype=I32)
    pos = jnp.sum(jnp.where(eid[:, :, None] == experts, pstart, 0), axis=-1) + rank
    n_blk = -(-(n_items + min(N_EXPERTS, n_items) * (rows - 1)) // rows)
    blk_e = jnp.minimum(jnp.searchsorted(pend, jnp.arange(n_blk, dtype=I32) * rows, side='right'), N_EXPERTS - 1)
    seg = jnp.concatenate([pstart + counts, pend[-1:], padded - counts, n_blk * rows - pend[-1:]])
    return pos.astype(I32), blk_e.astype(I32), seg.astype(I32), n_blk * rows


def hier_moe(x, g, shift, scale, gate, w_group, b_group, w_expert, b_expert, w_gate_up, w_down, rows,
             final_g=None, ts_router=512, ts_combine=128):
    route, counts = moe_router(x, g, shift, scale, w_group, b_group, w_expert, b_expert, ts=ts_router)
    pos, blk_e, seg, n_rows = _route_layout(route, counts, rows)
    x_rows = moe_dispatch(x, g, shift, scale, pos, seg, n_rows)
    y_rows = moe_experts(x_rows, blk_e, w_gate_up, w_down, rows)
    return moe_combine(x, gate, route, y_rows, pos, final_g=final_g, ts=ts_combine)


def kernel(x_prompt, x_sample, cache_moba_k, cache_moba_v, cache_mla_ckv, cache_mla_kpe, state_conv, page_table, c_prompt, c_sample, norm_mix_g, norm_ffn_g, w_ada, b_ada, rel_bias, moba_w_qkv, moba_w_o, mla_w_in, mla_q_norm_g, mla_w_qb, mla_kv_norm_g, mla_w_kvb, mla_w_o, chunk_w_in, chunk_ln_g, chunk_ln_b, chunk_w_s, chunk_b_s, chunk_w_out, conv_w_in, conv_w, conv_w_out, moe_w_group, moe_b_group, moe_w_expert, moe_b_expert, moe_w_gate_up, moe_w_down, final_norm_g):
    bp, sp, d = x_prompt.shape
    db, t_new, _ = x_sample.shape
    ns = db * t_new
    depth = w_ada.shape[0]
    n_pages = page_table.shape[1]
    page = cache_moba_k.shape[1]
    past = n_pages * page

    mods = ada_mods(jnp.concatenate([c_prompt, c_sample], axis=0), w_ada, b_ada)

    def mod_p(i, k):
        return mods[i, :bp, k * d:(k + 1) * d].reshape(bp, 1, d)

    def mod_s(i, k):
        return jnp.repeat(mods[i, bp:, k * d:(k + 1) * d], t_new, axis=0).reshape(1, ns, d)

    xp = x_prompt
    xs = x_sample.reshape(1, ns, d)
    bf = lambda w: w.astype(BF16)
    outs = {}

    for i in range(depth):
        kind = i % 4
        g_mix = norm_mix_g[i].reshape(1, d)
        g_ffn = norm_ffn_g[i].reshape(1, d)
        mp = [mod_p(i, k) for k in range(6)]
        ms = [mod_s(i, k) for k in range(6)]
        if kind == 0:
            nq, nkv = HQ_A * DH_A, HKV_A * DH_A
            offs = [(0, nq), (nq, nkv), (nq + nkv, nkv)]
            w_qkv, w_o = bf(moba_w_qkv), bf(moba_w_o)
            q, k, v = norm_mod_matmul(xp, g_mix, mp[0], mp[1], w_qkv, offs, [F32] * 3, ts=512)
            o = moba_prompt_attention(q, k, v, rel_bias)
            xp = out_proj(xp, mp[2], o, w_o)
            outs['moba_k_prompt'] = k.reshape(bp, sp, HKV_A, DH_A)
            outs['moba_v_prompt'] = v.reshape(bp, sp, HKV_A, DH_A)
            q, k, v = norm_mod_matmul(xs, g_mix, ms[0], ms[1], w_qkv, offs, [F32] * 3, ts=ns)
            o = moba_sample_attention(q.reshape(db, t_new, nq), k.reshape(db, t_new, nkv), v.reshape(db, t_new, nkv),
                                      cache_moba_k.reshape(-1, page, nkv), cache_moba_v.reshape(-1, page, nkv),
                                      page_table, rel_bias)
            xs = out_proj(xs, ms[2], o.reshape(1, ns, nq), w_o, ts=ns)
            outs['moba_k_sample'] = k.reshape(db, t_new, HKV_A, DH_A)
            outs['moba_v_sample'] = v.reshape(db, t_new, HKV_A, DH_A)
        elif kind == 1:
            w_v = bf(mla_w_kvb[:, :, NOPE_B:].transpose(1, 0, 2))
            w_o = bf(mla_w_o)
            proj = (mla_w_in, mla_q_norm_g, mla_w_qb, mla_kv_norm_g, mla_w_kvb)
            pos_p = jnp.broadcast_to(jnp.arange(sp, dtype=I32)[None], (bp, sp))
            qh, kv, ckv, kpe = mla_in(xp, g_mix, mp[0], mp[1], *proj, pos_p, ts=256)
            o_lat = mla_prompt_attention(qh, kv)
            xp = out_proj(xp, mp[2], o_lat, w_o, w_v=w_v, ts=256)
            outs['mla_ckv_prompt'], outs['mla_kpe_prompt'] = ckv, kpe
            pos_s = jnp.tile(past + jnp.arange(t_new, dtype=I32), db)[None]
            qh, _, ckv, kpe = mla_in(xs, g_mix, ms[0], ms[1], *proj, pos_s, ts=ns)
            lat = ckv.shape[-1]
            qs = qh.reshape(HEADS_B, db, t_new, lat + ROPE_B).transpose(1, 0, 2, 3).reshape(db, HEADS_B * t_new, -1)
            o_lat = mla_sample_attention(qs, ckv.reshape(db, t_new, lat), kpe.reshape(db, t_new, ROPE_B),
                                         cache_mla_ckv, cache_mla_kpe, page_table)
            o_lat = o_lat.reshape(db, HEADS_B, t_new, lat).transpose(0, 2, 1, 3).reshape(1, ns, HEADS_B * lat)
            xs = out_proj(xs, ms[2], o_lat, w_o, w_v=w_v, ts=ns)
            outs['mla_ckv_sample'] = ckv.reshape(db, t_new, lat)
            outs['mla_kpe_sample'] = kpe.reshape(db, t_new, ROPE_B)
        elif kind == 2:
            w_in, w_out = bf(chunk_w_in), bf(chunk_w_out)
            gw = chunk_ln_g.shape[0] // GROUPS_C
            bmix_p = jnp.repeat(chunk_b_s.T, gw, axis=1)
            xp, _ = gmlp(xp, g_mix, mp[0], mp[1], mp[2], w_in, chunk_ln_g, chunk_ln_b, chunk_w_s, bmix_p, w_out,
                         want_v=False)
            reps = CHUNK_C // t_new
            eye = jnp.eye(reps, dtype=F32)
            wmix_s = jnp.einsum('ab,gij->gaibj', eye, chunk_w_s[:, :t_new, :t_new]).reshape(GROUPS_C, CHUNK_C, CHUNK_C)
            bmix_s = jnp.repeat(jnp.tile(chunk_b_s[:, :t_new].T, (reps, 1)), gw, axis=1)
            xs, v_s = gmlp(xs, g_mix, ms[0], ms[1], ms[2], w_in, chunk_ln_g, chunk_ln_b, wmix_s, bmix_s, w_out,
                           want_v=True)
            outs['chunk_v_sample'] = v_s.reshape(db, t_new, -1)
        else:
            w_in, w_out = bf(conv_w_in), bf(conv_w_out)
            xp, st = short_conv(xp, g_mix, mp[0], mp[1], mp[2], w_in, conv_w, w_out)
            outs['conv_prompt'] = st
            f1 = jnp.repeat(state_conv[:, 1], t_new, axis=0).reshape(1, ns, d)
            f2 = jnp.tile(state_conv, (1, t_new // (CONV_W - 1), 1)).reshape(1, ns, d)
            xs, zc = short_conv(xs, g_mix, ms[0], ms[1], ms[2], w_in, conv_w, w_out, fix=(f1, f2), period=t_new,
                                ts=ns)
            outs['conv_sample'] = zc.reshape(db, t_new, d)[:, t_new - (CONV_W - 1):]
        fg = final_norm_g.reshape(1, d) if i == depth - 1 else None
        moe_w = (moe_w_group[i], moe_b_group[i], moe_w_expert[i], moe_b_expert[i],
                 bf(moe_w_gate_up[i]), bf(moe_w_down[i]))
        xp = hier_moe(xp, g_ffn, mp[3], mp[4], mp[5], *moe_w, rows=256, final_g=fg)
        xs = hier_moe(xs, g_ffn, ms[3], ms[4], ms[5], *moe_w, rows=128, final_g=fg, ts_router=ns)

    return (xp, xs.reshape(db, t_new, d),
            outs['moba_k_prompt'], outs['moba_v_prompt'], outs['moba_k_sample'], outs['moba_v_sample'],
            outs['mla_ckv_prompt'], outs['mla_kpe_prompt'], outs['mla_ckv_sample'], outs['mla_kpe_sample'],
            outs['chunk_v_sample'], outs['conv_prompt'], outs['conv_sample'])
```

```python
import functools
import math

import numpy as np
import jax
import jax.numpy as jnp
from jax import lax
from jax.experimental import pallas as pl
from jax.experimental.pallas import tpu as pltpu

F32, BF16, I32 = jnp.float32, jnp.bfloat16, jnp.int32
EPS = 1e-6
NEG_INF = -1e30
LANES = 128
SUBLANES = 8

HQ_A, HKV_A, GRP_A, DH_A = 16, 4, 4, 64
BLOCK_A, TOPK_A = 256, 3
NUM_BUCKETS, MAX_DISTANCE = 32, 128
HEADS_B, NOPE_B, ROPE_B, V_B = 16, 64, 32, 64
ROPE_THETA = 10000.0
CHUNK_C, GROUPS_C = 128, 8
CONV_W = 3
N_GROUPS_E, EXPERTS_PER_GROUP, TOPK_E = 4, 8, 2
N_EXPERTS = N_GROUPS_E * EXPERTS_PER_GROUP

NT_DIMS = (((1,), (1,)), ((), ()))
HIGHEST = lax.Precision.HIGHEST


def _cparams(sem, vmem_mb):
    return pltpu.CompilerParams(dimension_semantics=sem, vmem_limit_bytes=vmem_mb << 20)


def _dot(a, b):
    return jnp.dot(a.astype(BF16), b.astype(BF16), preferred_element_type=F32)


def _dot_nt(a, b):
    return lax.dot_general(a.astype(BF16), b.astype(BF16), NT_DIMS, preferred_element_type=F32)


def _rms(x, g):
    return x * lax.rsqrt(jnp.mean(x * x, axis=-1, keepdims=True) + EPS) * g


def _ada_norm(x, g, shift, scale):
    return _rms(x, g) * (1.0 + scale) + shift


def _mod_spec(m, ts):
    d = m.shape[-1]
    if m.shape[1] == 1:
        return pl.BlockSpec((1, 1, d), lambda b, j: (b, 0, 0))
    return pl.BlockSpec((1, ts, d), lambda b, j: (b, j, 0))


def _full_spec(a):
    nd = a.ndim
    return pl.BlockSpec(a.shape, lambda *_: (0,) * nd)


def _row_spec(a, ts):
    return pl.BlockSpec((1, ts) + a.shape[2:], lambda b, j: (b, j) + (0,) * (a.ndim - 2))


def _ada_kernel(c_ref, w_ref, b_ref, o_ref):
    c = c_ref[...]
    o_ref[0] = _dot(c * jax.nn.sigmoid(c), w_ref[0]) + b_ref[0]


def ada_mods(c_all, w_ada, b_ada):
    depth, d, n = w_ada.shape
    rows = c_all.shape[0]
    tn = 1536
    return pl.pallas_call(
        _ada_kernel,
        grid=(depth, n // tn),
        in_specs=[pl.BlockSpec((rows, d), lambda i, j: (0, 0)),
                  pl.BlockSpec((1, d, tn), lambda i, j: (i, 0, j)),
                  pl.BlockSpec((1, 1, tn), lambda i, j: (i, 0, j))],
        out_specs=pl.BlockSpec((1, rows, tn), lambda i, j: (i, 0, j)),
        out_shape=jax.ShapeDtypeStruct((depth, rows, n), F32),
        compiler_params=_cparams(("arbitrary", "arbitrary"), 40),
        name="ada_mods",
    )(c_all, w_ada, b_ada.reshape(depth, 1, n))


def _nmm_kernel(x_ref, g_ref, sh_ref, sc_ref, w_ref, *out_refs, offs):
    h = _ada_norm(x_ref[0], g_ref[...], sh_ref[0], sc_ref[0])
    y = _dot(h, w_ref[...])
    for o_ref, (lo, n) in zip(out_refs, offs):
        o_ref[0] = y[:, lo:lo + n].astype(o_ref.dtype)


def norm_mod_matmul(x, g, shift, scale, w, offs, dtypes, ts):
    b, s, d = x.shape
    outs = [jax.ShapeDtypeStruct((b, s, n), dt) for (_, n), dt in zip(offs, dtypes)]
    return pl.pallas_call(
        functools.partial(_nmm_kernel, offs=tuple(offs)),
        grid=(b, s // ts),
        in_specs=[_row_spec(x, ts), _full_spec(g), _mod_spec(shift, ts), _mod_spec(scale, ts), _full_spec(w)],
        out_specs=[pl.BlockSpec((1, ts, n), lambda bb, j: (bb, j, 0)) for _, n in offs],
        out_shape=outs,
        compiler_params=_cparams(("arbitrary", "arbitrary"), 48),
        name="norm_mod_matmul",
    )(x, g, shift, scale, w)


def _out_proj_kernel(x_ref, gate_ref, a_ref, *rest, n_heads):
    if n_heads:
        wv_ref, wo_ref, o_ref = rest
        a = a_ref[0]
        c = a.shape[1] // n_heads
        a = jnp.concatenate([_dot(a[:, h * c:(h + 1) * c], wv_ref[h]) for h in range(n_heads)], axis=1)
    else:
        wo_ref, o_ref = rest
        a = a_ref[0]
    o_ref[0] = x_ref[0] + gate_ref[0] * _dot(a, wo_ref[...])


def out_proj(x, gate, a, w_o, w_v=None, ts=512):
    b, s, d = x.shape
    ins = [x, gate, a] + ([w_v] if w_v is not None else []) + [w_o]
    specs = [_row_spec(x, ts), _mod_spec(gate, ts), _row_spec(a, ts)]
    specs += ([_full_spec(w_v)] if w_v is not None else []) + [_full_spec(w_o)]
    return pl.pallas_call(
        functools.partial(_out_proj_kernel, n_heads=0 if w_v is None else w_v.shape[0]),
        grid=(b, s // ts),
        in_specs=specs,
        out_specs=_row_spec(x, ts),
        out_shape=jax.ShapeDtypeStruct(x.shape, F32),
        compiler_params=_cparams(("arbitrary", "arbitrary"), 48),
        name="out_proj",
    )(*ins)


def _np_bucket(dist):
    n = np.maximum(dist, 0)
    max_exact = NUM_BUCKETS // 2
    ratio = np.log(np.maximum(n, 1).astype(np.float32) / max_exact) / math.log(MAX_DISTANCE / max_exact)
    large = np.minimum(max_exact + (ratio * (NUM_BUCKETS - max_exact)).astype(np.int32), NUM_BUCKETS - 1)
    return np.where(n < max_exact, n, large).astype(np.int32)


assert int(_np_bucket(np.array(BLOCK_A + 1))) == NUM_BUCKETS - 1


def _bias_table(rel_bias, dist):
    idx = _np_bucket(dist)
    onehot = (jnp.asarray(idx.reshape(-1, 1)) == jnp.arange(NUM_BUCKETS, dtype=I32)[None, :]).astype(F32)
    tab = jnp.einsum('bh,nb->hn', rel_bias, onehot, precision=HIGHEST)
    return tab.reshape((rel_bias.shape[1],) + idx.shape)


def _halves_max(s):
    return jnp.maximum(s[:, :LANES], s[:, LANES:])


def _two_pass_finish(s_ref, mx_ref, l_ref, acc_ref, n_chunks, value_chunk):
    m = jnp.max(mx_ref[...], axis=1, keepdims=True)
    mx_ref[...] = jnp.broadcast_to(m, mx_ref.shape)
    l_ref[...] = jnp.zeros_like(l_ref)
    acc_ref[...] = jnp.zeros_like(acc_ref)

    def body(j, c):
        mb = mx_ref[...]
        p = jnp.exp(s_ref[j] - jnp.concatenate([mb, mb], axis=1))
        l_ref[...] += p[:, :LANES] + p[:, LANES:]
        acc_ref[...] += jnp.dot(p.astype(BF16), value_chunk(j), preferred_element_type=F32)
        return c

    lax.fori_loop(0, n_chunks, body, 0)
    return acc_ref[...] * (1.0 / jnp.sum(l_ref[...], axis=1, keepdims=True))


def _moba_prompt_kernel(far_ref, q_ref, k_ref, v_ref, bown_ref, badj_ref, o_ref,
                        kmbd_ref, mb_ref, s_ref, mx_ref, l_ref, acc_ref, *, nb):
    i = pl.program_id(1)
    bs = BLOCK_A
    scale = DH_A ** -0.5

    @pl.when(i == 0)
    def _():
        kmbd_ref[...] = jnp.zeros_like(kmbd_ref)
        for j in range(nb):
            km = jnp.mean(k_ref[0, j * bs:(j + 1) * bs, :], axis=0, keepdims=True)
            for hq in range(HQ_A):
                kvh = hq // GRP_A
                kmbd_ref[j * HQ_A + hq:j * HQ_A + hq + 1, hq * DH_A:(hq + 1) * DH_A] = km[:, kvh * DH_A:(kvh + 1) * DH_A]

    q = q_ref[0]
    gates = lax.dot_general(q, kmbd_ref[...], NT_DIMS, precision=HIGHEST, preferred_element_type=F32)
    lane = lax.broadcasted_iota(I32, gates.shape, 1)
    jidx = lane // HQ_A
    past = jidx < i
    gm = jnp.where(past, gates, NEG_INF)
    rank = jnp.zeros(gates.shape, I32)
    for k in range(1, nb):
        other = pltpu.roll(gm, HQ_A * k, axis=1)
        jo = jnp.where(jidx >= k, jidx - k, jidx - k + nb)
        beats = (other > gm) | ((other == gm) & (jo < jidx))
        rank = rank + beats.astype(I32)
    mb = jnp.where(past & (rank < TOPK_A), 0.0, NEG_INF).astype(F32)
    for j in range(nb):
        mb_ref[j] = mb[:, j * HQ_A:(j + 1) * HQ_A]

    def kv_block(j, h):
        lo = pl.multiple_of(j * bs, bs)
        return (k_ref[0, pl.ds(lo, bs), h * DH_A:(h + 1) * DH_A].astype(BF16),
                v_ref[0, pl.ds(lo, bs), h * DH_A:(h + 1) * DH_A].astype(BF16))

    for h in range(HKV_A):
        heads = [GRP_A * h + g for g in range(GRP_A)]
        q4 = (jnp.concatenate([q[:, hq * DH_A:(hq + 1) * DH_A] for hq in heads], axis=0) * scale).astype(BF16)

        def mask_col(j, extra):
            mj = mb_ref[j]
            return jnp.concatenate([mj[:, hq:hq + 1] + (extra[n] if extra else 0.0)
                                    for n, hq in enumerate(heads)], axis=0)

        s = _dot_nt(q4, kv_block(i, h)[0]) + bown_ref[GRP_A * h:GRP_A * (h + 1)].reshape(GRP_A * bs, bs)
        s_ref[i] = s
        mx_ref[...] = _halves_max(s)

        def scores(j, bias):
            sj = _dot_nt(q4, kv_block(j, h)[0]) + bias
            s_ref[j] = sj
            mx_ref[...] = jnp.maximum(mx_ref[...], _halves_max(sj))

        @pl.when(i >= 1)
        def _():
            badj = badj_ref[GRP_A * h:GRP_A * (h + 1)].reshape(GRP_A * bs, bs)
            scores(i - 1, badj + mask_col(i - 1, None))

        far = [far_ref[hq] for hq in heads]

        def far_body(j, c):
            scores(j, mask_col(j, far))
            return c

        lax.fori_loop(0, jnp.maximum(i - 1, 0), far_body, 0)

        o = _two_pass_finish(s_ref, mx_ref, l_ref, acc_ref, i + 1, lambda j: kv_block(j, h)[1])
        for g, hq in enumerate(heads):
            o_ref[0, :, hq * DH_A:(hq + 1) * DH_A] = o[g * bs:(g + 1) * bs].astype(o_ref.dtype)


def moba_prompt_attention(q, k, v, rel_bias):
    b, s, _ = q.shape
    nb = s // BLOCK_A
    assert s % BLOCK_A == 0 and nb * HQ_A == LANES and nb >= TOPK_A
    t = np.arange(BLOCK_A)[:, None]
    sk = np.arange(BLOCK_A)[None, :]
    bown = jnp.where(jnp.asarray(t >= sk)[None], _bias_table(rel_bias, t - sk), NEG_INF)
    badj = _bias_table(rel_bias, BLOCK_A + t - sk)
    far = rel_bias[NUM_BUCKETS - 1]
    kv_spec = pl.BlockSpec((1, s, HKV_A * DH_A), lambda bb, i: (bb, 0, 0))
    return pl.pallas_call(
        functools.partial(_moba_prompt_kernel, nb=nb),
        grid=(b, nb),
        in_specs=[pl.BlockSpec(memory_space=pltpu.SMEM),
                  pl.BlockSpec((1, BLOCK_A, HQ_A * DH_A), lambda bb, i: (bb, i, 0)),
                  kv_spec, kv_spec, _full_spec(bown), _full_spec(badj)],
        out_specs=pl.BlockSpec((1, BLOCK_A, HQ_A * DH_A), lambda bb, i: (bb, i, 0)),
        out_shape=jax.ShapeDtypeStruct((b, s, HQ_A * DH_A), BF16),
        scratch_shapes=[pltpu.VMEM((nb * HQ_A, HQ_A * DH_A), F32),
                        pltpu.VMEM((nb, BLOCK_A, HQ_A), F32),
                        pltpu.VMEM((nb, GRP_A * BLOCK_A, BLOCK_A), F32),
                        pltpu.VMEM((GRP_A * BLOCK_A, LANES), F32),
                        pltpu.VMEM((GRP_A * BLOCK_A, LANES), F32),
                        pltpu.VMEM((GRP_A * BLOCK_A, DH_A), F32)],
        compiler_params=_cparams(("arbitrary", "arbitrary"), 48),
        name="moba_prompt_attention",
    )(far, q, k, v, bown, badj)


def _page_copies(cache_hbm, buf, sem, pt_ref, bb, slot, n_pages):
    return [pltpu.make_async_copy(cache_hbm.at[pt_ref[bb * n_pages + p]], buf.at[slot, p], sem.at[slot])
            for p in range(n_pages)]


def _pages_wait(cache_hbm, buf, sem, slot, n_pages):
    pltpu.make_async_copy(cache_hbm.at[pl.ds(0, n_pages)], buf.at[slot], sem.at[slot]).wait()


def _new_token_partials(qs, kn, vn, bnew, n_new):
    qf = qs.astype(BF16).astype(F32)
    kf = kn.astype(BF16).astype(F32)
    vf = vn.astype(BF16).astype(F32)
    cols = [jnp.sum(qf * kf[s:s + 1, :], axis=1, keepdims=True) + bnew[:, s:s + 1] for s in range(n_new)]
    m = functools.reduce(jnp.maximum, cols)
    ps = [jnp.exp(c - m) for c in cols]
    l = functools.reduce(jnp.add, ps)
    acc = functools.reduce(jnp.add, [p.astype(BF16).astype(F32) * vf[s:s + 1, :] for s, p in enumerate(ps)])
    return m, l, acc


def _moba_sample_kernel(pt_ref, q_ref, kn_ref, vn_ref, far_ref, blast_ref, bnew_ref, ck_hbm, cv_hbm, o_ref,
                        kbuf, vbuf, ksem, vsem, qbd_ref, *, n_pages, n_new):
    b = pl.program_id(0)
    slot = b % 2
    page = kbuf.shape[3]
    nb = n_pages * page // BLOCK_A
    ppb = n_pages // nb
    scale = DH_A ** -0.5

    def fetch(bb, sl):
        for c in _page_copies(ck_hbm, kbuf, ksem, pt_ref, bb, sl, n_pages):
            c.start()
        for c in _page_copies(cv_hbm, vbuf, vsem, pt_ref, bb, sl, n_pages):
            c.start()

    @pl.when(b == 0)
    def _():
        fetch(0, 0)

    @pl.when(b + 1 < pl.num_programs(0))
    def _():
        fetch(b + 1, 1 - slot)

    q = q_ref[0]
    qbd_ref[...] = jnp.zeros_like(qbd_ref)
    for hq in range(HQ_A):
        kvh = hq // GRP_A
        qbd_ref[hq * n_new:(hq + 1) * n_new, kvh * DH_A:(kvh + 1) * DH_A] = q[:, hq * DH_A:(hq + 1) * DH_A]
    qbd = qbd_ref[...]
    qs = qbd * scale

    carry = _new_token_partials(qs, kn_ref[0], vn_ref[0], bnew_ref[...], n_new)

    _pages_wait(ck_hbm, kbuf, ksem, slot, n_pages)
    rows = qbd.shape[0]
    qsb = qs.astype(BF16)
    raw = [jnp.dot(qsb, kbuf[slot, p].astype(BF16), preferred_element_type=F32) for p in range(n_pages)]

    lane = lax.broadcasted_iota(I32, (rows, LANES), 1)
    gcur = jnp.full((rows, LANES), NEG_INF, F32)
    for j in range(nb):
        blk = functools.reduce(jnp.add, raw[j * ppb:(j + 1) * ppb])
        gcur = jnp.where(lane == j, jnp.sum(blk, axis=1, keepdims=True) * (1.0 / BLOCK_A), gcur)
    sel = jnp.zeros(gcur.shape, jnp.bool_)
    for _ in range(TOPK_A):
        mx = jnp.max(gcur, axis=1, keepdims=True)
        idx = jnp.min(jnp.where(gcur == mx, lane, LANES), axis=1, keepdims=True)
        pick = lane == idx
        sel = sel | (pick & (mx > 0.5 * NEG_INF))
        gcur = jnp.where(pick, -3e38, gcur)
    selbias = jnp.where(sel, 0.0, NEG_INF).astype(F32)

    _pages_wait(cv_hbm, vbuf, vsem, slot, n_pages)
    far = far_ref[...]

    def page_bias(p):
        j = p // ppb
        col = selbias[:, j:j + 1]
        if j == nb - 1:
            lo = (p % ppb) * page
            return blast_ref[:, lo:lo + page] + col
        return jnp.broadcast_to(col + far, (rows, page))

    scores = [raw[p] + page_bias(p) for p in range(n_pages)]
    m0, l0, acc0 = carry
    m = jnp.maximum(m0, jnp.max(functools.reduce(jnp.maximum, scores), axis=1, keepdims=True))
    alpha = jnp.exp(m0 - m)
    lsum = jnp.zeros((rows, page), F32)
    acc = alpha * acc0
    for p in range(n_pages):
        pr = jnp.exp(scores[p] - m)
        lsum = lsum + pr
        acc = acc + _dot_nt(pr, vbuf[slot, p])
    l = alpha * l0 + jnp.sum(lsum, axis=1, keepdims=True)

    o = acc * (1.0 / l)
    for hq in range(HQ_A):
        kvh = hq // GRP_A
        o_ref[0, :, hq * DH_A:(hq + 1) * DH_A] = (
            o[hq * n_new:(hq + 1) * n_new, kvh * DH_A:(kvh + 1) * DH_A].astype(o_ref.dtype))


def moba_sample_attention(q, k_new, v_new, cache_k, cache_v, page_table, rel_bias):
    db, t_new, _ = q.shape
    n_pages = page_table.shape[1]
    page = cache_k.shape[2]
    past = n_pages * page
    assert past % BLOCK_A == 0 and BLOCK_A % page == 0 and t_new <= BLOCK_A
    nb = past // BLOCK_A
    assert TOPK_A <= nb <= LANES
    tt = np.arange(t_new)[:, None]
    blast = _bias_table(rel_bias, BLOCK_A + tt - np.arange(BLOCK_A)[None, :]).reshape(HQ_A * t_new, BLOCK_A)
    ts = np.arange(t_new)[None, :]
    bnew = jnp.where(jnp.asarray(tt >= ts)[None], _bias_table(rel_bias, tt - ts), NEG_INF)
    bnew = bnew.reshape(HQ_A * t_new, t_new)
    far = jnp.repeat(rel_bias[NUM_BUCKETS - 1], t_new).reshape(HQ_A * t_new, 1)
    rows = HQ_A * t_new
    kw = HKV_A * DH_A
    grid_spec = pltpu.PrefetchScalarGridSpec(
        num_scalar_prefetch=1,
        grid=(db,),
        in_specs=[pl.BlockSpec((1, t_new, HQ_A * DH_A), lambda b, pt: (b, 0, 0)),
                  pl.BlockSpec((1, t_new, kw), lambda b, pt: (b, 0, 0)),
                  pl.BlockSpec((1, t_new, kw), lambda b, pt: (b, 0, 0)),
                  pl.BlockSpec((rows, 1), lambda b, pt: (0, 0)),
                  pl.BlockSpec((rows, BLOCK_A), lambda b, pt: (0, 0)),
                  pl.BlockSpec((rows, t_new), lambda b, pt: (0, 0)),
                  pl.BlockSpec(memory_space=pl.ANY),
                  pl.BlockSpec(memory_space=pl.ANY)],
        out_specs=pl.BlockSpec((1, t_new, HQ_A * DH_A), lambda b, pt: (b, 0, 0)),
        scratch_shapes=[pltpu.VMEM((2, n_pages, kw, page), F32),
                        pltpu.VMEM((2, n_pages, kw, page), F32),
                        pltpu.SemaphoreType.DMA((2,)),
                        pltpu.SemaphoreType.DMA((2,)),
                        pltpu.VMEM((rows, kw), F32)])
    return pl.pallas_call(
        functools.partial(_moba_sample_kernel, n_pages=n_pages, n_new=t_new),
        grid_spec=grid_spec,
        out_shape=jax.ShapeDtypeStruct((db, t_new, HQ_A * DH_A), BF16),
        compiler_params=_cparams(("arbitrary",), 56),
        name="moba_sample_attention",
    )(page_table.reshape(-1), q, k_new, v_new, far, blast, bnew, cache_k, cache_v)


def _mla_in_kernel(x_ref, g_ref, sh_ref, sc_ref, win_ref, qg_ref, wqb_ref, kvg_ref, wn_ref,
                   cq_ref, sq_ref, q_ref, kv_ref, ckv_ref, kpe_ref):
    nope_w = HEADS_B * NOPE_B
    rope_w = HEADS_B * ROPE_B
    lat = kv_ref.shape[-1] - ROPE_B
    qa_w = qg_ref.shape[-1]
    scale = (NOPE_B + ROPE_B) ** -0.5
    h = _ada_norm(x_ref[0], g_ref[...], sh_ref[0], sc_ref[0])
    y = _dot(h, win_ref[...])
    qn = _rms(y[:, :qa_w], qg_ref[...])
    ckv = _rms(y[:, qa_w:qa_w + lat], kvg_ref[...])
    cos = cq_ref[0]
    sin = sq_ref[0]
    k_lo = qa_w + lat
    kpe = y[:, k_lo:k_lo + ROPE_B] * cos[:, :ROPE_B] + y[:, k_lo + LANES:k_lo + LANES + ROPE_B] * sin[:, :ROPE_B]
    ckv_ref[0] = ckv
    kpe_ref[0] = kpe
    kv_ref[0, :, :lat] = ckv.astype(BF16)
    kv_ref[0, :, lat:] = kpe.astype(BF16)
    qq = _dot(qn, wqb_ref[...])
    q_pe = (qq[:, nope_w:nope_w + rope_w] * cos + qq[:, nope_w + rope_w:] * sin) * scale
    for hd in range(HEADS_B):
        q_lat = _dot(qq[:, hd * NOPE_B:(hd + 1) * NOPE_B], wn_ref[hd]) * scale
        q_ref[0, hd, :, :lat] = q_lat.astype(BF16)
        q_ref[0, hd, :, lat:] = q_pe[:, hd * ROPE_B:(hd + 1) * ROPE_B].astype(BF16)


def _rope_swap_cols(w, n_rope):
    half = n_rope // 2
    return jnp.concatenate([w[..., half:], w[..., :half]], axis=-1)


def mla_in(x, g, shift, scale, w_in, q_norm_g, w_qb, kv_norm_g, w_kvb, pos, ts):
    b, s, d = x.shape
    qa_w = q_norm_g.shape[0]
    lat = kv_norm_g.shape[0]
    kpe_w = w_in[:, qa_w + lat:]
    pad = jnp.zeros((d, LANES - ROPE_B), w_in.dtype)
    w_in_ext = jnp.concatenate([w_in[:, :qa_w + lat], kpe_w, pad, _rope_swap_cols(kpe_w, ROPE_B), pad], axis=1).astype(BF16)
    wq = w_qb.reshape(qa_w, HEADS_B, NOPE_B + ROPE_B)
    wq_rope = wq[:, :, NOPE_B:]
    w_qb_ext = jnp.concatenate([wq[:, :, :NOPE_B].reshape(qa_w, -1), wq_rope.reshape(qa_w, -1),
                                _rope_swap_cols(wq_rope, ROPE_B).reshape(qa_w, -1)], axis=1).astype(BF16)
    w_nope = w_kvb[:, :, :NOPE_B].transpose(1, 2, 0).astype(BF16)
    half = ROPE_B // 2
    inv = ROPE_THETA ** (-jnp.arange(half, dtype=F32) / half)
    ang = pos.astype(F32)[..., None] * inv
    cos, sin = jnp.cos(ang), jnp.sin(ang)
    cos_t = jnp.tile(jnp.concatenate([cos, cos], axis=-1), (1, 1, HEADS_B))
    sin_t = jnp.tile(jnp.concatenate([-sin, sin], axis=-1), (1, 1, HEADS_B))
    outs = [jax.ShapeDtypeStruct((b, HEADS_B, s, lat + ROPE_B), BF16),
            jax.ShapeDtypeStruct((b, s, lat + ROPE_B), BF16),
            jax.ShapeDtypeStruct((b, s, lat), F32),
            jax.ShapeDtypeStruct((b, s, ROPE_B), F32)]
    qg = q_norm_g.reshape(1, -1)
    kvg = kv_norm_g.reshape(1, -1)
    return pl.pallas_call(
        _mla_in_kernel,
        grid=(b, s // ts),
        in_specs=[_row_spec(x, ts), _full_spec(g), _mod_spec(shift, ts), _mod_spec(scale, ts),
                  _full_spec(w_in_ext), _full_spec(qg), _full_spec(w_qb_ext), _full_spec(kvg), _full_spec(w_nope),
                  _row_spec(cos_t, ts), _row_spec(sin_t, ts)],
        out_specs=[pl.BlockSpec((1, HEADS_B, ts, lat + ROPE_B), lambda bb, j: (bb, 0, j, 0)),
                   pl.BlockSpec((1, ts, lat + ROPE_B), lambda bb, j: (bb, j, 0)),
                   pl.BlockSpec((1, ts, lat), lambda bb, j: (bb, j, 0)),
                   pl.BlockSpec((1, ts, ROPE_B), lambda bb, j: (bb, j, 0))],
        out_shape=outs,
        compiler_params=_cparams(("arbitrary", "arbitrary"), 48),
        name="mla_in",
    )(x, g, shift, scale, w_in_ext, qg, w_qb_ext, kvg, w_nope, cos_t, sin_t)


def _mla_prompt_kernel(q_ref, kv_ref, o_ref, s_ref, mx_ref, l_ref, acc_ref, *, tq, hc):
    i = pl.program_id(1)
    lat = acc_ref.shape[-1]
    rows = hc * tq
    row_t = lax.broadcasted_iota(I32, (rows, tq), 0) % tq
    col_s = lax.broadcasted_iota(I32, (rows, tq), 1)
    causal = col_s <= row_t

    def kv_chunk(j):
        return kv_ref[0, pl.ds(pl.multiple_of(j * tq, tq), tq), :]

    for c in range(HEADS_B // hc):
        q = q_ref[0, c * hc:(c + 1) * hc].reshape(rows, q_ref.shape[-1])

        s = jnp.where(causal, lax.dot_general(q, kv_chunk(i), NT_DIMS, preferred_element_type=F32), NEG_INF)
        s_ref[i] = s
        mx_ref[...] = _halves_max(s)

        def body(j, cc):
            sj = lax.dot_general(q, kv_chunk(j), NT_DIMS, preferred_element_type=F32)
            s_ref[j] = sj
            mx_ref[...] = jnp.maximum(mx_ref[...], _halves_max(sj))
            return cc

        lax.fori_loop(0, i, body, 0)
        o = _two_pass_finish(s_ref, mx_ref, l_ref, acc_ref, i + 1, lambda j: kv_chunk(j)[:, :lat])
        for hl in range(hc):
            hd = c * hc + hl
            o_ref[0, :, hd * lat:(hd + 1) * lat] = o[hl * tq:(hl + 1) * tq].astype(o_ref.dtype)


def mla_prompt_attention(q, kv, hc=4):
    b, nh, s, w = q.shape
    lat = w - ROPE_B
    tq = 2 * LANES
    assert s % tq == 0
    return pl.pallas_call(
        functools.partial(_mla_prompt_kernel, tq=tq, hc=hc),
        grid=(b, s // tq),
        in_specs=[pl.BlockSpec((1, nh, tq, w), lambda bb, i: (bb, 0, i, 0)),
                  pl.BlockSpec((1, s, w), lambda bb, i: (bb, 0, 0))],
        out_specs=pl.BlockSpec((1, tq, nh * lat), lambda bb, i: (bb, i, 0)),
        out_shape=jax.ShapeDtypeStruct((b, s, nh * lat), BF16),
        scratch_shapes=[pltpu.VMEM((s // tq, hc * tq, tq), F32),
                        pltpu.VMEM((hc * tq, LANES), F32), pltpu.VMEM((hc * tq, LANES), F32),
                        pltpu.VMEM((hc * tq, lat), F32)],
        compiler_params=_cparams(("arbitrary", "arbitrary"), 48),
        name="mla_prompt_attention",
    )(q, kv)


def _mla_sample_kernel(pt_ref, q_ref, cn_ref, pn_ref, bnew_ref, cc_hbm, cp_hbm, o_ref,
                       cbuf, pbuf, csem, psem, *, n_pages, n_new, chunk_pages):
    b = pl.program_id(0)
    slot = b % 2
    lat = cbuf.shape[-1]
    page = cbuf.shape[2]

    def fetch(bb, sl):
        for c in _page_copies(cc_hbm, cbuf, csem, pt_ref, bb, sl, n_pages):
            c.start()
        for c in _page_copies(cp_hbm, pbuf, psem, pt_ref, bb, sl, n_pages):
            c.start()

    @pl.when(b == 0)
    def _():
        fetch(0, 0)

    @pl.when(b + 1 < pl.num_programs(0))
    def _():
        fetch(b + 1, 1 - slot)

    q = q_ref[0]
    ql, qp = q[:, :lat], q[:, lat:]
    qf, qpf = ql.astype(F32), qp.astype(F32)
    cn = cn_ref[0].astype(BF16).astype(F32)
    pn = pn_ref[0].astype(BF16).astype(F32)
    bnew = bnew_ref[...]
    cols = [jnp.sum(qf * cn[s:s + 1, :], axis=1, keepdims=True) + jnp.sum(qpf * pn[s:s + 1, :], axis=1, keepdims=True)
            + bnew[:, s:s + 1] for s in range(n_new)]
    _pages_wait(cc_hbm, cbuf, csem, slot, n_pages)
    _pages_wait(cp_hbm, pbuf, psem, slot, n_pages)

    def chunk(buf, c):
        blk = buf[slot, c * chunk_pages:(c + 1) * chunk_pages]
        return blk.reshape(chunk_pages * page, blk.shape[-1]).astype(BF16)

    def rope_scores(c):
        return jnp.concatenate([jnp.dot(qp, pbuf[slot, p].astype(BF16), preferred_element_type=F32)
                                for p in range(c * chunk_pages, (c + 1) * chunk_pages)], axis=1)

    n_chunks = n_pages // chunk_pages
    scores = [_dot_nt(ql, chunk(cbuf, c)) + rope_scores(c) for c in range(n_chunks)]
    m = functools.reduce(jnp.maximum, cols + [jnp.max(s, axis=1, keepdims=True) for s in scores])
    ps = [jnp.exp(c - m) for c in cols]
    l = functools.reduce(jnp.add, ps)
    acc = functools.reduce(jnp.add, [p.astype(BF16).astype(F32) * cn[s:s + 1, :] for s, p in enumerate(ps)])
    for c, s in enumerate(scores):
        p = jnp.exp(s - m)
        l = l + jnp.sum(p, axis=1, keepdims=True)
        acc = acc + jnp.dot(p.astype(BF16), chunk(cbuf, c), preferred_element_type=F32)
    o_ref[0] = (acc * (1.0 / l)).astype(o_ref.dtype)


def mla_sample_attention(q, ckv_new, kpe_new, cache_ckv, cache_kpe, page_table):
    db, rows, w = q.shape
    t_new = ckv_new.shape[1]
    lat = ckv_new.shape[2]
    n_pages = page_table.shape[1]
    page = cache_ckv.shape[1]
    tt = np.arange(t_new)[:, None]
    ts = np.arange(t_new)[None, :]
    bnew = jnp.asarray(np.tile(np.where(tt >= ts, 0.0, NEG_INF).astype(np.float32), (rows // t_new, 1)))
    grid_spec = pltpu.PrefetchScalarGridSpec(
        num_scalar_prefetch=1,
        grid=(db,),
        in_specs=[pl.BlockSpec((1, rows, w), lambda b, pt: (b, 0, 0)),
                  pl.BlockSpec((1, t_new, lat), lambda b, pt: (b, 0, 0)),
                  pl.BlockSpec((1, t_new, ROPE_B), lambda b, pt: (b, 0, 0)),
                  pl.BlockSpec((rows, t_new), lambda b, pt: (0, 0)),
                  pl.BlockSpec(memory_space=pl.ANY),
                  pl.BlockSpec(memory_space=pl.ANY)],
        out_specs=pl.BlockSpec((1, rows, lat), lambda b, pt: (b, 0, 0)),
        scratch_shapes=[pltpu.VMEM((2, n_pages, page, lat), F32),
                        pltpu.VMEM((2, n_pages, ROPE_B, page), F32),
                        pltpu.SemaphoreType.DMA((2,)),
                        pltpu.SemaphoreType.DMA((2,))])
    return pl.pallas_call(
        functools.partial(_mla_sample_kernel, n_pages=n_pages, n_new=t_new,
                          chunk_pages=n_pages // 4 if n_pages % 4 == 0 else n_pages),
        grid_spec=grid_spec,
        out_shape=jax.ShapeDtypeStruct((db, rows, lat), BF16),
        compiler_params=_cparams(("arbitrary",), 40),
        name="mla_sample_attention",
    )(page_table.reshape(-1), q, ckv_new, kpe_new, bnew, cache_ckv, cache_kpe)


def _gmlp_kernel(x_ref, g_ref, sh_ref, sc_ref, gate_ref, win_ref, lng_ref, lnb_ref, wmix_ref, bmix_ref, wout_ref,
                 o_ref, *v_out):
    dv = lng_ref.shape[-1]
    gw = dv // GROUPS_C
    ts = x_ref.shape[1]
    x = x_ref[0]
    h = _ada_norm(x, g_ref[...], sh_ref[0], sc_ref[0])
    y = _dot(h, win_ref[...])
    a = 0.5 * y * (1.0 + lax.erf(y * (1.0 / math.sqrt(2.0))))
    u, v = a[:, :dv], a[:, dv:]
    mu = jnp.mean(v, axis=-1, keepdims=True)
    vc = v - mu
    var = jnp.mean(vc * vc, axis=-1, keepdims=True)
    v = vc * lax.rsqrt(var + EPS) * lng_ref[...] + lnb_ref[...]
    if v_out:
        v_out[0][0] = v
    r = lax.broadcasted_iota(I32, (CHUNK_C, CHUNK_C), 0)
    c = lax.broadcasted_iota(I32, (CHUNK_C, CHUNK_C), 1)
    tril = r >= c
    wm = [jnp.where(tril, wmix_ref[gi], 0.0).astype(BF16) for gi in range(GROUPS_C)]
    vb = v.astype(BF16)
    outs = []
    for ci in range(ts // CHUNK_C):
        rows = slice(ci * CHUNK_C, (ci + 1) * CHUNK_C)
        mixed = jnp.concatenate(
            [jnp.dot(wm[gi], vb[rows, gi * gw:(gi + 1) * gw], preferred_element_type=F32) for gi in range(GROUPS_C)],
            axis=1) + bmix_ref[...]
        outs.append(_dot(u[rows] * mixed, wout_ref[...]))
    o_ref[0] = x + gate_ref[0] * jnp.concatenate(outs, axis=0)


def gmlp(x, g, shift, scale, gate, w_in, ln_g, ln_b, wmix, bmix, w_out, want_v, ts=256):
    b, s, d = x.shape
    dv = ln_g.shape[0]
    lng, lnb = ln_g.reshape(1, dv), ln_b.reshape(1, dv)
    outs = [jax.ShapeDtypeStruct(x.shape, F32)]
    out_specs = [_row_spec(x, ts)]
    if want_v:
        outs.append(jax.ShapeDtypeStruct((b, s, dv), F32))
        out_specs.append(pl.BlockSpec((1, ts, dv), lambda bb, j: (bb, j, 0)))
    res = pl.pallas_call(
        _gmlp_kernel,
        grid=(b, s // ts),
        in_specs=[_row_spec(x, ts), _full_spec(g), _mod_spec(shift, ts), _mod_spec(scale, ts), _mod_spec(gate, ts),
                  _full_spec(w_in), _full_spec(lng), _full_spec(lnb), _full_spec(wmix), _full_spec(bmix),
                  _full_spec(w_out)],
        out_specs=out_specs,
        out_shape=outs,
        compiler_params=_cparams(("arbitrary", "arbitrary"), 56),
        name="gmlp",
    )(x, g, shift, scale, gate, w_in, lng, lnb, wmix, bmix, w_out)
    return res if want_v else (res[0], None)


def _conv_kernel(x_ref, g_ref, sh_ref, sc_ref, gate_ref, win_ref, cw_ref, wout_ref, *rest, period):
    ts, d = x_ref.shape[1], x_ref.shape[2]
    if period:
        f1_ref, f2_ref, o_ref, zc_ref, zp_ref = rest
    else:
        o_ref, st_ref, zp_ref = rest
    x = x_ref[0]
    h = _ada_norm(x, g_ref[...], sh_ref[0], sc_ref[0])
    y = _dot(h, win_ref[...])
    gate_b, zc = y[:, :d], y[:, d:2 * d] * y[:, 2 * d:]
    if period:
        zp_ref[0:SUBLANES] = jnp.zeros((SUBLANES, d), F32)
    else:
        @pl.when(pl.program_id(1) == 0)
        def _():
            zp_ref[0:SUBLANES] = jnp.zeros((SUBLANES, d), F32)
    zp_ref[SUBLANES:] = zc
    r1 = zp_ref[SUBLANES - 1:SUBLANES - 1 + ts]
    r2 = zp_ref[SUBLANES - 2:SUBLANES - 2 + ts]
    if period:
        t = lax.broadcasted_iota(I32, (ts, 1), 0) % period
        r1 = jnp.where(t >= 1, r1, f1_ref[0])
        r2 = jnp.where(t >= 2, r2, f2_ref[0])
        zc_ref[0] = zc
    else:
        st_ref[0] = zc[ts - (CONV_W - 1):]
        zp_ref[0:SUBLANES] = zc[ts - SUBLANES:]
    cw = cw_ref[...]
    yc = cw[0:1] * r2 + cw[1:2] * r1 + cw[2:3] * zc
    o_ref[0] = x + gate_ref[0] * _dot(gate_b * yc, wout_ref[...])


def short_conv(x, g, shift, scale, gate, w_in, conv_w, w_out, fix=None, period=0, ts=512):
    b, s, d = x.shape
    assert CONV_W == 3
    ins = [x, g, shift, scale, gate, w_in, conv_w, w_out]
    specs = [_row_spec(x, ts), _full_spec(g), _mod_spec(shift, ts), _mod_spec(scale, ts), _mod_spec(gate, ts),
             _full_spec(w_in), _full_spec(conv_w), _full_spec(w_out)]
    if period:
        ins += list(fix)
        specs += [_row_spec(x, ts), _row_spec(x, ts)]
        outs = [jax.ShapeDtypeStruct(x.shape, F32), jax.ShapeDtypeStruct(x.shape, F32)]
        out_specs = [_row_spec(x, ts), _row_spec(x, ts)]
    else:
        outs = [jax.ShapeDtypeStruct(x.shape, F32), jax.ShapeDtypeStruct((b, CONV_W - 1, d), F32)]
        out_specs = [_row_spec(x, ts), pl.BlockSpec((1, CONV_W - 1, d), lambda bb, j: (bb, 0, 0))]
    return pl.pallas_call(
        functools.partial(_conv_kernel, period=period),
        grid=(b, s // ts),
        in_specs=specs,
        out_specs=out_specs,
        out_shape=outs,
        scratch_shapes=[pltpu.VMEM((SUBLANES + ts, d), F32)],
        compiler_params=_cparams(("arbitrary", "arbitrary"), 56),
        name="short_conv",
    )(*ins)


def _router_kernel(x_ref, g_ref, sh_ref, sc_ref, whi_ref, wlo_ref, bias_ref, ltri_ref, route_ref, cnt_ref, run_ref):
    first = (pl.program_id(0) == 0) & (pl.program_id(1) == 0)

    @pl.when(first)
    def _():
        run_ref[...] = jnp.zeros_like(run_ref)

    h = _ada_norm(x_ref[0], g_ref[...], sh_ref[0], sc_ref[0])
    h_hi = h.astype(BF16)
    h_lo = (h - h_hi.astype(F32)).astype(BF16)
    logits = (jnp.dot(h_hi, whi_ref[...], preferred_element_type=F32)
              + jnp.dot(h_lo, whi_ref[...], preferred_element_type=F32)
              + jnp.dot(h_hi, wlo_ref[...], preferred_element_type=F32)) + bias_ref[...]
    lane = lax.broadcasted_iota(I32, logits.shape, 1)
    low = -3e38
    is_g = lane < N_GROUPS_E
    lg = jnp.where(is_g, logits, low)
    mg = jnp.max(lg, axis=1, keepdims=True)
    g_top = jnp.min(jnp.where(lg == mg, lane, LANES), axis=1, keepdims=True)
    p_g = 1.0 / jnp.sum(jnp.where(is_g, jnp.exp(lg - mg), 0.0), axis=1, keepdims=True)
    e_lo = N_GROUPS_E + g_top * EXPERTS_PER_GROUP
    le = jnp.where((lane >= e_lo) & (lane < e_lo + EXPERTS_PER_GROUP), logits, low)
    m1 = jnp.max(le, axis=1, keepdims=True)
    i1 = jnp.min(jnp.where(le == m1, lane, LANES), axis=1, keepdims=True)
    le2 = jnp.where(lane == i1, low, le)
    m2 = jnp.max(le2, axis=1, keepdims=True)
    i2 = jnp.min(jnp.where(le2 == m2, lane, LANES), axis=1, keepdims=True)
    e2 = jnp.exp(m2 - m1)
    w1 = p_g / (1.0 + e2)
    w2 = p_g * e2 / (1.0 + e2)
    e1 = i1 - N_GROUPS_E
    e2i = i2 - N_GROUPS_E
    oh1 = (lane == e1).astype(F32)
    oh2 = (lane == e2i).astype(F32)
    cnt = oh1 + oh2
    base = jnp.dot(ltri_ref[...], cnt.astype(BF16), preferred_element_type=F32) + run_ref[...]
    r1 = jnp.sum(oh1 * base, axis=1, keepdims=True)
    r2 = jnp.sum(oh2 * base, axis=1, keepdims=True)
    run = run_ref[...] + jnp.sum(cnt, axis=0, keepdims=True)
    run_ref[...] = run
    cnt_ref[...] = run
    vals = (e1.astype(F32), e2i.astype(F32), w1, w2, r1, r2)
    route = jnp.zeros(logits.shape, F32)
    for n, val in enumerate(vals):
        route = jnp.where(lane == n, val, route)
    route_ref[...] = route


def moe_router(x, g, shift, scale, w_group, b_group, w_expert, b_expert, ts=512):
    b, s, d = x.shape
    n = b * s
    w = jnp.concatenate([w_group, w_expert], axis=1)
    w = jnp.pad(w, ((0, 0), (0, LANES - w.shape[1])))
    w_hi = w.astype(BF16)
    w_lo = (w - w_hi.astype(F32)).astype(BF16)
    bias = jnp.pad(jnp.concatenate([b_group, b_expert]), (0, LANES - N_GROUPS_E - N_EXPERTS)).reshape(1, LANES)
    ltri = jnp.asarray(np.tril(np.ones((ts, ts), np.float32), -1), BF16)
    nj = s // ts
    return pl.pallas_call(
        _router_kernel,
        grid=(b, nj),
        in_specs=[_row_spec(x, ts), _full_spec(g), _mod_spec(shift, ts), _mod_spec(scale, ts),
                  _full_spec(w_hi), _full_spec(w_lo), _full_spec(bias), _full_spec(ltri)],
        out_specs=[pl.BlockSpec((ts, LANES), lambda bb, j: (bb * nj + j, 0)),
                   pl.BlockSpec((1, LANES), lambda bb, j: (0, 0))],
        out_shape=[jax.ShapeDtypeStruct((n, LANES), F32), jax.ShapeDtypeStruct((1, LANES), F32)],
        scratch_shapes=[pltpu.VMEM((1, LANES), F32)],
        compiler_params=_cparams(("arbitrary", "arbitrary"), 40),
        name="moe_router",
    )(x, g, shift, scale, w_hi, w_lo, bias, ltri)


def _dispatch_kernel(seg_ref, pos_ref, x_ref, g_ref, sh_ref, sc_ref, xs_hbm, hbuf, zero_ref, sem, zsem, *, ts):
    i = pl.program_id(0) * pl.num_programs(1) + pl.program_id(1)
    n = pl.num_programs(0) * pl.num_programs(1)
    slot = i % 2
    n_seg = seg_ref.shape[0] // 2

    def row_tile(p):
        return xs_hbm.at[pl.ds(pl.multiple_of(p * SUBLANES, SUBLANES), SUBLANES), :]

    def slot_wait(sl):
        for _ in range(TOPK_E):
            pltpu.make_async_copy(hbuf.at[sl], xs_hbm.at[pl.ds(0, ts * SUBLANES), :], sem.at[sl]).wait()

    @pl.when(i == 0)
    def _():
        zero_ref[...] = jnp.zeros_like(zero_ref)
        for phase in ("start", "wait"):
            def seg_body(sg, c):
                def row_body(r, cc):
                    cp = pltpu.make_async_copy(zero_ref, row_tile(seg_ref[sg] + r), zsem.at[0])
                    cp.start() if phase == "start" else cp.wait()
                    return cc
                return lax.fori_loop(0, seg_ref[n_seg + sg], row_body, c)
            lax.fori_loop(0, n_seg, seg_body, 0)

    @pl.when(i >= 2)
    def _():
        slot_wait(slot)

    h = _ada_norm(x_ref[0], g_ref[...], sh_ref[0], sc_ref[0])
    for s in range(SUBLANES):
        hbuf[slot, pl.ds(s, ts, stride=SUBLANES), :] = h[:, s * LANES:(s + 1) * LANES]
    for r in range(ts):
        for k in range(TOPK_E):
            pltpu.make_async_copy(hbuf.at[slot, pl.ds(r * SUBLANES, SUBLANES), :],
                                  row_tile(pos_ref[0, 0, TOPK_E * r + k]), sem.at[slot]).start(priority=k % 2)

    @pl.when(i == n - 1)
    def _():
        slot_wait(slot)

        @pl.when(n >= 2)
        def _():
            slot_wait(1 - slot)


def moe_dispatch(x, g, shift, scale, pos, seg, n_rows, ts=256):
    b, s, d = x.shape
    assert d == SUBLANES * LANES
    ts = min(ts, s)
    nj = s // ts
    pos3 = pos.reshape(b * nj, 1, TOPK_E * ts)
    grid_spec = pltpu.PrefetchScalarGridSpec(
        num_scalar_prefetch=1,
        grid=(b, nj),
        in_specs=[pl.BlockSpec((1, 1, TOPK_E * ts), lambda bb, j, sg: (bb * nj + j, 0, 0), memory_space=pltpu.SMEM),
                  pl.BlockSpec((1, ts, d), lambda bb, j, sg: (bb, j, 0)),
                  pl.BlockSpec(g.shape, lambda bb, j, sg: (0, 0)),
                  (pl.BlockSpec((1, 1, d), lambda bb, j, sg: (bb, 0, 0)) if shift.shape[1] == 1
                   else pl.BlockSpec((1, ts, d), lambda bb, j, sg: (bb, j, 0))),
                  (pl.BlockSpec((1, 1, d), lambda bb, j, sg: (bb, 0, 0)) if scale.shape[1] == 1
                   else pl.BlockSpec((1, ts, d), lambda bb, j, sg: (bb, j, 0)))],
        out_specs=pl.BlockSpec(memory_space=pl.ANY),
        scratch_shapes=[pltpu.VMEM((2, ts * SUBLANES, LANES), F32), pltpu.VMEM((SUBLANES, LANES), F32),
                        pltpu.SemaphoreType.DMA((2,)), pltpu.SemaphoreType.DMA((1,))])
    return pl.pallas_call(
        functools.partial(_dispatch_kernel, ts=ts),
        grid_spec=grid_spec,
        out_shape=jax.ShapeDtypeStruct((n_rows * SUBLANES, LANES), F32),
        compiler_params=_cparams(("arbitrary", "arbitrary"), 40),
        name="moe_dispatch",
    )(seg, pos3, x, g, shift, scale)


def _row_copy(src_hbm, t, buf, slot, r, sem):
    return pltpu.make_async_copy(src_hbm.at[pl.ds(pl.multiple_of(t * SUBLANES, SUBLANES), SUBLANES), :],
                                 buf.at[slot, pl.ds(r * SUBLANES, SUBLANES), :], sem.at[slot])


def _rows_wait(src_hbm, buf, slot, sem):
    pltpu.make_async_copy(src_hbm.at[pl.ds(0, buf.shape[1]), :], buf.at[slot], sem.at[slot]).wait()


def _gathered_rows(buf, slot, n_rows, start, step):
    return jnp.concatenate([buf[slot, pl.ds(start * SUBLANES + s, n_rows, stride=step * SUBLANES), :]
                            for s in range(SUBLANES)], axis=1)


def _expert_kernel(be_ref, x_ref, wgu_ref, wd_ref, y_ref, *, rows):
    x = jnp.concatenate([x_ref[pl.ds(s, rows, stride=SUBLANES), :] for s in range(SUBLANES)], axis=1)
    ab = _dot(x, wgu_ref[0])
    f = ab.shape[1] // 2
    a, bb = ab[:, :f], ab[:, f:]
    y = _dot(a * jax.nn.sigmoid(a) * bb, wd_ref[0])
    for s in range(SUBLANES):
        y_ref[pl.ds(s, rows, stride=SUBLANES), :] = y[:, s * LANES:(s + 1) * LANES]


def moe_experts(x_rows, blk_e, w_gate_up, w_down, rows):
    n_blk = blk_e.shape[0]
    ne, d, f2 = w_gate_up.shape
    grid_spec = pltpu.PrefetchScalarGridSpec(
        num_scalar_prefetch=1,
        grid=(n_blk,),
        in_specs=[pl.BlockSpec((rows * SUBLANES, LANES), lambda i, be: (i, 0)),
                  pl.BlockSpec((1, d, f2), lambda i, be: (be[i], 0, 0)),
                  pl.BlockSpec((1, f2 // 2, d), lambda i, be: (be[i], 0, 0))],
        out_specs=pl.BlockSpec((rows * SUBLANES, LANES), lambda i, be: (i, 0)))
    return pl.pallas_call(
        functools.partial(_expert_kernel, rows=rows),
        grid_spec=grid_spec,
        out_shape=jax.ShapeDtypeStruct(x_rows.shape, F32),
        compiler_params=_cparams(("arbitrary",), 40),
        name="moe_experts",
    )(blk_e, x_rows, w_gate_up, w_down)


def _combine_kernel(cur_ref, nxt_ref, x_ref, gate_ref, route_ref, y_hbm, *rest, ts, final):
    if final:
        fg_ref, o_ref, ybuf, sem = rest
    else:
        o_ref, ybuf, sem = rest
    i = pl.program_id(0) * pl.num_programs(1) + pl.program_id(1)
    n = pl.num_programs(0) * pl.num_programs(1)
    slot = i % 2

    def fetch(src_ref, sl):
        for r in range(TOPK_E * ts):
            _row_copy(y_hbm, src_ref[0, 0, r], ybuf, sl, r, sem).start(priority=r % 2)

    @pl.when(i == 0)
    def _():
        fetch(cur_ref, 0)

    @pl.when(i + 1 < n)
    def _():
        fetch(nxt_ref, 1 - slot)

    _rows_wait(y_hbm, ybuf, slot, sem)
    route = route_ref[...]
    y = (route[:, 2:3] * _gathered_rows(ybuf, slot, ts, 0, TOPK_E)
         + route[:, 3:4] * _gathered_rows(ybuf, slot, ts, 1, TOPK_E))
    out = x_ref[0] + gate_ref[0] * y
    if final:
        out = _rms(out, fg_ref[...])
    o_ref[0] = out


def moe_combine(x, gate, route, y_rows, row_of_item, final_g=None, ts=128):
    b, s, d = x.shape
    nj = s // ts
    n_tiles = b * nj
    rows3 = row_of_item.reshape(n_tiles, 1, TOPK_E * ts)
    smem_blk = lambda f: pl.BlockSpec((1, 1, TOPK_E * ts), f, memory_space=pltpu.SMEM)
    ins = [rows3, rows3, x, gate, route, y_rows]
    specs = [smem_blk(lambda bb, j: (bb * nj + j, 0, 0)),
             smem_blk(lambda bb, j: (jnp.minimum(bb * nj + j + 1, n_tiles - 1), 0, 0)),
             _row_spec(x, ts), _mod_spec(gate, ts),
             pl.BlockSpec((ts, LANES), lambda bb, j: (bb * nj + j, 0)),
             pl.BlockSpec(memory_space=pl.ANY)]
    if final_g is not None:
        ins.append(final_g)
        specs.append(_full_spec(final_g))
    return pl.pallas_call(
        functools.partial(_combine_kernel, ts=ts, final=final_g is not None),
        grid=(b, nj),
        in_specs=specs,
        out_specs=_row_spec(x, ts),
        out_shape=jax.ShapeDtypeStruct(x.shape, F32),
        scratch_shapes=[pltpu.VMEM((2, TOPK_E * ts * SUBLANES, LANES), F32), pltpu.SemaphoreType.DMA((2,))],
        compiler_params=_cparams(("arbitrary", "arbitrary"), 40),
        name="moe_combine",
    )(*ins)


def _route_layout(route, counts, rows):
    n_items = route.shape[0] * TOPK_E
    eid = route[:, :TOPK_E].astype(I32)
    rank = route[:, 2 * TOPK_E:3 * TOPK_E].astype(I32)
    counts = counts[0, :N_EXPERTS].astype(I32)
    padded = (counts + rows - 1) // rows * rows
    pend = jnp.cumsum(padded)
    pstart = pend - padded
    experts = jnp.arange(N_EXPERTS, dtype=I32)
    pos = jnp.sum(jnp.where(eid[:, :, None] == experts, pstart, 0), axis=-1) + rank
    n_blk = -(-(n_items + min(N_EXPERTS, n_items) * (rows - 1)) // rows)
    blk_lo = jnp.arange(n_blk, dtype=I32) * rows
    blk_e = jnp.minimum(jnp.sum((pend[None, :] <= blk_lo[:, None]).astype(I32), axis=1), N_EXPERTS - 1)
    seg = jnp.concatenate([pstart + counts, pend[-1:], padded - counts, n_blk * rows - pend[-1:]])
    return pos.astype(I32), blk_e.astype(I32), seg.astype(I32), n_blk * rows


def hier_moe(x, g, shift, scale, gate, w_group, b_group, w_expert, b_expert, w_gate_up, w_down, rows,
             final_g=None, ts_router=512, ts_combine=128):
    route, counts = moe_router(x, g, shift, scale, w_group, b_group, w_expert, b_expert, ts=ts_router)
    pos, blk_e, seg, n_rows = _route_layout(route, counts, rows)
    x_rows = moe_dispatch(x, g, shift, scale, pos, seg, n_rows)
    y_rows = moe_experts(x_rows, blk_e, w_gate_up, w_down, rows)
    return moe_combine(x, gate, route, y_rows, pos, final_g=final_g, ts=ts_combine)


def kernel(x_prompt, x_sample, cache_moba_k, cache_moba_v, cache_mla_ckv, cache_mla_kpe, state_conv, page_table, c_prompt, c_sample, norm_mix_g, norm_ffn_g, w_ada, b_ada, rel_bias, moba_w_qkv, moba_w_o, mla_w_in, mla_q_norm_g, mla_w_qb, mla_kv_norm_g, mla_w_kvb, mla_w_o, chunk_w_in, chunk_ln_g, chunk_ln_b, chunk_w_s, chunk_b_s, chunk_w_out, conv_w_in, conv_w, conv_w_out, moe_w_group, moe_b_group, moe_w_expert, moe_b_expert, moe_w_gate_up, moe_w_down, final_norm_g):
    bp, sp, d = x_prompt.shape
    db, t_new, _ = x_sample.shape
    ns = db * t_new
    depth = w_ada.shape[0]
    n_pages = page_table.shape[1]
    page = cache_moba_k.shape[1]
    past = n_pages * page

    mods = ada_mods(jnp.concatenate([c_prompt, c_sample], axis=0), w_ada, b_ada)

    def mod_p(i, k):
        return mods[i, :bp, k * d:(k + 1) * d].reshape(bp, 1, d)

    def mod_s(i, k):
        return jnp.repeat(mods[i, bp:, k * d:(k + 1) * d], t_new, axis=0).reshape(1, ns, d)

    xp = x_prompt
    xs = x_sample.reshape(1, ns, d)
    bf = lambda w: w.astype(BF16)
    outs = {}

    for i in range(depth):
        kind = i % 4
        g_mix = norm_mix_g[i].reshape(1, d)
        g_ffn = norm_ffn_g[i].reshape(1, d)
        mp = [mod_p(i, k) for k in range(6)]
        ms = [mod_s(i, k) for k in range(6)]
        if kind == 0:
            nq, nkv = HQ_A * DH_A, HKV_A * DH_A
            offs = [(0, nq), (nq, nkv), (nq + nkv, nkv)]
            w_qkv, w_o = bf(moba_w_qkv), bf(moba_w_o)
            q, k, v = norm_mod_matmul(xp, g_mix, mp[0], mp[1], w_qkv, offs, [F32] * 3, ts=512)
            o = moba_prompt_attention(q, k, v, rel_bias)
            xp = out_proj(xp, mp[2], o, w_o)
            outs['moba_k_prompt'] = k.reshape(bp, sp, HKV_A, DH_A)
            outs['moba_v_prompt'] = v.reshape(bp, sp, HKV_A, DH_A)
            q, k, v = norm_mod_matmul(xs, g_mix, ms[0], ms[1], w_qkv, offs, [F32] * 3, ts=ns)
            tok_minor = lambda c: jnp.transpose(c, (0, 2, 3, 1)).reshape(-1, nkv, page)
            o = moba_sample_attention(q.reshape(db, t_new, nq), k.reshape(db, t_new, nkv), v.reshape(db, t_new, nkv),
                                      tok_minor(cache_moba_k), tok_minor(cache_moba_v), page_table, rel_bias)
            xs = out_proj(xs, ms[2], o.reshape(1, ns, nq), w_o, ts=ns)
            outs['moba_k_sample'] = k.reshape(db, t_new, HKV_A, DH_A)
            outs['moba_v_sample'] = v.reshape(db, t_new, HKV_A, DH_A)
        elif kind == 1:
            w_v = bf(mla_w_kvb[:, :, NOPE_B:].transpose(1, 0, 2))
            w_o = bf(mla_w_o)
            proj = (mla_w_in, mla_q_norm_g, mla_w_qb, mla_kv_norm_g, mla_w_kvb)
            pos_p = jnp.broadcast_to(jnp.arange(sp, dtype=I32)[None], (bp, sp))
            qh, kv, ckv, kpe = mla_in(xp, g_mix, mp[0], mp[1], *proj, pos_p, ts=256)
            o_lat = mla_prompt_attention(qh, kv)
            xp = out_proj(xp, mp[2], o_lat, w_o, w_v=w_v, ts=256)
            outs['mla_ckv_prompt'], outs['mla_kpe_prompt'] = ckv, kpe
            pos_s = jnp.tile(past + jnp.arange(t_new, dtype=I32), db)[None]
            qh, _, ckv, kpe = mla_in(xs, g_mix, ms[0], ms[1], *proj, pos_s, ts=ns)
            lat = ckv.shape[-1]
            qs = qh.reshape(HEADS_B, db, t_new, lat + ROPE_B).transpose(1, 0, 2, 3).reshape(db, HEADS_B * t_new, -1)
            o_lat = mla_sample_attention(qs, ckv.reshape(db, t_new, lat), kpe.reshape(db, t_new, ROPE_B),
                                         cache_mla_ckv, jnp.transpose(cache_mla_kpe, (0, 2, 1)), page_table)
            o_lat = o_lat.reshape(db, HEADS_B, t_new, lat).transpose(0, 2, 1, 3).reshape(1, ns, HEADS_B * lat)
            xs = out_proj(xs, ms[2], o_lat, w_o, w_v=w_v, ts=ns)
            outs['mla_ckv_sample'] = ckv.reshape(db, t_new, lat)
            outs['mla_kpe_sample'] = kpe.reshape(db, t_new, ROPE_B)
        elif kind == 2:
            w_in, w_out = bf(chunk_w_in), bf(chunk_w_out)
            gw = chunk_ln_g.shape[0] // GROUPS_C
            bmix_p = jnp.repeat(chunk_b_s.T, gw, axis=1)
            xp, _ = gmlp(xp, g_mix, mp[0], mp[1], mp[2], w_in, chunk_ln_g, chunk_ln_b, chunk_w_s, bmix_p, w_out,
                         want_v=False)
            reps = CHUNK_C // t_new
            eye = jnp.eye(reps, dtype=F32)
            wmix_s = jnp.einsum('ab,gij->gaibj', eye, chunk_w_s[:, :t_new, :t_new]).reshape(GROUPS_C, CHUNK_C, CHUNK_C)
            bmix_s = jnp.repeat(jnp.tile(chunk_b_s[:, :t_new].T, (reps, 1)), gw, axis=1)
            xs, v_s = gmlp(xs, g_mix, ms[0], ms[1], ms[2], w_in, chunk_ln_g, chunk_ln_b, wmix_s, bmix_s, w_out,
                           want_v=True)
            outs['chunk_v_sample'] = v_s.reshape(db, t_new, -1)
        else:
            w_in, w_out = bf(conv_w_in), bf(conv_w_out)
            xp, st = short_conv(xp, g_mix, mp[0], mp[1], mp[2], w_in, conv_w, w_out)
            outs['conv_prompt'] = st
            f1 = jnp.repeat(state_conv[:, 1], t_new, axis=0).reshape(1, ns, d)
            f2 = jnp.tile(state_conv, (1, t_new // (CONV_W - 1), 1)).reshape(1, ns, d)
            xs, zc = short_conv(xs, g_mix, ms[0], ms[1], ms[2], w_in, conv_w, w_out, fix=(f1, f2), period=t_new,
                                ts=ns)
            outs['conv_sample'] = zc.reshape(db, t_new, d)[:, t_new - (CONV_W - 1):]
        fg = final_norm_g.reshape(1, d) if i == depth - 1 else None
        moe_w = (moe_w_group[i], moe_b_group[i], moe_w_expert[i], moe_b_expert[i],
                 bf(moe_w_gate_up[i]), bf(moe_w_down[i]))
        xp = hier_moe(xp, g_ffn, mp[3], mp[4], mp[5], *moe_w, rows=256, final_g=fg)
        xs = hier_moe(xs, g_ffn, ms[3], ms[4], ms[5], *moe_w, rows=128, final_g=fg, ts_router=ns)

    return (xp, xs.reshape(db, t_new, d),
            outs['moba_k_prompt'], outs['moba_v_prompt'], outs['moba_k_sample'], outs['moba_v_sample'],
            outs['mla_ckv_prompt'], outs['mla_kpe_prompt'], outs['mla_ckv_sample'], outs['mla_kpe_sample'],
            outs['chunk_v_sample'], outs['conv_prompt'], outs['conv_sample'])
```

```python
import functools
import math

import numpy as np
import jax
import jax.numpy as jnp
from jax import lax
from jax.experimental import pallas as pl
from jax.experimental.pallas import tpu as pltpu

F32, BF16, I32 = jnp.float32, jnp.bfloat16, jnp.int32
EPS = 1e-6
NEG_INF = -1e30
LANES = 128
SUBLANES = 8

HQ_A, HKV_A, GRP_A, DH_A = 16, 4, 4, 64
BLOCK_A, TOPK_A = 256, 3
NUM_BUCKETS, MAX_DISTANCE = 32, 128
HEADS_B, NOPE_B, ROPE_B, V_B = 16, 64, 32, 64
ROPE_THETA = 10000.0
CHUNK_C, GROUPS_C = 128, 8
CONV_W = 3
N_GROUPS_E, EXPERTS_PER_GROUP, TOPK_E = 4, 8, 2
N_EXPERTS = N_GROUPS_E * EXPERTS_PER_GROUP

NT_DIMS = (((1,), (1,)), ((), ()))
HIGHEST = lax.Precision.HIGHEST


def _cparams(sem, vmem_mb):
    return pltpu.CompilerParams(dimension_semantics=sem, vmem_limit_bytes=vmem_mb << 20)


def _dot(a, b):
    return jnp.dot(a.astype(BF16), b.astype(BF16), preferred_element_type=F32)


def _dot_nt(a, b):
    return lax.dot_general(a.astype(BF16), b.astype(BF16), NT_DIMS, preferred_element_type=F32)


def _rms(x, g):
    return x * lax.rsqrt(jnp.mean(x * x, axis=-1, keepdims=True) + EPS) * g


def _ada_norm(x, g, shift, scale):
    return _rms(x, g) * (1.0 + scale) + shift


def _mod_spec(m, ts):
    d = m.shape[-1]
    if m.shape[1] == 1:
        return pl.BlockSpec((1, 1, d), lambda b, j: (b, 0, 0))
    return pl.BlockSpec((1, ts, d), lambda b, j: (b, j, 0))


def _full_spec(a):
    nd = a.ndim
    return pl.BlockSpec(a.shape, lambda *_: (0,) * nd)


def _row_spec(a, ts):
    return pl.BlockSpec((1, ts) + a.shape[2:], lambda b, j: (b, j) + (0,) * (a.ndim - 2))


def _ada_kernel(c_ref, w_ref, b_ref, o_ref):
    c = c_ref[...]
    o_ref[0] = _dot(c * jax.nn.sigmoid(c), w_ref[0]) + b_ref[0]


def ada_mods(c_all, w_ada, b_ada):
    depth, d, n = w_ada.shape
    rows = c_all.shape[0]
    tn = 1536
    return pl.pallas_call(
        _ada_kernel,
        grid=(depth, n // tn),
        in_specs=[pl.BlockSpec((rows, d), lambda i, j: (0, 0)),
                  pl.BlockSpec((1, d, tn), lambda i, j: (i, 0, j)),
                  pl.BlockSpec((1, 1, tn), lambda i, j: (i, 0, j))],
        out_specs=pl.BlockSpec((1, rows, tn), lambda i, j: (i, 0, j)),
        out_shape=jax.ShapeDtypeStruct((depth, rows, n), F32),
        compiler_params=_cparams(("arbitrary", "arbitrary"), 40),
        name="ada_mods",
    )(c_all, w_ada, b_ada.reshape(depth, 1, n))


def _nmm_kernel(x_ref, g_ref, sh_ref, sc_ref, w_ref, *out_refs, offs):
    h = _ada_norm(x_ref[0], g_ref[...], sh_ref[0], sc_ref[0])
    y = _dot(h, w_ref[...])
    for o_ref, (lo, n) in zip(out_refs, offs):
        o_ref[0] = y[:, lo:lo + n].astype(o_ref.dtype)


def norm_mod_matmul(x, g, shift, scale, w, offs, dtypes, ts):
    b, s, d = x.shape
    outs = [jax.ShapeDtypeStruct((b, s, n), dt) for (_, n), dt in zip(offs, dtypes)]
    return pl.pallas_call(
        functools.partial(_nmm_kernel, offs=tuple(offs)),
        grid=(b, s // ts),
        in_specs=[_row_spec(x, ts), _full_spec(g), _mod_spec(shift, ts), _mod_spec(scale, ts), _full_spec(w)],
        out_specs=[pl.BlockSpec((1, ts, n), lambda bb, j: (bb, j, 0)) for _, n in offs],
        out_shape=outs,
        compiler_params=_cparams(("arbitrary", "arbitrary"), 48),
        name="norm_mod_matmul",
    )(x, g, shift, scale, w)


def _out_proj_kernel(x_ref, gate_ref, a_ref, *rest, n_heads):
    if n_heads:
        wv_ref, wo_ref, o_ref = rest
        a = a_ref[0]
        c = a.shape[1] // n_heads
        a = jnp.concatenate([_dot(a[:, h * c:(h + 1) * c], wv_ref[h]) for h in range(n_heads)], axis=1)
    else:
        wo_ref, o_ref = rest
        a = a_ref[0]
    o_ref[0] = x_ref[0] + gate_ref[0] * _dot(a, wo_ref[...])


def out_proj(x, gate, a, w_o, w_v=None, ts=512):
    b, s, d = x.shape
    ins = [x, gate, a] + ([w_v] if w_v is not None else []) + [w_o]
    specs = [_row_spec(x, ts), _mod_spec(gate, ts), _row_spec(a, ts)]
    specs += ([_full_spec(w_v)] if w_v is not None else []) + [_full_spec(w_o)]
    return pl.pallas_call(
        functools.partial(_out_proj_kernel, n_heads=0 if w_v is None else w_v.shape[0]),
        grid=(b, s // ts),
        in_specs=specs,
        out_specs=_row_spec(x, ts),
        out_shape=jax.ShapeDtypeStruct(x.shape, F32),
        compiler_params=_cparams(("arbitrary", "arbitrary"), 48),
        name="out_proj",
    )(*ins)


def _np_bucket(dist):
    n = np.maximum(dist, 0)
    max_exact = NUM_BUCKETS // 2
    ratio = np.log(np.maximum(n, 1).astype(np.float32) / max_exact) / math.log(MAX_DISTANCE / max_exact)
    large = np.minimum(max_exact + (ratio * (NUM_BUCKETS - max_exact)).astype(np.int32), NUM_BUCKETS - 1)
    return np.where(n < max_exact, n, large).astype(np.int32)


assert int(_np_bucket(np.array(BLOCK_A + 1))) == NUM_BUCKETS - 1


def _bias_table(rel_bias, dist):
    idx = _np_bucket(dist)
    onehot = (jnp.asarray(idx.reshape(-1, 1)) == jnp.arange(NUM_BUCKETS, dtype=I32)[None, :]).astype(F32)
    tab = jnp.einsum('bh,nb->hn', rel_bias, onehot, precision=HIGHEST)
    return tab.reshape((rel_bias.shape[1],) + idx.shape)


def _halves_max(s):
    return jnp.maximum(s[:, :LANES], s[:, LANES:])


def _two_pass_finish(s_ref, mx_ref, l_ref, acc_ref, n_chunks, value_chunk):
    m = jnp.max(mx_ref[...], axis=1, keepdims=True)
    mx_ref[...] = jnp.broadcast_to(m, mx_ref.shape)
    l_ref[...] = jnp.zeros_like(l_ref)
    acc_ref[...] = jnp.zeros_like(acc_ref)

    def body(j, c):
        mb = mx_ref[...]
        p = jnp.exp(s_ref[j] - jnp.concatenate([mb, mb], axis=1))
        l_ref[...] += p[:, :LANES] + p[:, LANES:]
        acc_ref[...] += jnp.dot(p.astype(BF16), value_chunk(j), preferred_element_type=F32)
        return c

    lax.fori_loop(0, n_chunks, body, 0)
    return acc_ref[...] * (1.0 / jnp.sum(l_ref[...], axis=1, keepdims=True))


def _moba_prompt_kernel(far_ref, q_ref, k_ref, v_ref, bown_ref, badj_ref, o_ref,
                        kmbd_ref, mb_ref, s_ref, mx_ref, l_ref, acc_ref, *, nb):
    i = pl.program_id(1)
    bs = BLOCK_A
    scale = DH_A ** -0.5

    @pl.when(i == 0)
    def _():
        kmbd_ref[...] = jnp.zeros_like(kmbd_ref)
        for j in range(nb):
            km = jnp.mean(k_ref[0, j * bs:(j + 1) * bs, :], axis=0, keepdims=True)
            for hq in range(HQ_A):
                kvh = hq // GRP_A
                kmbd_ref[j * HQ_A + hq:j * HQ_A + hq + 1, hq * DH_A:(hq + 1) * DH_A] = km[:, kvh * DH_A:(kvh + 1) * DH_A]

    q = q_ref[0]
    gates = lax.dot_general(q, kmbd_ref[...], NT_DIMS, precision=HIGHEST, preferred_element_type=F32)
    lane = lax.broadcasted_iota(I32, gates.shape, 1)
    jidx = lane // HQ_A
    past = jidx < i
    gm = jnp.where(past, gates, NEG_INF)
    rank = jnp.zeros(gates.shape, I32)
    for k in range(1, nb):
        other = pltpu.roll(gm, HQ_A * k, axis=1)
        jo = jnp.where(jidx >= k, jidx - k, jidx - k + nb)
        beats = (other > gm) | ((other == gm) & (jo < jidx))
        rank = rank + beats.astype(I32)
    mb = jnp.where(past & (rank < TOPK_A), 0.0, NEG_INF).astype(F32)
    for j in range(nb):
        mb_ref[j] = mb[:, j * HQ_A:(j + 1) * HQ_A]

    def kv_block(j, h):
        lo = pl.multiple_of(j * bs, bs)
        return (k_ref[0, pl.ds(lo, bs), h * DH_A:(h + 1) * DH_A].astype(BF16),
                v_ref[0, pl.ds(lo, bs), h * DH_A:(h + 1) * DH_A].astype(BF16))

    for h in range(HKV_A):
        heads = [GRP_A * h + g for g in range(GRP_A)]
        q4 = (jnp.concatenate([q[:, hq * DH_A:(hq + 1) * DH_A] for hq in heads], axis=0) * scale).astype(BF16)

        def mask_col(j, extra):
            mj = mb_ref[j]
            return jnp.concatenate([mj[:, hq:hq + 1] + (extra[n] if extra else 0.0)
                                    for n, hq in enumerate(heads)], axis=0)

        s = _dot_nt(q4, kv_block(i, h)[0]) + bown_ref[GRP_A * h:GRP_A * (h + 1)].reshape(GRP_A * bs, bs)
        s_ref[i] = s
        mx_ref[...] = _halves_max(s)

        def scores(j, bias):
            sj = _dot_nt(q4, kv_block(j, h)[0]) + bias
            s_ref[j] = sj
            mx_ref[...] = jnp.maximum(mx_ref[...], _halves_max(sj))

        @pl.when(i >= 1)
        def _():
            badj = badj_ref[GRP_A * h:GRP_A * (h + 1)].reshape(GRP_A * bs, bs)
            scores(i - 1, badj + mask_col(i - 1, None))

        far = [far_ref[hq] for hq in heads]

        def far_body(j, c):
            scores(j, mask_col(j, far))
            return c

        lax.fori_loop(0, jnp.maximum(i - 1, 0), far_body, 0)

        o = _two_pass_finish(s_ref, mx_ref, l_ref, acc_ref, i + 1, lambda j: kv_block(j, h)[1])
        for g, hq in enumerate(heads):
            o_ref[0, :, hq * DH_A:(hq + 1) * DH_A] = o[g * bs:(g + 1) * bs].astype(o_ref.dtype)


def moba_prompt_attention(q, k, v, rel_bias):
    b, s, _ = q.shape
    nb = s // BLOCK_A
    assert s % BLOCK_A == 0 and nb * HQ_A == LANES and nb >= TOPK_A
    t = np.arange(BLOCK_A)[:, None]
    sk = np.arange(BLOCK_A)[None, :]
    bown = jnp.where(jnp.asarray(t >= sk)[None], _bias_table(rel_bias, t - sk), NEG_INF)
    badj = _bias_table(rel_bias, BLOCK_A + t - sk)
    far = rel_bias[NUM_BUCKETS - 1]
    kv_spec = pl.BlockSpec((1, s, HKV_A * DH_A), lambda bb, i: (bb, 0, 0))
    return pl.pallas_call(
        functools.partial(_moba_prompt_kernel, nb=nb),
        grid=(b, nb),
        in_specs=[pl.BlockSpec(memory_space=pltpu.SMEM),
                  pl.BlockSpec((1, BLOCK_A, HQ_A * DH_A), lambda bb, i: (bb, i, 0)),
                  kv_spec, kv_spec, _full_spec(bown), _full_spec(badj)],
        out_specs=pl.BlockSpec((1, BLOCK_A, HQ_A * DH_A), lambda bb, i: (bb, i, 0)),
        out_shape=jax.ShapeDtypeStruct((b, s, HQ_A * DH_A), BF16),
        scratch_shapes=[pltpu.VMEM((nb * HQ_A, HQ_A * DH_A), F32),
                        pltpu.VMEM((nb, BLOCK_A, HQ_A), F32),
                        pltpu.VMEM((nb, GRP_A * BLOCK_A, BLOCK_A), F32),
                        pltpu.VMEM((GRP_A * BLOCK_A, LANES), F32),
                        pltpu.VMEM((GRP_A * BLOCK_A, LANES), F32),
                        pltpu.VMEM((GRP_A * BLOCK_A, DH_A), F32)],
        compiler_params=_cparams(("arbitrary", "arbitrary"), 48),
        name="moba_prompt_attention",
    )(far, q, k, v, bown, badj)


def _page_copies(cache_hbm, buf, sem, pt_ref, bb, slot, n_pages):
    return [pltpu.make_async_copy(cache_hbm.at[pt_ref[bb * n_pages + p]], buf.at[slot, p], sem.at[slot])
            for p in range(n_pages)]


def _pages_wait(cache_hbm, buf, sem, slot, n_pages):
    pltpu.make_async_copy(cache_hbm.at[pl.ds(0, n_pages)], buf.at[slot], sem.at[slot]).wait()


def _new_token_partials(qs, kn, vn, bnew, n_new):
    qf = qs.astype(BF16).astype(F32)
    kf = kn.astype(BF16).astype(F32)
    vf = vn.astype(BF16).astype(F32)
    cols = [jnp.sum(qf * kf[s:s + 1, :], axis=1, keepdims=True) + bnew[:, s:s + 1] for s in range(n_new)]
    m = functools.reduce(jnp.maximum, cols)
    ps = [jnp.exp(c - m) for c in cols]
    l = functools.reduce(jnp.add, ps)
    acc = functools.reduce(jnp.add, [p.astype(BF16).astype(F32) * vf[s:s + 1, :] for s, p in enumerate(ps)])
    return m, l, acc


def _moba_sample_kernel(pt_ref, q_ref, kn_ref, vn_ref, far_ref, blast_ref, bnew_ref, ck_hbm, cv_hbm, o_ref,
                        kbuf, vbuf, ksem, vsem, qbd_ref, *, n_pages, n_new):
    b = pl.program_id(0)
    slot = b % 2
    page = kbuf.shape[3]
    nb = n_pages * page // BLOCK_A
    ppb = n_pages // nb
    scale = DH_A ** -0.5

    def fetch(bb, sl):
        for c in _page_copies(ck_hbm, kbuf, ksem, pt_ref, bb, sl, n_pages):
            c.start()
        for c in _page_copies(cv_hbm, vbuf, vsem, pt_ref, bb, sl, n_pages):
            c.start()

    @pl.when(b == 0)
    def _():
        fetch(0, 0)

    @pl.when(b + 1 < pl.num_programs(0))
    def _():
        fetch(b + 1, 1 - slot)

    q = q_ref[0]
    qbd_ref[...] = jnp.zeros_like(qbd_ref)
    for hq in range(HQ_A):
        kvh = hq // GRP_A
        qbd_ref[hq * n_new:(hq + 1) * n_new, kvh * DH_A:(kvh + 1) * DH_A] = q[:, hq * DH_A:(hq + 1) * DH_A]
    qbd = qbd_ref[...]
    qs = qbd * scale

    carry = _new_token_partials(qs, kn_ref[0], vn_ref[0], bnew_ref[...], n_new)

    _pages_wait(ck_hbm, kbuf, ksem, slot, n_pages)
    rows = qbd.shape[0]
    qsb = qs.astype(BF16)
    raw = [jnp.dot(qsb, kbuf[slot, p].astype(BF16), preferred_element_type=F32) for p in range(n_pages)]

    lane = lax.broadcasted_iota(I32, (rows, LANES), 1)
    gcur = jnp.full((rows, LANES), NEG_INF, F32)
    for j in range(nb):
        blk = functools.reduce(jnp.add, raw[j * ppb:(j + 1) * ppb])
        gcur = jnp.where(lane == j, jnp.sum(blk, axis=1, keepdims=True) * (1.0 / BLOCK_A), gcur)
    sel = jnp.zeros(gcur.shape, jnp.bool_)
    for _ in range(TOPK_A):
        mx = jnp.max(gcur, axis=1, keepdims=True)
        idx = jnp.min(jnp.where(gcur == mx, lane, LANES), axis=1, keepdims=True)
        pick = lane == idx
        sel = sel | (pick & (mx > 0.5 * NEG_INF))
        gcur = jnp.where(pick, -3e38, gcur)
    selbias = jnp.where(sel, 0.0, NEG_INF).astype(F32)

    _pages_wait(cv_hbm, vbuf, vsem, slot, n_pages)
    far = far_ref[...]

    def page_bias(p):
        j = p // ppb
        col = selbias[:, j:j + 1]
        if j == nb - 1:
            lo = (p % ppb) * page
            return blast_ref[:, lo:lo + page] + col
        return jnp.broadcast_to(col + far, (rows, page))

    scores = [raw[p] + page_bias(p) for p in range(n_pages)]
    m0, l0, acc0 = carry
    m = jnp.maximum(m0, jnp.max(functools.reduce(jnp.maximum, scores), axis=1, keepdims=True))
    alpha = jnp.exp(m0 - m)
    lsum = jnp.zeros((rows, page), F32)
    acc = alpha * acc0
    for p in range(n_pages):
        pr = jnp.exp(scores[p] - m)
        lsum = lsum + pr
        acc = acc + _dot_nt(pr, vbuf[slot, p])
    l = alpha * l0 + jnp.sum(lsum, axis=1, keepdims=True)

    o = acc * (1.0 / l)
    for hq in range(HQ_A):
        kvh = hq // GRP_A
        o_ref[0, :, hq * DH_A:(hq + 1) * DH_A] = (
            o[hq * n_new:(hq + 1) * n_new, kvh * DH_A:(kvh + 1) * DH_A].astype(o_ref.dtype))


def moba_sample_attention(q, k_new, v_new, cache_k, cache_v, page_table, rel_bias):
    db, t_new, _ = q.shape
    n_pages = page_table.shape[1]
    page = cache_k.shape[2]
    past = n_pages * page
    assert past % BLOCK_A == 0 and BLOCK_A % page == 0 and t_new <= BLOCK_A
    nb = past // BLOCK_A
    assert TOPK_A <= nb <= LANES
    tt = np.arange(t_new)[:, None]
    blast = _bias_table(rel_bias, BLOCK_A + tt - np.arange(BLOCK_A)[None, :]).reshape(HQ_A * t_new, BLOCK_A)
    ts = np.arange(t_new)[None, :]
    bnew = jnp.where(jnp.asarray(tt >= ts)[None], _bias_table(rel_bias, tt - ts), NEG_INF)
    bnew = bnew.reshape(HQ_A * t_new, t_new)
    far = jnp.repeat(rel_bias[NUM_BUCKETS - 1], t_new).reshape(HQ_A * t_new, 1)
    rows = HQ_A * t_new
    kw = HKV_A * DH_A
    grid_spec = pltpu.PrefetchScalarGridSpec(
        num_scalar_prefetch=1,
        grid=(db,),
        in_specs=[pl.BlockSpec((1, t_new, HQ_A * DH_A), lambda b, pt: (b, 0, 0)),
                  pl.BlockSpec((1, t_new, kw), lambda b, pt: (b, 0, 0)),
                  pl.BlockSpec((1, t_new, kw), lambda b, pt: (b, 0, 0)),
                  pl.BlockSpec((rows, 1), lambda b, pt: (0, 0)),
                  pl.BlockSpec((rows, BLOCK_A), lambda b, pt: (0, 0)),
                  pl.BlockSpec((rows, t_new), lambda b, pt: (0, 0)),
                  pl.BlockSpec(memory_space=pl.ANY),
                  pl.BlockSpec(memory_space=pl.ANY)],
        out_specs=pl.BlockSpec((1, t_new, HQ_A * DH_A), lambda b, pt: (b, 0, 0)),
        scratch_shapes=[pltpu.VMEM((2, n_pages, kw, page), F32),
                        pltpu.VMEM((2, n_pages, kw, page), F32),
                        pltpu.SemaphoreType.DMA((2,)),
                        pltpu.SemaphoreType.DMA((2,)),
                        pltpu.VMEM((rows, kw), F32)])
    return pl.pallas_call(
        functools.partial(_moba_sample_kernel, n_pages=n_pages, n_new=t_new),
        grid_spec=grid_spec,
        out_shape=jax.ShapeDtypeStruct((db, t_new, HQ_A * DH_A), BF16),
        compiler_params=_cparams(("arbitrary",), 56),
        name="moba_sample_attention",
    )(page_table.reshape(-1), q, k_new, v_new, far, blast, bnew, cache_k, cache_v)


def _mla_in_kernel(x_ref, g_ref, sh_ref, sc_ref, win_ref, qg_ref, wqb_ref, kvg_ref, wn_ref,
                   cq_ref, sq_ref, q_ref, kv_ref, ckv_ref, kpe_ref):
    nope_w = HEADS_B * NOPE_B
    rope_w = HEADS_B * ROPE_B
    lat = kv_ref.shape[-1] - ROPE_B
    qa_w = qg_ref.shape[-1]
    scale = (NOPE_B + ROPE_B) ** -0.5
    h = _ada_norm(x_ref[0], g_ref[...], sh_ref[0], sc_ref[0])
    y = _dot(h, win_ref[...])
    qn = _rms(y[:, :qa_w], qg_ref[...])
    ckv = _rms(y[:, qa_w:qa_w + lat], kvg_ref[...])
    cos = cq_ref[0]
    sin = sq_ref[0]
    k_lo = qa_w + lat
    kpe = y[:, k_lo:k_lo + ROPE_B] * cos[:, :ROPE_B] + y[:, k_lo + LANES:k_lo + LANES + ROPE_B] * sin[:, :ROPE_B]
    ckv_ref[0] = ckv
    kpe_ref[0] = kpe
    kv_ref[0, :, :lat] = ckv.astype(BF16)
    kv_ref[0, :, lat:] = kpe.astype(BF16)
    qq = _dot(qn, wqb_ref[...])
    q_pe = (qq[:, nope_w:nope_w + rope_w] * cos + qq[:, nope_w + rope_w:] * sin) * scale
    for hd in range(HEADS_B):
        q_lat = _dot(qq[:, hd * NOPE_B:(hd + 1) * NOPE_B], wn_ref[hd]) * scale
        q_ref[0, hd, :, :lat] = q_lat.astype(BF16)
        q_ref[0, hd, :, lat:] = q_pe[:, hd * ROPE_B:(hd + 1) * ROPE_B].astype(BF16)


def _rope_swap_cols(w, n_rope):
    half = n_rope // 2
    return jnp.concatenate([w[..., half:], w[..., :half]], axis=-1)


def mla_in(x, g, shift, scale, w_in, q_norm_g, w_qb, kv_norm_g, w_kvb, pos, ts):
    b, s, d = x.shape
    qa_w = q_norm_g.shape[0]
    lat = kv_norm_g.shape[0]
    kpe_w = w_in[:, qa_w + lat:]
    pad = jnp.zeros((d, LANES - ROPE_B), w_in.dtype)
    w_in_ext = jnp.concatenate([w_in[:, :qa_w + lat], kpe_w, pad, _rope_swap_cols(kpe_w, ROPE_B), pad], axis=1).astype(BF16)
    wq = w_qb.reshape(qa_w, HEADS_B, NOPE_B + ROPE_B)
    wq_rope = wq[:, :, NOPE_B:]
    w_qb_ext = jnp.concatenate([wq[:, :, :NOPE_B].reshape(qa_w, -1), wq_rope.reshape(qa_w, -1),
                                _rope_swap_cols(wq_rope, ROPE_B).reshape(qa_w, -1)], axis=1).astype(BF16)
    w_nope = w_kvb[:, :, :NOPE_B].transpose(1, 2, 0).astype(BF16)
    half = ROPE_B // 2
    inv = ROPE_THETA ** (-jnp.arange(half, dtype=F32) / half)
    ang = pos.astype(F32)[..., None] * inv
    cos, sin = jnp.cos(ang), jnp.sin(ang)
    cos_t = jnp.tile(jnp.concatenate([cos, cos], axis=-1), (1, 1, HEADS_B))
    sin_t = jnp.tile(jnp.concatenate([-sin, sin], axis=-1), (1, 1, HEADS_B))
    outs = [jax.ShapeDtypeStruct((b, HEADS_B, s, lat + ROPE_B), BF16),
            jax.ShapeDtypeStruct((b, s, lat + ROPE_B), BF16),
            jax.ShapeDtypeStruct((b, s, lat), F32),
            jax.ShapeDtypeStruct((b, s, ROPE_B), F32)]
    qg = q_norm_g.reshape(1, -1)
    kvg = kv_norm_g.reshape(1, -1)
    return pl.pallas_call(
        _mla_in_kernel,
        grid=(b, s // ts),
        in_specs=[_row_spec(x, ts), _full_spec(g), _mod_spec(shift, ts), _mod_spec(scale, ts),
                  _full_spec(w_in_ext), _full_spec(qg), _full_spec(w_qb_ext), _full_spec(kvg), _full_spec(w_nope),
                  _row_spec(cos_t, ts), _row_spec(sin_t, ts)],
        out_specs=[pl.BlockSpec((1, HEADS_B, ts, lat + ROPE_B), lambda bb, j: (bb, 0, j, 0)),
                   pl.BlockSpec((1, ts, lat + ROPE_B), lambda bb, j: (bb, j, 0)),
                   pl.BlockSpec((1, ts, lat), lambda bb, j: (bb, j, 0)),
                   pl.BlockSpec((1, ts, ROPE_B), lambda bb, j: (bb, j, 0))],
        out_shape=outs,
        compiler_params=_cparams(("arbitrary", "arbitrary"), 48),
        name="mla_in",
    )(x, g, shift, scale, w_in_ext, qg, w_qb_ext, kvg, w_nope, cos_t, sin_t)


def _mla_prompt_kernel(q_ref, kv_ref, o_ref, s_ref, mx_ref, l_ref, acc_ref, *, tq, hc):
    i = pl.program_id(1)
    lat = acc_ref.shape[-1]
    rows = hc * tq
    row_t = lax.broadcasted_iota(I32, (rows, tq), 0) % tq
    col_s = lax.broadcasted_iota(I32, (rows, tq), 1)
    causal = col_s <= row_t

    def kv_chunk(j):
        return kv_ref[0, pl.ds(pl.multiple_of(j * tq, tq), tq), :]

    for c in range(HEADS_B // hc):
        q = q_ref[0, c * hc:(c + 1) * hc].reshape(rows, q_ref.shape[-1])

        s = jnp.where(causal, lax.dot_general(q, kv_chunk(i), NT_DIMS, preferred_element_type=F32), NEG_INF)
        s_ref[i] = s
        mx_ref[...] = _halves_max(s)

        def body(j, cc):
            sj = lax.dot_general(q, kv_chunk(j), NT_DIMS, preferred_element_type=F32)
            s_ref[j] = sj
            mx_ref[...] = jnp.maximum(mx_ref[...], _halves_max(sj))
            return cc

        lax.fori_loop(0, i, body, 0)
        o = _two_pass_finish(s_ref, mx_ref, l_ref, acc_ref, i + 1, lambda j: kv_chunk(j)[:, :lat])
        for hl in range(hc):
            hd = c * hc + hl
            o_ref[0, :, hd * lat:(hd + 1) * lat] = o[hl * tq:(hl + 1) * tq].astype(o_ref.dtype)


def mla_prompt_attention(q, kv, hc=4):
    b, nh, s, w = q.shape
    lat = w - ROPE_B
    tq = 2 * LANES
    assert s % tq == 0
    return pl.pallas_call(
        functools.partial(_mla_prompt_kernel, tq=tq, hc=hc),
        grid=(b, s // tq),
        in_specs=[pl.BlockSpec((1, nh, tq, w), lambda bb, i: (bb, 0, i, 0)),
                  pl.BlockSpec((1, s, w), lambda bb, i: (bb, 0, 0))],
        out_specs=pl.BlockSpec((1, tq, nh * lat), lambda bb, i: (bb, i, 0)),
        out_shape=jax.ShapeDtypeStruct((b, s, nh * lat), BF16),
        scratch_shapes=[pltpu.VMEM((s // tq, hc * tq, tq), F32),
                        pltpu.VMEM((hc * tq, LANES), F32), pltpu.VMEM((hc * tq, LANES), F32),
                        pltpu.VMEM((hc * tq, lat), F32)],
        compiler_params=_cparams(("arbitrary", "arbitrary"), 48),
        name="mla_prompt_attention",
    )(q, kv)


def _mla_sample_kernel(pt_ref, q_ref, cn_ref, pn_ref, bnew_ref, cc_hbm, cp_hbm, o_ref,
                       cbuf, pbuf, csem, psem, *, n_pages, n_new, chunk_pages):
    b = pl.program_id(0)
    slot = b % 2
    lat = cbuf.shape[-1]
    page = cbuf.shape[2]

    def fetch(bb, sl):
        for c in _page_copies(cc_hbm, cbuf, csem, pt_ref, bb, sl, n_pages):
            c.start()
        for c in _page_copies(cp_hbm, pbuf, psem, pt_ref, bb, sl, n_pages):
            c.start()

    @pl.when(b == 0)
    def _():
        fetch(0, 0)

    @pl.when(b + 1 < pl.num_programs(0))
    def _():
        fetch(b + 1, 1 - slot)

    q = q_ref[0]
    ql, qp = q[:, :lat], q[:, lat:]
    qf, qpf = ql.astype(F32), qp.astype(F32)
    cn = cn_ref[0].astype(BF16).astype(F32)
    pn = pn_ref[0].astype(BF16).astype(F32)
    bnew = bnew_ref[...]
    cols = [jnp.sum(qf * cn[s:s + 1, :], axis=1, keepdims=True) + jnp.sum(qpf * pn[s:s + 1, :], axis=1, keepdims=True)
            + bnew[:, s:s + 1] for s in range(n_new)]
    _pages_wait(cc_hbm, cbuf, csem, slot, n_pages)
    _pages_wait(cp_hbm, pbuf, psem, slot, n_pages)

    def chunk(buf, c):
        blk = buf[slot, c * chunk_pages:(c + 1) * chunk_pages]
        return blk.reshape(chunk_pages * page, blk.shape[-1]).astype(BF16)

    def rope_scores(c):
        return jnp.concatenate([jnp.dot(qp, pbuf[slot, p].astype(BF16), preferred_element_type=F32)
                                for p in range(c * chunk_pages, (c + 1) * chunk_pages)], axis=1)

    n_chunks = n_pages // chunk_pages
    scores = [_dot_nt(ql, chunk(cbuf, c)) + rope_scores(c) for c in range(n_chunks)]
    m = functools.reduce(jnp.maximum, cols + [jnp.max(s, axis=1, keepdims=True) for s in scores])
    ps = [jnp.exp(c - m) for c in cols]
    l = functools.reduce(jnp.add, ps)
    acc = functools.reduce(jnp.add, [p.astype(BF16).astype(F32) * cn[s:s + 1, :] for s, p in enumerate(ps)])
    for c, s in enumerate(scores):
        p = jnp.exp(s - m)
        l = l + jnp.sum(p, axis=1, keepdims=True)
        acc = acc + jnp.dot(p.astype(BF16), chunk(cbuf, c), preferred_element_type=F32)
    o_ref[0] = (acc * (1.0 / l)).astype(o_ref.dtype)


def mla_sample_attention(q, ckv_new, kpe_new, cache_ckv, cache_kpe, page_table):
    db, rows, w = q.shape
    t_new = ckv_new.shape[1]
    lat = ckv_new.shape[2]
    n_pages = page_table.shape[1]
    page = cache_ckv.shape[1]
    tt = np.arange(t_new)[:, None]
    ts = np.arange(t_new)[None, :]
    bnew = jnp.asarray(np.tile(np.where(tt >= ts, 0.0, NEG_INF).astype(np.float32), (rows // t_new, 1)))
    grid_spec = pltpu.PrefetchScalarGridSpec(
        num_scalar_prefetch=1,
        grid=(db,),
        in_specs=[pl.BlockSpec((1, rows, w), lambda b, pt: (b, 0, 0)),
                  pl.BlockSpec((1, t_new, lat), lambda b, pt: (b, 0, 0)),
                  pl.BlockSpec((1, t_new, ROPE_B), lambda b, pt: (b, 0, 0)),
                  pl.BlockSpec((rows, t_new), lambda b, pt: (0, 0)),
                  pl.BlockSpec(memory_space=pl.ANY),
                  pl.BlockSpec(memory_space=pl.ANY)],
        out_specs=pl.BlockSpec((1, rows, lat), lambda b, pt: (b, 0, 0)),
        scratch_shapes=[pltpu.VMEM((2, n_pages, page, lat), F32),
                        pltpu.VMEM((2, n_pages, ROPE_B, page), F32),
                        pltpu.SemaphoreType.DMA((2,)),
                        pltpu.SemaphoreType.DMA((2,))])
    return pl.pallas_call(
        functools.partial(_mla_sample_kernel, n_pages=n_pages, n_new=t_new,
                          chunk_pages=n_pages // 4 if n_pages % 4 == 0 else n_pages),
        grid_spec=grid_spec,
        out_shape=jax.ShapeDtypeStruct((db, rows, lat), BF16),
        compiler_params=_cparams(("arbitrary",), 40),
        name="mla_sample_attention",
    )(page_table.reshape(-1), q, ckv_new, kpe_new, bnew, cache_ckv, cache_kpe)


def _gmlp_kernel(x_ref, g_ref, sh_ref, sc_ref, gate_ref, win_ref, lng_ref, lnb_ref, wmix_ref, bmix_ref, wout_ref,
                 o_ref, *v_out):
    dv = lng_ref.shape[-1]
    gw = dv // GROUPS_C
    ts = x_ref.shape[1]
    x = x_ref[0]
    h = _ada_norm(x, g_ref[...], sh_ref[0], sc_ref[0])
    y = _dot(h, win_ref[...])
    a = 0.5 * y * (1.0 + lax.erf(y * (1.0 / math.sqrt(2.0))))
    u, v = a[:, :dv], a[:, dv:]
    mu = jnp.mean(v, axis=-1, keepdims=True)
    vc = v - mu
    var = jnp.mean(vc * vc, axis=-1, keepdims=True)
    v = vc * lax.rsqrt(var + EPS) * lng_ref[...] + lnb_ref[...]
    if v_out:
        v_out[0][0] = v
    r = lax.broadcasted_iota(I32, (CHUNK_C, CHUNK_C), 0)
    c = lax.broadcasted_iota(I32, (CHUNK_C, CHUNK_C), 1)
    tril = r >= c
    wm = [jnp.where(tril, wmix_ref[gi], 0.0).astype(BF16) for gi in range(GROUPS_C)]
    vb = v.astype(BF16)
    outs = []
    for ci in range(ts // CHUNK_C):
        rows = slice(ci * CHUNK_C, (ci + 1) * CHUNK_C)
        mixed = jnp.concatenate(
            [jnp.dot(wm[gi], vb[rows, gi * gw:(gi + 1) * gw], preferred_element_type=F32) for gi in range(GROUPS_C)],
            axis=1) + bmix_ref[...]
        outs.append(_dot(u[rows] * mixed, wout_ref[...]))
    o_ref[0] = x + gate_ref[0] * jnp.concatenate(outs, axis=0)


def gmlp(x, g, shift, scale, gate, w_in, ln_g, ln_b, wmix, bmix, w_out, want_v, ts=256):
    b, s, d = x.shape
    dv = ln_g.shape[0]
    lng, lnb = ln_g.reshape(1, dv), ln_b.reshape(1, dv)
    outs = [jax.ShapeDtypeStruct(x.shape, F32)]
    out_specs = [_row_spec(x, ts)]
    if want_v:
        outs.append(jax.ShapeDtypeStruct((b, s, dv), F32))
        out_specs.append(pl.BlockSpec((1, ts, dv), lambda bb, j: (bb, j, 0)))
    res = pl.pallas_call(
        _gmlp_kernel,
        grid=(b, s // ts),
        in_specs=[_row_spec(x, ts), _full_spec(g), _mod_spec(shift, ts), _mod_spec(scale, ts), _mod_spec(gate, ts),
                  _full_spec(w_in), _full_spec(lng), _full_spec(lnb), _full_spec(wmix), _full_spec(bmix),
                  _full_spec(w_out)],
        out_specs=out_specs,
        out_shape=outs,
        compiler_params=_cparams(("arbitrary", "arbitrary"), 56),
        name="gmlp",
    )(x, g, shift, scale, gate, w_in, lng, lnb, wmix, bmix, w_out)
    return res if want_v else (res[0], None)


def _conv_kernel(x_ref, g_ref, sh_ref, sc_ref, gate_ref, win_ref, cw_ref, wout_ref, *rest, period):
    ts, d = x_ref.shape[1], x_ref.shape[2]
    if period:
        f1_ref, f2_ref, o_ref, zc_ref, zp_ref = rest
    else:
        o_ref, st_ref, zp_ref = rest
    x = x_ref[0]
    h = _ada_norm(x, g_ref[...], sh_ref[0], sc_ref[0])
    y = _dot(h, win_ref[...])
    gate_b, zc = y[:, :d], y[:, d:2 * d] * y[:, 2 * d:]
    if period:
        zp_ref[0:SUBLANES] = jnp.zeros((SUBLANES, d), F32)
    else:
        @pl.when(pl.program_id(1) == 0)
        def _():
            zp_ref[0:SUBLANES] = jnp.zeros((SUBLANES, d), F32)
    zp_ref[SUBLANES:] = zc
    r1 = zp_ref[SUBLANES - 1:SUBLANES - 1 + ts]
    r2 = zp_ref[SUBLANES - 2:SUBLANES - 2 + ts]
    if period:
        t = lax.broadcasted_iota(I32, (ts, 1), 0) % period
        r1 = jnp.where(t >= 1, r1, f1_ref[0])
        r2 = jnp.where(t >= 2, r2, f2_ref[0])
        zc_ref[0] = zc
    else:
        st_ref[0] = zc[ts - (CONV_W - 1):]
        zp_ref[0:SUBLANES] = zc[ts - SUBLANES:]
    cw = cw_ref[...]
    yc = cw[0:1] * r2 + cw[1:2] * r1 + cw[2:3] * zc
    o_ref[0] = x + gate_ref[0] * _dot(gate_b * yc, wout_ref[...])


def short_conv(x, g, shift, scale, gate, w_in, conv_w, w_out, fix=None, period=0, ts=512):
    b, s, d = x.shape
    assert CONV_W == 3
    ins = [x, g, shift, scale, gate, w_in, conv_w, w_out]
    specs = [_row_spec(x, ts), _full_spec(g), _mod_spec(shift, ts), _mod_spec(scale, ts), _mod_spec(gate, ts),
             _full_spec(w_in), _full_spec(conv_w), _full_spec(w_out)]
    if period:
        ins += list(fix)
        specs += [_row_spec(x, ts), _row_spec(x, ts)]
        outs = [jax.ShapeDtypeStruct(x.shape, F32), jax.ShapeDtypeStruct(x.shape, F32)]
        out_specs = [_row_spec(x, ts), _row_spec(x, ts)]
    else:
        outs = [jax.ShapeDtypeStruct(x.shape, F32), jax.ShapeDtypeStruct((b, CONV_W - 1, d), F32)]
        out_specs = [_row_spec(x, ts), pl.BlockSpec((1, CONV_W - 1, d), lambda bb, j: (bb, 0, 0))]
    return pl.pallas_call(
        functools.partial(_conv_kernel, period=period),
        grid=(b, s // ts),
        in_specs=specs,
        out_specs=out_specs,
        out_shape=outs,
        scratch_shapes=[pltpu.VMEM((SUBLANES + ts, d), F32)],
        compiler_params=_cparams(("arbitrary", "arbitrary"), 56),
        name="short_conv",
    )(*ins)


def _router_kernel(x_ref, g_ref, sh_ref, sc_ref, whi_ref, wlo_ref, bias_ref, ltri_ref, route_ref, cnt_ref, run_ref):
    first = (pl.program_id(0) == 0) & (pl.program_id(1) == 0)

    @pl.when(first)
    def _():
        run_ref[...] = jnp.zeros_like(run_ref)

    h = _ada_norm(x_ref[0], g_ref[...], sh_ref[0], sc_ref[0])
    h_hi = h.astype(BF16)
    h_lo = (h - h_hi.astype(F32)).astype(BF16)
    logits = (jnp.dot(h_hi, whi_ref[...], preferred_element_type=F32)
              + jnp.dot(h_lo, whi_ref[...], preferred_element_type=F32)
              + jnp.dot(h_hi, wlo_ref[...], preferred_element_type=F32)) + bias_ref[...]
    lane = lax.broadcasted_iota(I32, logits.shape, 1)
    low = -3e38
    is_g = lane < N_GROUPS_E
    lg = jnp.where(is_g, logits, low)
    mg = jnp.max(lg, axis=1, keepdims=True)
    g_top = jnp.min(jnp.where(lg == mg, lane, LANES), axis=1, keepdims=True)
    p_g = 1.0 / jnp.sum(jnp.where(is_g, jnp.exp(lg - mg), 0.0), axis=1, keepdims=True)
    e_lo = N_GROUPS_E + g_top * EXPERTS_PER_GROUP
    le = jnp.where((lane >= e_lo) & (lane < e_lo + EXPERTS_PER_GROUP), logits, low)
    m1 = jnp.max(le, axis=1, keepdims=True)
    i1 = jnp.min(jnp.where(le == m1, lane, LANES), axis=1, keepdims=True)
    le2 = jnp.where(lane == i1, low, le)
    m2 = jnp.max(le2, axis=1, keepdims=True)
    i2 = jnp.min(jnp.where(le2 == m2, lane, LANES), axis=1, keepdims=True)
    e2 = jnp.exp(m2 - m1)
    w1 = p_g / (1.0 + e2)
    w2 = p_g * e2 / (1.0 + e2)
    e1 = i1 - N_GROUPS_E
    e2i = i2 - N_GROUPS_E
    oh1 = (lane == e1).astype(F32)
    oh2 = (lane == e2i).astype(F32)
    cnt = oh1 + oh2
    base = jnp.dot(ltri_ref[...], cnt.astype(BF16), preferred_element_type=F32) + run_ref[...]
    r1 = jnp.sum(oh1 * base, axis=1, keepdims=True)
    r2 = jnp.sum(oh2 * base, axis=1, keepdims=True)
    before = run_ref[...]
    run = before + jnp.sum(cnt, axis=0, keepdims=True)
    run_ref[...] = run
    sub = lax.broadcasted_iota(I32, cnt_ref.shape, 0)
    cnt_ref[...] = jnp.where(sub == 0, before, jnp.where(sub == 1, run, 0.0))
    vals = (e1.astype(F32), e2i.astype(F32), w1, w2, r1, r2)
    route = jnp.zeros(logits.shape, F32)
    for n, val in enumerate(vals):
        route = jnp.where(lane == n, val, route)
    route_ref[...] = route


def moe_router(x, g, shift, scale, w_group, b_group, w_expert, b_expert, ts=512):
    b, s, d = x.shape
    n = b * s
    w = jnp.concatenate([w_group, w_expert], axis=1)
    w = jnp.pad(w, ((0, 0), (0, LANES - w.shape[1])))
    w_hi = w.astype(BF16)
    w_lo = (w - w_hi.astype(F32)).astype(BF16)
    bias = jnp.pad(jnp.concatenate([b_group, b_expert]), (0, LANES - N_GROUPS_E - N_EXPERTS)).reshape(1, LANES)
    ltri = jnp.asarray(np.tril(np.ones((ts, ts), np.float32), -1), BF16)
    nj = s // ts
    return pl.pallas_call(
        _router_kernel,
        grid=(b, nj),
        in_specs=[_row_spec(x, ts), _full_spec(g), _mod_spec(shift, ts), _mod_spec(scale, ts),
                  _full_spec(w_hi), _full_spec(w_lo), _full_spec(bias), _full_spec(ltri)],
        out_specs=[pl.BlockSpec((ts, LANES), lambda bb, j: (bb * nj + j, 0)),
                   pl.BlockSpec((SUBLANES, LANES), lambda bb, j: (bb * nj + j, 0))],
        out_shape=[jax.ShapeDtypeStruct((n, LANES), F32), jax.ShapeDtypeStruct((b * nj * SUBLANES, LANES), F32)],
        scratch_shapes=[pltpu.VMEM((1, LANES), F32)],
        compiler_params=_cparams(("arbitrary", "arbitrary"), 40),
        name="moe_router",
    )(x, g, shift, scale, w_hi, w_lo, bias, ltri)


RUN_PIECE_LOG2 = 6


def _run_pieces(n, piece):
    big = 1 << RUN_PIECE_LOG2
    n_big = n >> RUN_PIECE_LOG2

    def body(it, c):
        piece(it * big, big)
        return c

    lax.fori_loop(0, n_big, body, 0)
    base = n_big * big
    rem = n - base
    for bit in range(RUN_PIECE_LOG2 - 1, -1, -1):
        @pl.when(((rem >> bit) & 1) == 1)
        def _():
            piece(base + ((rem >> (bit + 1)) << (bit + 1)), 1 << bit)


def _tile_rows(ref, lead, p, size):
    return ref.at[lead + (pl.ds(pl.multiple_of(p * SUBLANES, SUBLANES), size * SUBLANES), slice(None))]


def _tile_runs(meta_ref):
    return [(meta_ref[0, 0, e], meta_ref[0, 0, N_EXPERTS + e], meta_ref[0, 0, 2 * N_EXPERTS + e])
            for e in range(N_EXPERTS)]


def _dispatch_kernel(seg_ref, meta_ref, lpos_ref, x_ref, g_ref, sh_ref, sc_ref, xs_hbm,
                     hbuf, gbuf, zero_ref, sem, zsem, *, ts):
    i = pl.program_id(0) * pl.num_programs(1) + pl.program_id(1)
    n = pl.num_programs(0) * pl.num_programs(1)
    slot = i % 2
    n_seg = seg_ref.shape[0] // 2

    def row_tile(p):
        return _tile_rows(xs_hbm, (), p, 1)

    def slot_wait(sl):
        pltpu.make_async_copy(gbuf.at[sl], xs_hbm.at[pl.ds(0, gbuf.shape[1]), :], sem.at[sl]).wait()

    @pl.when(i == 0)
    def _():
        zero_ref[...] = jnp.zeros_like(zero_ref)
        for phase in ("start", "wait"):
            def seg_body(sg, c):
                def row_body(r, cc):
                    cp = pltpu.make_async_copy(zero_ref, row_tile(seg_ref[sg] + r), zsem.at[0])
                    cp.start() if phase == "start" else cp.wait()
                    return cc
                return lax.fori_loop(0, seg_ref[n_seg + sg], row_body, c)
            lax.fori_loop(0, n_seg, seg_body, 0)

    @pl.when(i >= 2)
    def _():
        slot_wait(slot)

    h = _ada_norm(x_ref[0], g_ref[...], sh_ref[0], sc_ref[0])
    for s in range(SUBLANES):
        hbuf[pl.ds(s, ts, stride=SUBLANES), :] = h[:, s * LANES:(s + 1) * LANES]
    for r in range(ts):
        row = hbuf[r * SUBLANES:(r + 1) * SUBLANES, :]
        for k in range(TOPK_E):
            lp = lpos_ref[0, 0, TOPK_E * r + k]
            gbuf[slot, pl.ds(pl.multiple_of(lp * SUBLANES, SUBLANES), SUBLANES), :] = row
    for first, local, length in _tile_runs(meta_ref):
        def send(off, size, first=first, local=local):
            pltpu.make_async_copy(_tile_rows(gbuf, (slot,), local + off, size),
                                  _tile_rows(xs_hbm, (), first + off, size), sem.at[slot]).start()
        _run_pieces(length, send)

    @pl.when(i == n - 1)
    def _():
        slot_wait(slot)

        @pl.when(n >= 2)
        def _():
            slot_wait(1 - slot)


def moe_dispatch(x, g, shift, scale, lpos, meta, seg, n_rows, ts):
    b, s, d = x.shape
    assert d == SUBLANES * LANES
    nj = s // ts
    items = TOPK_E * ts
    grid_spec = pltpu.PrefetchScalarGridSpec(
        num_scalar_prefetch=1,
        grid=(b, nj),
        in_specs=[pl.BlockSpec((1, 1, 3 * N_EXPERTS), lambda bb, j, sg: (bb * nj + j, 0, 0), memory_space=pltpu.SMEM),
                  pl.BlockSpec((1, 1, items), lambda bb, j, sg: (bb * nj + j, 0, 0), memory_space=pltpu.SMEM),
                  pl.BlockSpec((1, ts, d), lambda bb, j, sg: (bb, j, 0)),
                  pl.BlockSpec(g.shape, lambda bb, j, sg: (0, 0)),
                  (pl.BlockSpec((1, 1, d), lambda bb, j, sg: (bb, 0, 0)) if shift.shape[1] == 1
                   else pl.BlockSpec((1, ts, d), lambda bb, j, sg: (bb, j, 0))),
                  (pl.BlockSpec((1, 1, d), lambda bb, j, sg: (bb, 0, 0)) if scale.shape[1] == 1
                   else pl.BlockSpec((1, ts, d), lambda bb, j, sg: (bb, j, 0)))],
        out_specs=pl.BlockSpec(memory_space=pl.ANY),
        scratch_shapes=[pltpu.VMEM((ts * SUBLANES, LANES), F32), pltpu.VMEM((2, items * SUBLANES, LANES), F32),
                        pltpu.VMEM((SUBLANES, LANES), F32),
                        pltpu.SemaphoreType.DMA((2,)), pltpu.SemaphoreType.DMA((1,))])
    return pl.pallas_call(
        functools.partial(_dispatch_kernel, ts=ts),
        grid_spec=grid_spec,
        out_shape=jax.ShapeDtypeStruct((n_rows * SUBLANES, LANES), F32),
        compiler_params=_cparams(("arbitrary", "arbitrary"), 48),
        name="moe_dispatch",
    )(seg, meta.reshape(b * nj, 1, 3 * N_EXPERTS), lpos.reshape(b * nj, 1, items), x, g, shift, scale)


def _expert_kernel(be_ref, x_ref, wgu_ref, wd_ref, y_ref, *, rows):
    x = jnp.concatenate([x_ref[pl.ds(s, rows, stride=SUBLANES), :] for s in range(SUBLANES)], axis=1)
    ab = _dot(x, wgu_ref[0])
    f = ab.shape[1] // 2
    a, bb = ab[:, :f], ab[:, f:]
    y = _dot(a * jax.nn.sigmoid(a) * bb, wd_ref[0])
    for s in range(SUBLANES):
        y_ref[pl.ds(s, rows, stride=SUBLANES), :] = y[:, s * LANES:(s + 1) * LANES]


def moe_experts(x_rows, blk_e, w_gate_up, w_down, rows):
    n_blk = blk_e.shape[0]
    ne, d, f2 = w_gate_up.shape
    grid_spec = pltpu.PrefetchScalarGridSpec(
        num_scalar_prefetch=1,
        grid=(n_blk,),
        in_specs=[pl.BlockSpec((rows * SUBLANES, LANES), lambda i, be: (i, 0)),
                  pl.BlockSpec((1, d, f2), lambda i, be: (be[i], 0, 0)),
                  pl.BlockSpec((1, f2 // 2, d), lambda i, be: (be[i], 0, 0))],
        out_specs=pl.BlockSpec((rows * SUBLANES, LANES), lambda i, be: (i, 0)))
    return pl.pallas_call(
        functools.partial(_expert_kernel, rows=rows),
        grid_spec=grid_spec,
        out_shape=jax.ShapeDtypeStruct(x_rows.shape, F32),
        compiler_params=_cparams(("arbitrary",), 40),
        name="moe_experts",
    )(blk_e, x_rows, w_gate_up, w_down)


def _combine_kernel(cur_ref, nxt_ref, lpos_ref, x_ref, gate_ref, route_ref, y_hbm, *rest, ts, final):
    if final:
        fg_ref, o_ref, ybuf, tbuf, sem = rest
    else:
        o_ref, ybuf, tbuf, sem = rest
    i = pl.program_id(0) * pl.num_programs(1) + pl.program_id(1)
    n = pl.num_programs(0) * pl.num_programs(1)
    slot = i % 2

    def fetch(meta_ref, sl):
        for first, local, length in _tile_runs(meta_ref):
            def recv(off, size, first=first, local=local):
                pltpu.make_async_copy(_tile_rows(y_hbm, (), first + off, size),
                                      _tile_rows(ybuf, (sl,), local + off, size), sem.at[sl]).start()
            _run_pieces(length, recv)

    @pl.when(i == 0)
    def _():
        fetch(cur_ref, 0)

    @pl.when(i + 1 < n)
    def _():
        fetch(nxt_ref, 1 - slot)

    pltpu.make_async_copy(y_hbm.at[pl.ds(0, ybuf.shape[1]), :], ybuf.at[slot], sem.at[slot]).wait()
    for r in range(ts):
        for k in range(TOPK_E):
            lp = lpos_ref[0, 0, TOPK_E * r + k]
            tbuf[(k * ts + r) * SUBLANES:(k * ts + r + 1) * SUBLANES, :] = (
                ybuf[slot, pl.ds(pl.multiple_of(lp * SUBLANES, SUBLANES), SUBLANES), :])
    ys = [jnp.concatenate([tbuf[pl.ds(k * ts * SUBLANES + s, ts, stride=SUBLANES), :] for s in range(SUBLANES)],
                          axis=1) for k in range(TOPK_E)]
    route = route_ref[...]
    y = route[:, 2:3] * ys[0] + route[:, 3:4] * ys[1]
    out = x_ref[0] + gate_ref[0] * y
    if final:
        out = _rms(out, fg_ref[...])
    o_ref[0] = out


def moe_combine(x, gate, route, y_rows, lpos, meta, ts, final_g=None):
    b, s, d = x.shape
    nj = s // ts
    n_tiles = b * nj
    items = TOPK_E * ts
    meta3 = meta.reshape(n_tiles, 1, 3 * N_EXPERTS)
    meta_blk = lambda f: pl.BlockSpec((1, 1, 3 * N_EXPERTS), f, memory_space=pltpu.SMEM)
    ins = [meta3, meta3, lpos.reshape(n_tiles, 1, items), x, gate, route, y_rows]
    specs = [meta_blk(lambda bb, j: (bb * nj + j, 0, 0)),
             meta_blk(lambda bb, j: (jnp.minimum(bb * nj + j + 1, n_tiles - 1), 0, 0)),
             pl.BlockSpec((1, 1, items), lambda bb, j: (bb * nj + j, 0, 0), memory_space=pltpu.SMEM),
             _row_spec(x, ts), _mod_spec(gate, ts),
             pl.BlockSpec((ts, LANES), lambda bb, j: (bb * nj + j, 0)),
             pl.BlockSpec(memory_space=pl.ANY)]
    if final_g is not None:
        ins.append(final_g)
        specs.append(_full_spec(final_g))
    return pl.pallas_call(
        functools.partial(_combine_kernel, ts=ts, final=final_g is not None),
        grid=(b, nj),
        in_specs=specs,
        out_specs=_row_spec(x, ts),
        out_shape=jax.ShapeDtypeStruct(x.shape, F32),
        scratch_shapes=[pltpu.VMEM((2, items * SUBLANES, LANES), F32), pltpu.VMEM((items * SUBLANES, LANES), F32),
                        pltpu.SemaphoreType.DMA((2,))],
        compiler_params=_cparams(("arbitrary", "arbitrary"), 48),
        name="moe_combine",
    )(*ins)


def _route_layout(route, tiles, rows, ts):
    n_tok = route.shape[0]
    n_items = n_tok * TOPK_E
    n_tiles = n_tok // ts
    eid = route[:, :TOPK_E].astype(I32).reshape(n_tiles, ts, TOPK_E)
    rank = route[:, 2 * TOPK_E:3 * TOPK_E].astype(I32).reshape(n_tiles, ts, TOPK_E)
    t3 = tiles.reshape(n_tiles, SUBLANES, LANES)
    before = t3[:, 0, :N_EXPERTS].astype(I32)
    after = t3[:, 1, :N_EXPERTS].astype(I32)
    length = after - before
    counts = after[-1]
    padded = (counts + rows - 1) // rows * rows
    pend = jnp.cumsum(padded)
    pstart = pend - padded
    local = jnp.cumsum(length, axis=1) - length
    experts = jnp.arange(N_EXPERTS, dtype=I32)
    shift = (local - before)[:, None, None, :]
    lpos = jnp.sum(jnp.where(eid[..., None] == experts, shift, 0), axis=-1) + rank
    meta = jnp.concatenate([pstart[None, :] + before, local, length], axis=1)
    n_blk = -(-(n_items + min(N_EXPERTS, n_items) * (rows - 1)) // rows)
    blk_lo = jnp.arange(n_blk, dtype=I32) * rows
    blk_e = jnp.minimum(jnp.sum((pend[None, :] <= blk_lo[:, None]).astype(I32), axis=1), N_EXPERTS - 1)
    seg = jnp.concatenate([pstart + counts, pend[-1:], padded - counts, n_blk * rows - pend[-1:]])
    return lpos.astype(I32), meta.astype(I32), blk_e.astype(I32), seg.astype(I32), n_blk * rows


def hier_moe(x, g, shift, scale, gate, w_group, b_group, w_expert, b_expert, w_gate_up, w_down, rows,
             final_g=None, ts=512):
    route, tiles = moe_router(x, g, shift, scale, w_group, b_group, w_expert, b_expert, ts=ts)
    lpos, meta, blk_e, seg, n_rows = _route_layout(route, tiles, rows, ts)
    x_rows = moe_dispatch(x, g, shift, scale, lpos, meta, seg, n_rows, ts)
    y_rows = moe_experts(x_rows, blk_e, w_gate_up, w_down, rows)
    return moe_combine(x, gate, route, y_rows, lpos, meta, ts, final_g=final_g)


def kernel(x_prompt, x_sample, cache_moba_k, cache_moba_v, cache_mla_ckv, cache_mla_kpe, state_conv, page_table, c_prompt, c_sample, norm_mix_g, norm_ffn_g, w_ada, b_ada, rel_bias, moba_w_qkv, moba_w_o, mla_w_in, mla_q_norm_g, mla_w_qb, mla_kv_norm_g, mla_w_kvb, mla_w_o, chunk_w_in, chunk_ln_g, chunk_ln_b, chunk_w_s, chunk_b_s, chunk_w_out, conv_w_in, conv_w, conv_w_out, moe_w_group, moe_b_group, moe_w_expert, moe_b_expert, moe_w_gate_up, moe_w_down, final_norm_g):
    bp, sp, d = x_prompt.shape
    db, t_new, _ = x_sample.shape
    ns = db * t_new
    depth = w_ada.shape[0]
    n_pages = page_table.shape[1]
    page = cache_moba_k.shape[1]
    past = n_pages * page

    mods = ada_mods(jnp.concatenate([c_prompt, c_sample], axis=0), w_ada, b_ada)

    def mod_p(i, k):
        return mods[i, :bp, k * d:(k + 1) * d].reshape(bp, 1, d)

    def mod_s(i, k):
        return jnp.repeat(mods[i, bp:, k * d:(k + 1) * d], t_new, axis=0).reshape(1, ns, d)

    xp = x_prompt
    xs = x_sample.reshape(1, ns, d)
    bf = lambda w: w.astype(BF16)
    outs = {}

    for i in range(depth):
        kind = i % 4
        g_mix = norm_mix_g[i].reshape(1, d)
        g_ffn = norm_ffn_g[i].reshape(1, d)
        mp = [mod_p(i, k) for k in range(6)]
        ms = [mod_s(i, k) for k in range(6)]
        if kind == 0:
            nq, nkv = HQ_A * DH_A, HKV_A * DH_A
            offs = [(0, nq), (nq, nkv), (nq + nkv, nkv)]
            w_qkv, w_o = bf(moba_w_qkv), bf(moba_w_o)
            q, k, v = norm_mod_matmul(xp, g_mix, mp[0], mp[1], w_qkv, offs, [F32] * 3, ts=512)
            o = moba_prompt_attention(q, k, v, rel_bias)
            xp = out_proj(xp, mp[2], o, w_o)
            outs['moba_k_prompt'] = k.reshape(bp, sp, HKV_A, DH_A)
            outs['moba_v_prompt'] = v.reshape(bp, sp, HKV_A, DH_A)
            q, k, v = norm_mod_matmul(xs, g_mix, ms[0], ms[1], w_qkv, offs, [F32] * 3, ts=ns)
            tok_minor = lambda c: jnp.transpose(c, (0, 2, 3, 1)).reshape(-1, nkv, page)
            o = moba_sample_attention(q.reshape(db, t_new, nq), k.reshape(db, t_new, nkv), v.reshape(db, t_new, nkv),
                                      tok_minor(cache_moba_k), tok_minor(cache_moba_v), page_table, rel_bias)
            xs = out_proj(xs, ms[2], o.reshape(1, ns, nq), w_o, ts=ns)
            outs['moba_k_sample'] = k.reshape(db, t_new, HKV_A, DH_A)
            outs['moba_v_sample'] = v.reshape(db, t_new, HKV_A, DH_A)
        elif kind == 1:
            w_v = bf(mla_w_kvb[:, :, NOPE_B:].transpose(1, 0, 2))
            w_o = bf(mla_w_o)
            proj = (mla_w_in, mla_q_norm_g, mla_w_qb, mla_kv_norm_g, mla_w_kvb)
            pos_p = jnp.broadcast_to(jnp.arange(sp, dtype=I32)[None], (bp, sp))
            qh, kv, ckv, kpe = mla_in(xp, g_mix, mp[0], mp[1], *proj, pos_p, ts=256)
            o_lat = mla_prompt_attention(qh, kv)
            xp = out_proj(xp, mp[2], o_lat, w_o, w_v=w_v, ts=256)
            outs['mla_ckv_prompt'], outs['mla_kpe_prompt'] = ckv, kpe
            pos_s = jnp.tile(past + jnp.arange(t_new, dtype=I32), db)[None]
            qh, _, ckv, kpe = mla_in(xs, g_mix, ms[0], ms[1], *proj, pos_s, ts=ns)
            lat = ckv.shape[-1]
            qs = qh.reshape(HEADS_B, db, t_new, lat + ROPE_B).transpose(1, 0, 2, 3).reshape(db, HEADS_B * t_new, -1)
            o_lat = mla_sample_attention(qs, ckv.reshape(db, t_new, lat), kpe.reshape(db, t_new, ROPE_B),
                                         cache_mla_ckv, jnp.transpose(cache_mla_kpe, (0, 2, 1)), page_table)
            o_lat = o_lat.reshape(db, HEADS_B, t_new, lat).transpose(0, 2, 1, 3).reshape(1, ns, HEADS_B * lat)
            xs = out_proj(xs, ms[2], o_lat, w_o, w_v=w_v, ts=ns)
            outs['mla_ckv_sample'] = ckv.reshape(db, t_new, lat)
            outs['mla_kpe_sample'] = kpe.reshape(db, t_new, ROPE_B)
        elif kind == 2:
            w_in, w_out = bf(chunk_w_in), bf(chunk_w_out)
            gw = chunk_ln_g.shape[0] // GROUPS_C
            bmix_p = jnp.repeat(chunk_b_s.T, gw, axis=1)
            xp, _ = gmlp(xp, g_mix, mp[0], mp[1], mp[2], w_in, chunk_ln_g, chunk_ln_b, chunk_w_s, bmix_p, w_out,
                         want_v=False)
            reps = CHUNK_C // t_new
            eye = jnp.eye(reps, dtype=F32)
            wmix_s = jnp.einsum('ab,gij->gaibj', eye, chunk_w_s[:, :t_new, :t_new]).reshape(GROUPS_C, CHUNK_C, CHUNK_C)
            bmix_s = jnp.repeat(jnp.tile(chunk_b_s[:, :t_new].T, (reps, 1)), gw, axis=1)
            xs, v_s = gmlp(xs, g_mix, ms[0], ms[1], ms[2], w_in, chunk_ln_g, chunk_ln_b, wmix_s, bmix_s, w_out,
                           want_v=True)
            outs['chunk_v_sample'] = v_s.reshape(db, t_new, -1)
        else:
            w_in, w_out = bf(conv_w_in), bf(conv_w_out)
            xp, st = short_conv(xp, g_mix, mp[0], mp[1], mp[2], w_in, conv_w, w_out)
            outs['conv_prompt'] = st
            f1 = jnp.repeat(state_conv[:, 1], t_new, axis=0).reshape(1, ns, d)
            f2 = jnp.tile(state_conv, (1, t_new // (CONV_W - 1), 1)).reshape(1, ns, d)
            xs, zc = short_conv(xs, g_mix, ms[0], ms[1], ms[2], w_in, conv_w, w_out, fix=(f1, f2), period=t_new,
                                ts=ns)
            outs['conv_sample'] = zc.reshape(db, t_new, d)[:, t_new - (CONV_W - 1):]
        fg = final_norm_g.reshape(1, d) if i == depth - 1 else None
        moe_w = (moe_w_group[i], moe_b_group[i], moe_w_expert[i], moe_b_expert[i], moe_w_gate_up[i], moe_w_down[i])
        xp = hier_moe(xp, g_ffn, mp[3], mp[4], mp[5], *moe_w, rows=256, final_g=fg)
        xs = hier_moe(xs, g_ffn, ms[3], ms[4], ms[5], *moe_w, rows=32, final_g=fg, ts=ns)

    return (xp, xs.reshape(db, t_new, d),
            outs['moba_k_prompt'], outs['moba_v_prompt'], outs['moba_k_sample'], outs['moba_v_sample'],
            outs['mla_ckv_prompt'], outs['mla_kpe_prompt'], outs['mla_ckv_sample'], outs['mla_kpe_sample'],
            outs['chunk_v_sample'], outs['conv_prompt'], outs['conv_sample'])
```

```python
import functools
import math

import numpy as np
import jax
import jax.numpy as jnp
from jax import lax
from jax.experimental import pallas as pl
from jax.experimental.pallas import tpu as pltpu

F32, BF16, I32 = jnp.float32, jnp.bfloat16, jnp.int32
EPS = 1e-6
NEG_INF = -1e30
LANES = 128
SUBLANES = 8

HQ_A, HKV_A, GRP_A, DH_A = 16, 4, 4, 64
BLOCK_A, TOPK_A = 256, 3
NUM_BUCKETS, MAX_DISTANCE = 32, 128
HEADS_B, NOPE_B, ROPE_B, V_B = 16, 64, 32, 64
ROPE_THETA = 10000.0
CHUNK_C, GROUPS_C = 128, 8
CONV_W = 3
N_GROUPS_E, EXPERTS_PER_GROUP, TOPK_E = 4, 8, 2
N_EXPERTS = N_GROUPS_E * EXPERTS_PER_GROUP

NT_DIMS = (((1,), (1,)), ((), ()))
HIGHEST = lax.Precision.HIGHEST


def _cparams(sem, vmem_mb):
    return pltpu.CompilerParams(dimension_semantics=sem, vmem_limit_bytes=vmem_mb << 20)


def _dot(a, b):
    return jnp.dot(a.astype(BF16), b.astype(BF16), preferred_element_type=F32)


def _dot_nt(a, b):
    return lax.dot_general(a.astype(BF16), b.astype(BF16), NT_DIMS, preferred_element_type=F32)


def _rms(x, g):
    return x * lax.rsqrt(jnp.mean(x * x, axis=-1, keepdims=True) + EPS) * g


def _ada_norm(x, g, shift, scale):
    return _rms(x, g) * (1.0 + scale) + shift


def _mod_spec(m, ts):
    d = m.shape[-1]
    if m.shape[1] == 1:
        return pl.BlockSpec((1, 1, d), lambda b, j: (b, 0, 0))
    return pl.BlockSpec((1, ts, d), lambda b, j: (b, j, 0))


def _full_spec(a):
    nd = a.ndim
    return pl.BlockSpec(a.shape, lambda *_: (0,) * nd)


def _row_spec(a, ts):
    return pl.BlockSpec((1, ts) + a.shape[2:], lambda b, j: (b, j) + (0,) * (a.ndim - 2))


def _ada_kernel(c_ref, w_ref, b_ref, o_ref):
    c = c_ref[...]
    o_ref[0] = _dot(c * jax.nn.sigmoid(c), w_ref[0]) + b_ref[0]


def ada_mods(c_all, w_ada, b_ada):
    depth, d, n = w_ada.shape
    rows = c_all.shape[0]
    tn = 1536
    return pl.pallas_call(
        _ada_kernel,
        grid=(depth, n // tn),
        in_specs=[pl.BlockSpec((rows, d), lambda i, j: (0, 0)),
                  pl.BlockSpec((1, d, tn), lambda i, j: (i, 0, j)),
                  pl.BlockSpec((1, 1, tn), lambda i, j: (i, 0, j))],
        out_specs=pl.BlockSpec((1, rows, tn), lambda i, j: (i, 0, j)),
        out_shape=jax.ShapeDtypeStruct((depth, rows, n), F32),
        compiler_params=_cparams(("arbitrary", "arbitrary"), 40),
        name="ada_mods",
    )(c_all, w_ada, b_ada.reshape(depth, 1, n))


def _nmm_kernel(x_ref, g_ref, sh_ref, sc_ref, w_ref, *out_refs, offs):
    h = _ada_norm(x_ref[0], g_ref[...], sh_ref[0], sc_ref[0])
    y = _dot(h, w_ref[...])
    for o_ref, (lo, n) in zip(out_refs, offs):
        o_ref[0] = y[:, lo:lo + n].astype(o_ref.dtype)


def norm_mod_matmul(x, g, shift, scale, w, offs, dtypes, ts):
    b, s, d = x.shape
    outs = [jax.ShapeDtypeStruct((b, s, n), dt) for (_, n), dt in zip(offs, dtypes)]
    return pl.pallas_call(
        functools.partial(_nmm_kernel, offs=tuple(offs)),
        grid=(b, s // ts),
        in_specs=[_row_spec(x, ts), _full_spec(g), _mod_spec(shift, ts), _mod_spec(scale, ts), _full_spec(w)],
        out_specs=[pl.BlockSpec((1, ts, n), lambda bb, j: (bb, j, 0)) for _, n in offs],
        out_shape=outs,
        compiler_params=_cparams(("arbitrary", "arbitrary"), 48),
        name="norm_mod_matmul",
    )(x, g, shift, scale, w)


def _out_proj_kernel(x_ref, gate_ref, a_ref, *rest, n_heads):
    if n_heads:
        wv_ref, wo_ref, o_ref = rest
        a = a_ref[0]
        c = a.shape[1] // n_heads
        a = jnp.concatenate([_dot(a[:, h * c:(h + 1) * c], wv_ref[h]) for h in range(n_heads)], axis=1)
    else:
        wo_ref, o_ref = rest
        a = a_ref[0]
    o_ref[0] = x_ref[0] + gate_ref[0] * _dot(a, wo_ref[...])


def out_proj(x, gate, a, w_o, w_v=None, ts=512):
    b, s, d = x.shape
    ins = [x, gate, a] + ([w_v] if w_v is not None else []) + [w_o]
    specs = [_row_spec(x, ts), _mod_spec(gate, ts), _row_spec(a, ts)]
    specs += ([_full_spec(w_v)] if w_v is not None else []) + [_full_spec(w_o)]
    return pl.pallas_call(
        functools.partial(_out_proj_kernel, n_heads=0 if w_v is None else w_v.shape[0]),
        grid=(b, s // ts),
        in_specs=specs,
        out_specs=_row_spec(x, ts),
        out_shape=jax.ShapeDtypeStruct(x.shape, F32),
        compiler_params=_cparams(("arbitrary", "arbitrary"), 48),
        name="out_proj",
    )(*ins)


def _np_bucket(dist):
    n = np.maximum(dist, 0)
    max_exact = NUM_BUCKETS // 2
    ratio = np.log(np.maximum(n, 1).astype(np.float32) / max_exact) / math.log(MAX_DISTANCE / max_exact)
    large = np.minimum(max_exact + (ratio * (NUM_BUCKETS - max_exact)).astype(np.int32), NUM_BUCKETS - 1)
    return np.where(n < max_exact, n, large).astype(np.int32)


assert int(_np_bucket(np.array(BLOCK_A + 1))) == NUM_BUCKETS - 1


def _bias_table(rel_bias, dist):
    idx = _np_bucket(dist)
    onehot = (jnp.asarray(idx.reshape(-1, 1)) == jnp.arange(NUM_BUCKETS, dtype=I32)[None, :]).astype(F32)
    tab = jnp.einsum('bh,nb->hn', rel_bias, onehot, precision=HIGHEST)
    return tab.reshape((rel_bias.shape[1],) + idx.shape)


def _halves_max(s):
    return jnp.maximum(s[:, :LANES], s[:, LANES:])


def _two_pass_finish(s_ref, mx_ref, l_ref, acc_ref, n_chunks, value_chunk):
    m = jnp.max(mx_ref[...], axis=1, keepdims=True)
    mx_ref[...] = jnp.broadcast_to(m, mx_ref.shape)
    l_ref[...] = jnp.zeros_like(l_ref)
    acc_ref[...] = jnp.zeros_like(acc_ref)

    def body(j, c):
        mb = mx_ref[...]
        p = jnp.exp(s_ref[j] - jnp.concatenate([mb, mb], axis=1))
        l_ref[...] += p[:, :LANES] + p[:, LANES:]
        acc_ref[...] += jnp.dot(p.astype(BF16), value_chunk(j), preferred_element_type=F32)
        return c

    lax.fori_loop(0, n_chunks, body, 0)
    return acc_ref[...] * (1.0 / jnp.sum(l_ref[...], axis=1, keepdims=True))


def _moba_prompt_kernel(far_ref, q_ref, k_ref, v_ref, bown_ref, badj_ref, o_ref,
                        kmbd_ref, mb_ref, s_ref, mx_ref, l_ref, acc_ref, *, nb):
    i = pl.program_id(1)
    bs = BLOCK_A
    scale = DH_A ** -0.5

    @pl.when(i == 0)
    def _():
        kmbd_ref[...] = jnp.zeros_like(kmbd_ref)
        for j in range(nb):
            km = jnp.mean(k_ref[0, j * bs:(j + 1) * bs, :], axis=0, keepdims=True)
            for hq in range(HQ_A):
                kvh = hq // GRP_A
                kmbd_ref[j * HQ_A + hq:j * HQ_A + hq + 1, hq * DH_A:(hq + 1) * DH_A] = km[:, kvh * DH_A:(kvh + 1) * DH_A]

    q = q_ref[0]
    gates = lax.dot_general(q, kmbd_ref[...], NT_DIMS, precision=HIGHEST, preferred_element_type=F32)
    lane = lax.broadcasted_iota(I32, gates.shape, 1)
    jidx = lane // HQ_A
    past = jidx < i
    gm = jnp.where(past, gates, NEG_INF)
    rank = jnp.zeros(gates.shape, I32)
    for k in range(1, nb):
        other = pltpu.roll(gm, HQ_A * k, axis=1)
        jo = jnp.where(jidx >= k, jidx - k, jidx - k + nb)
        beats = (other > gm) | ((other == gm) & (jo < jidx))
        rank = rank + beats.astype(I32)
    mb = jnp.where(past & (rank < TOPK_A), 0.0, NEG_INF).astype(F32)
    for j in range(nb):
        mb_ref[j] = mb[:, j * HQ_A:(j + 1) * HQ_A]

    def kv_block(j, h):
        lo = pl.multiple_of(j * bs, bs)
        return (k_ref[0, pl.ds(lo, bs), h * DH_A:(h + 1) * DH_A].astype(BF16),
                v_ref[0, pl.ds(lo, bs), h * DH_A:(h + 1) * DH_A].astype(BF16))

    for h in range(HKV_A):
        heads = [GRP_A * h + g for g in range(GRP_A)]
        q4 = (jnp.concatenate([q[:, hq * DH_A:(hq + 1) * DH_A] for hq in heads], axis=0) * scale).astype(BF16)

        def mask_col(j, extra):
            mj = mb_ref[j]
            return jnp.concatenate([mj[:, hq:hq + 1] + (extra[n] if extra else 0.0)
                                    for n, hq in enumerate(heads)], axis=0)

        s = _dot_nt(q4, kv_block(i, h)[0]) + bown_ref[GRP_A * h:GRP_A * (h + 1)].reshape(GRP_A * bs, bs)
        s_ref[i] = s
        mx_ref[...] = _halves_max(s)

        def scores(j, bias):
            sj = _dot_nt(q4, kv_block(j, h)[0]) + bias
            s_ref[j] = sj
            mx_ref[...] = jnp.maximum(mx_ref[...], _halves_max(sj))

        @pl.when(i >= 1)
        def _():
            badj = badj_ref[GRP_A * h:GRP_A * (h + 1)].reshape(GRP_A * bs, bs)
            scores(i - 1, badj + mask_col(i - 1, None))

        far = [far_ref[hq] for hq in heads]

        def far_body(j, c):
            scores(j, mask_col(j, far))
            return c

        lax.fori_loop(0, jnp.maximum(i - 1, 0), far_body, 0)

        o = _two_pass_finish(s_ref, mx_ref, l_ref, acc_ref, i + 1, lambda j: kv_block(j, h)[1])
        for g, hq in enumerate(heads):
            o_ref[0, :, hq * DH_A:(hq + 1) * DH_A] = o[g * bs:(g + 1) * bs].astype(o_ref.dtype)


def moba_prompt_attention(q, k, v, rel_bias):
    b, s, _ = q.shape
    nb = s // BLOCK_A
    assert s % BLOCK_A == 0 and nb * HQ_A == LANES and nb >= TOPK_A
    t = np.arange(BLOCK_A)[:, None]
    sk = np.arange(BLOCK_A)[None, :]
    bown = jnp.where(jnp.asarray(t >= sk)[None], _bias_table(rel_bias, t - sk), NEG_INF)
    badj = _bias_table(rel_bias, BLOCK_A + t - sk)
    far = rel_bias[NUM_BUCKETS - 1]
    kv_spec = pl.BlockSpec((1, s, HKV_A * DH_A), lambda bb, i: (bb, 0, 0))
    return pl.pallas_call(
        functools.partial(_moba_prompt_kernel, nb=nb),
        grid=(b, nb),
        in_specs=[pl.BlockSpec(memory_space=pltpu.SMEM),
                  pl.BlockSpec((1, BLOCK_A, HQ_A * DH_A), lambda bb, i: (bb, i, 0)),
                  kv_spec, kv_spec, _full_spec(bown), _full_spec(badj)],
        out_specs=pl.BlockSpec((1, BLOCK_A, HQ_A * DH_A), lambda bb, i: (bb, i, 0)),
        out_shape=jax.ShapeDtypeStruct((b, s, HQ_A * DH_A), BF16),
        scratch_shapes=[pltpu.VMEM((nb * HQ_A, HQ_A * DH_A), F32),
                        pltpu.VMEM((nb, BLOCK_A, HQ_A), F32),
                        pltpu.VMEM((nb, GRP_A * BLOCK_A, BLOCK_A), F32),
                        pltpu.VMEM((GRP_A * BLOCK_A, LANES), F32),
                        pltpu.VMEM((GRP_A * BLOCK_A, LANES), F32),
                        pltpu.VMEM((GRP_A * BLOCK_A, DH_A), F32)],
        compiler_params=_cparams(("arbitrary", "arbitrary"), 48),
        name="moba_prompt_attention",
    )(far, q, k, v, bown, badj)


def _page_copies(cache_hbm, buf, sem, pt_ref, bb, slot, n_pages):
    return [pltpu.make_async_copy(cache_hbm.at[pt_ref[bb * n_pages + p]], buf.at[slot, p], sem.at[slot])
            for p in range(n_pages)]


def _pages_wait(cache_hbm, buf, sem, slot, n_pages):
    pltpu.make_async_copy(cache_hbm.at[pl.ds(0, n_pages)], buf.at[slot], sem.at[slot]).wait()


def _new_token_partials(qs, kn, vn, bnew, n_new):
    qf = qs.astype(BF16).astype(F32)
    kf = kn.astype(BF16).astype(F32)
    vf = vn.astype(BF16).astype(F32)
    cols = [jnp.sum(qf * kf[s:s + 1, :], axis=1, keepdims=True) + bnew[:, s:s + 1] for s in range(n_new)]
    m = functools.reduce(jnp.maximum, cols)
    ps = [jnp.exp(c - m) for c in cols]
    l = functools.reduce(jnp.add, ps)
    acc = functools.reduce(jnp.add, [p.astype(BF16).astype(F32) * vf[s:s + 1, :] for s, p in enumerate(ps)])
    return m, l, acc


def _moba_sample_kernel(pt_ref, q_ref, kn_ref, vn_ref, far_ref, blast_ref, bnew_ref, ck_hbm, cv_hbm, o_ref,
                        kbuf, vbuf, ksem, vsem, qbd_ref, *, n_pages, n_new):
    b = pl.program_id(0)
    slot = b % 2
    page = kbuf.shape[3]
    nb = n_pages * page // BLOCK_A
    ppb = n_pages // nb
    scale = DH_A ** -0.5

    def fetch(bb, sl):
        for c in _page_copies(ck_hbm, kbuf, ksem, pt_ref, bb, sl, n_pages):
            c.start()
        for c in _page_copies(cv_hbm, vbuf, vsem, pt_ref, bb, sl, n_pages):
            c.start()

    @pl.when(b == 0)
    def _():
        fetch(0, 0)

    @pl.when(b + 1 < pl.num_programs(0))
    def _():
        fetch(b + 1, 1 - slot)

    q = q_ref[0]
    qbd_ref[...] = jnp.zeros_like(qbd_ref)
    for hq in range(HQ_A):
        kvh = hq // GRP_A
        qbd_ref[hq * n_new:(hq + 1) * n_new, kvh * DH_A:(kvh + 1) * DH_A] = q[:, hq * DH_A:(hq + 1) * DH_A]
    qbd = qbd_ref[...]
    qs = qbd * scale

    carry = _new_token_partials(qs, kn_ref[0], vn_ref[0], bnew_ref[...], n_new)

    _pages_wait(ck_hbm, kbuf, ksem, slot, n_pages)
    rows = qbd.shape[0]
    qsb = qs.astype(BF16)
    raw = [jnp.dot(qsb, kbuf[slot, p].astype(BF16), preferred_element_type=F32) for p in range(n_pages)]

    lane = lax.broadcasted_iota(I32, (rows, LANES), 1)
    gcur = jnp.full((rows, LANES), NEG_INF, F32)
    for j in range(nb):
        blk = functools.reduce(jnp.add, raw[j * ppb:(j + 1) * ppb])
        gcur = jnp.where(lane == j, jnp.sum(blk, axis=1, keepdims=True) * (1.0 / BLOCK_A), gcur)
    sel = jnp.zeros(gcur.shape, jnp.bool_)
    for _ in range(TOPK_A):
        mx = jnp.max(gcur, axis=1, keepdims=True)
        idx = jnp.min(jnp.where(gcur == mx, lane, LANES), axis=1, keepdims=True)
        pick = lane == idx
        sel = sel | (pick & (mx > 0.5 * NEG_INF))
        gcur = jnp.where(pick, -3e38, gcur)
    selbias = jnp.where(sel, 0.0, NEG_INF).astype(F32)

    _pages_wait(cv_hbm, vbuf, vsem, slot, n_pages)
    far = far_ref[...]

    def page_bias(p):
        j = p // ppb
        col = selbias[:, j:j + 1]
        if j == nb - 1:
            lo = (p % ppb) * page
            return blast_ref[:, lo:lo + page] + col
        return jnp.broadcast_to(col + far, (rows, page))

    scores = [raw[p] + page_bias(p) for p in range(n_pages)]
    m0, l0, acc0 = carry
    m = jnp.maximum(m0, jnp.max(functools.reduce(jnp.maximum, scores), axis=1, keepdims=True))
    alpha = jnp.exp(m0 - m)
    lsum = jnp.zeros((rows, page), F32)
    acc = alpha * acc0
    for p in range(n_pages):
        pr = jnp.exp(scores[p] - m)
        lsum = lsum + pr
        acc = acc + _dot_nt(pr, vbuf[slot, p])
    l = alpha * l0 + jnp.sum(lsum, axis=1, keepdims=True)

    o = acc * (1.0 / l)
    for hq in range(HQ_A):
        kvh = hq // GRP_A
        o_ref[0, :, hq * DH_A:(hq + 1) * DH_A] = (
            o[hq * n_new:(hq + 1) * n_new, kvh * DH_A:(kvh + 1) * DH_A].astype(o_ref.dtype))


def moba_sample_attention(q, k_new, v_new, cache_k, cache_v, page_table, rel_bias):
    db, t_new, _ = q.shape
    n_pages = page_table.shape[1]
    page = cache_k.shape[2]
    past = n_pages * page
    assert past % BLOCK_A == 0 and BLOCK_A % page == 0 and t_new <= BLOCK_A
    nb = past // BLOCK_A
    assert TOPK_A <= nb <= LANES
    tt = np.arange(t_new)[:, None]
    blast = _bias_table(rel_bias, BLOCK_A + tt - np.arange(BLOCK_A)[None, :]).reshape(HQ_A * t_new, BLOCK_A)
    ts = np.arange(t_new)[None, :]
    bnew = jnp.where(jnp.asarray(tt >= ts)[None], _bias_table(rel_bias, tt - ts), NEG_INF)
    bnew = bnew.reshape(HQ_A * t_new, t_new)
    far = jnp.repeat(rel_bias[NUM_BUCKETS - 1], t_new).reshape(HQ_A * t_new, 1)
    rows = HQ_A * t_new
    kw = HKV_A * DH_A
    grid_spec = pltpu.PrefetchScalarGridSpec(
        num_scalar_prefetch=1,
        grid=(db,),
        in_specs=[pl.BlockSpec((1, t_new, HQ_A * DH_A), lambda b, pt: (b, 0, 0)),
                  pl.BlockSpec((1, t_new, kw), lambda b, pt: (b, 0, 0)),
                  pl.BlockSpec((1, t_new, kw), lambda b, pt: (b, 0, 0)),
                  pl.BlockSpec((rows, 1), lambda b, pt: (0, 0)),
                  pl.BlockSpec((rows, BLOCK_A), lambda b, pt: (0, 0)),
                  pl.BlockSpec((rows, t_new), lambda b, pt: (0, 0)),
                  pl.BlockSpec(memory_space=pl.ANY),
                  pl.BlockSpec(memory_space=pl.ANY)],
        out_specs=pl.BlockSpec((1, t_new, HQ_A * DH_A), lambda b, pt: (b, 0, 0)),
        scratch_shapes=[pltpu.VMEM((2, n_pages, kw, page), F32),
                        pltpu.VMEM((2, n_pages, kw, page), F32),
                        pltpu.SemaphoreType.DMA((2,)),
                        pltpu.SemaphoreType.DMA((2,)),
                        pltpu.VMEM((rows, kw), F32)])
    return pl.pallas_call(
        functools.partial(_moba_sample_kernel, n_pages=n_pages, n_new=t_new),
        grid_spec=grid_spec,
        out_shape=jax.ShapeDtypeStruct((db, t_new, HQ_A * DH_A), BF16),
        compiler_params=_cparams(("arbitrary",), 56),
        name="moba_sample_attention",
    )(page_table.reshape(-1), q, k_new, v_new, far, blast, bnew, cache_k, cache_v)


def _mla_in_kernel(x_ref, g_ref, sh_ref, sc_ref, win_ref, qg_ref, wqb_ref, kvg_ref, wn_ref,
                   cq_ref, sq_ref, q_ref, kv_ref, ckv_ref, kpe_ref):
    nope_w = HEADS_B * NOPE_B
    rope_w = HEADS_B * ROPE_B
    lat = kv_ref.shape[-1] - ROPE_B
    qa_w = qg_ref.shape[-1]
    scale = (NOPE_B + ROPE_B) ** -0.5
    h = _ada_norm(x_ref[0], g_ref[...], sh_ref[0], sc_ref[0])
    y = _dot(h, win_ref[...])
    qn = _rms(y[:, :qa_w], qg_ref[...])
    ckv = _rms(y[:, qa_w:qa_w + lat], kvg_ref[...])
    cos = cq_ref[0]
    sin = sq_ref[0]
    k_lo = qa_w + lat
    kpe = y[:, k_lo:k_lo + ROPE_B] * cos[:, :ROPE_B] + y[:, k_lo + LANES:k_lo + LANES + ROPE_B] * sin[:, :ROPE_B]
    ckv_ref[0] = ckv
    kpe_ref[0] = kpe
    kv_ref[0, :, :lat] = ckv.astype(BF16)
    kv_ref[0, :, lat:] = kpe.astype(BF16)
    qq = _dot(qn, wqb_ref[...])
    q_pe = (qq[:, nope_w:nope_w + rope_w] * cos + qq[:, nope_w + rope_w:] * sin) * scale
    for hd in range(HEADS_B):
        q_lat = _dot(qq[:, hd * NOPE_B:(hd + 1) * NOPE_B], wn_ref[hd]) * scale
        q_ref[0, hd, :, :lat] = q_lat.astype(BF16)
        q_ref[0, hd, :, lat:] = q_pe[:, hd * ROPE_B:(hd + 1) * ROPE_B].astype(BF16)


def _rope_swap_cols(w, n_rope):
    half = n_rope // 2
    return jnp.concatenate([w[..., half:], w[..., :half]], axis=-1)


def mla_in(x, g, shift, scale, w_in, q_norm_g, w_qb, kv_norm_g, w_kvb, pos, ts):
    b, s, d = x.shape
    qa_w = q_norm_g.shape[0]
    lat = kv_norm_g.shape[0]
    kpe_w = w_in[:, qa_w + lat:]
    pad = jnp.zeros((d, LANES - ROPE_B), w_in.dtype)
    w_in_ext = jnp.concatenate([w_in[:, :qa_w + lat], kpe_w, pad, _rope_swap_cols(kpe_w, ROPE_B), pad], axis=1).astype(BF16)
    wq = w_qb.reshape(qa_w, HEADS_B, NOPE_B + ROPE_B)
    wq_rope = wq[:, :, NOPE_B:]
    w_qb_ext = jnp.concatenate([wq[:, :, :NOPE_B].reshape(qa_w, -1), wq_rope.reshape(qa_w, -1),
                                _rope_swap_cols(wq_rope, ROPE_B).reshape(qa_w, -1)], axis=1).astype(BF16)
    w_nope = w_kvb[:, :, :NOPE_B].transpose(1, 2, 0).astype(BF16)
    half = ROPE_B // 2
    inv = ROPE_THETA ** (-jnp.arange(half, dtype=F32) / half)
    ang = pos.astype(F32)[..., None] * inv
    cos, sin = jnp.cos(ang), jnp.sin(ang)
    cos_t = jnp.tile(jnp.concatenate([cos, cos], axis=-1), (1, 1, HEADS_B))
    sin_t = jnp.tile(jnp.concatenate([-sin, sin], axis=-1), (1, 1, HEADS_B))
    outs = [jax.ShapeDtypeStruct((b, HEADS_B, s, lat + ROPE_B), BF16),
            jax.ShapeDtypeStruct((b, s, lat + ROPE_B), BF16),
            jax.ShapeDtypeStruct((b, s, lat), F32),
            jax.ShapeDtypeStruct((b, s, ROPE_B), F32)]
    qg = q_norm_g.reshape(1, -1)
    kvg = kv_norm_g.reshape(1, -1)
    return pl.pallas_call(
        _mla_in_kernel,
        grid=(b, s // ts),
        in_specs=[_row_spec(x, ts), _full_spec(g), _mod_spec(shift, ts), _mod_spec(scale, ts),
                  _full_spec(w_in_ext), _full_spec(qg), _full_spec(w_qb_ext), _full_spec(kvg), _full_spec(w_nope),
                  _row_spec(cos_t, ts), _row_spec(sin_t, ts)],
        out_specs=[pl.BlockSpec((1, HEADS_B, ts, lat + ROPE_B), lambda bb, j: (bb, 0, j, 0)),
                   pl.BlockSpec((1, ts, lat + ROPE_B), lambda bb, j: (bb, j, 0)),
                   pl.BlockSpec((1, ts, lat), lambda bb, j: (bb, j, 0)),
                   pl.BlockSpec((1, ts, ROPE_B), lambda bb, j: (bb, j, 0))],
        out_shape=outs,
        compiler_params=_cparams(("arbitrary", "arbitrary"), 48),
        name="mla_in",
    )(x, g, shift, scale, w_in_ext, qg, w_qb_ext, kvg, w_nope, cos_t, sin_t)


def _mla_prompt_kernel(q_ref, kv_ref, o_ref, s_ref, mx_ref, l_ref, acc_ref, *, tq, hc):
    i = pl.program_id(1)
    lat = acc_ref.shape[-1]
    rows = hc * tq
    row_t = lax.broadcasted_iota(I32, (rows, tq), 0) % tq
    col_s = lax.broadcasted_iota(I32, (rows, tq), 1)
    causal = col_s <= row_t

    def kv_chunk(j):
        return kv_ref[0, pl.ds(pl.multiple_of(j * tq, tq), tq), :]

    for c in range(HEADS_B // hc):
        q = q_ref[0, c * hc:(c + 1) * hc].reshape(rows, q_ref.shape[-1])

        s = jnp.where(causal, lax.dot_general(q, kv_chunk(i), NT_DIMS, preferred_element_type=F32), NEG_INF)
        s_ref[i] = s
        mx_ref[...] = _halves_max(s)

        def body(j, cc):
            sj = lax.dot_general(q, kv_chunk(j), NT_DIMS, preferred_element_type=F32)
            s_ref[j] = sj
            mx_ref[...] = jnp.maximum(mx_ref[...], _halves_max(sj))
            return cc

        lax.fori_loop(0, i, body, 0)
        o = _two_pass_finish(s_ref, mx_ref, l_ref, acc_ref, i + 1, lambda j: kv_chunk(j)[:, :lat])
        for hl in range(hc):
            hd = c * hc + hl
            o_ref[0, :, hd * lat:(hd + 1) * lat] = o[hl * tq:(hl + 1) * tq].astype(o_ref.dtype)


def mla_prompt_attention(q, kv, hc=4):
    b, nh, s, w = q.shape
    lat = w - ROPE_B
    tq = 2 * LANES
    assert s % tq == 0
    return pl.pallas_call(
        functools.partial(_mla_prompt_kernel, tq=tq, hc=hc),
        grid=(b, s // tq),
        in_specs=[pl.BlockSpec((1, nh, tq, w), lambda bb, i: (bb, 0, i, 0)),
                  pl.BlockSpec((1, s, w), lambda bb, i: (bb, 0, 0))],
        out_specs=pl.BlockSpec((1, tq, nh * lat), lambda bb, i: (bb, i, 0)),
        out_shape=jax.ShapeDtypeStruct((b, s, nh * lat), BF16),
        scratch_shapes=[pltpu.VMEM((s // tq, hc * tq, tq), F32),
                        pltpu.VMEM((hc * tq, LANES), F32), pltpu.VMEM((hc * tq, LANES), F32),
                        pltpu.VMEM((hc * tq, lat), F32)],
        compiler_params=_cparams(("arbitrary", "arbitrary"), 48),
        name="mla_prompt_attention",
    )(q, kv)


def _mla_sample_kernel(pt_ref, q_ref, cn_ref, pn_ref, bnew_ref, cc_hbm, cp_hbm, o_ref,
                       cbuf, pbuf, csem, psem, *, n_pages, n_new, chunk_pages):
    b = pl.program_id(0)
    slot = b % 2
    lat = cbuf.shape[-1]
    page = cbuf.shape[2]

    def fetch(bb, sl):
        for c in _page_copies(cc_hbm, cbuf, csem, pt_ref, bb, sl, n_pages):
            c.start()
        for c in _page_copies(cp_hbm, pbuf, psem, pt_ref, bb, sl, n_pages):
            c.start()

    @pl.when(b == 0)
    def _():
        fetch(0, 0)

    @pl.when(b + 1 < pl.num_programs(0))
    def _():
        fetch(b + 1, 1 - slot)

    q = q_ref[0]
    ql, qp = q[:, :lat], q[:, lat:]
    qf, qpf = ql.astype(F32), qp.astype(F32)
    cn = cn_ref[0].astype(BF16).astype(F32)
    pn = pn_ref[0].astype(BF16).astype(F32)
    bnew = bnew_ref[...]
    cols = [jnp.sum(qf * cn[s:s + 1, :], axis=1, keepdims=True) + jnp.sum(qpf * pn[s:s + 1, :], axis=1, keepdims=True)
            + bnew[:, s:s + 1] for s in range(n_new)]
    _pages_wait(cc_hbm, cbuf, csem, slot, n_pages)
    _pages_wait(cp_hbm, pbuf, psem, slot, n_pages)

    def chunk(buf, c):
        blk = buf[slot, c * chunk_pages:(c + 1) * chunk_pages]
        return blk.reshape(chunk_pages * page, blk.shape[-1]).astype(BF16)

    def rope_scores(c):
        return jnp.concatenate([jnp.dot(qp, pbuf[slot, p].astype(BF16), preferred_element_type=F32)
                                for p in range(c * chunk_pages, (c + 1) * chunk_pages)], axis=1)

    n_chunks = n_pages // chunk_pages
    scores = [_dot_nt(ql, chunk(cbuf, c)) + rope_scores(c) for c in range(n_chunks)]
    m = functools.reduce(jnp.maximum, cols + [jnp.max(s, axis=1, keepdims=True) for s in scores])
    ps = [jnp.exp(c - m) for c in cols]
    l = functools.reduce(jnp.add, ps)
    acc = functools.reduce(jnp.add, [p.astype(BF16).astype(F32) * cn[s:s + 1, :] for s, p in enumerate(ps)])
    for c, s in enumerate(scores):
        p = jnp.exp(s - m)
        l = l + jnp.sum(p, axis=1, keepdims=True)
        acc = acc + jnp.dot(p.astype(BF16), chunk(cbuf, c), preferred_element_type=F32)
    o_ref[0] = (acc * (1.0 / l)).astype(o_ref.dtype)


def mla_sample_attention(q, ckv_new, kpe_new, cache_ckv, cache_kpe, page_table):
    db, rows, w = q.shape
    t_new = ckv_new.shape[1]
    lat = ckv_new.shape[2]
    n_pages = page_table.shape[1]
    page = cache_ckv.shape[1]
    tt = np.arange(t_new)[:, None]
    ts = np.arange(t_new)[None, :]
    bnew = jnp.asarray(np.tile(np.where(tt >= ts, 0.0, NEG_INF).astype(np.float32), (rows // t_new, 1)))
    grid_spec = pltpu.PrefetchScalarGridSpec(
        num_scalar_prefetch=1,
        grid=(db,),
        in_specs=[pl.BlockSpec((1, rows, w), lambda b, pt: (b, 0, 0)),
                  pl.BlockSpec((1, t_new, lat), lambda b, pt: (b, 0, 0)),
                  pl.BlockSpec((1, t_new, ROPE_B), lambda b, pt: (b, 0, 0)),
                  pl.BlockSpec((rows, t_new), lambda b, pt: (0, 0)),
                  pl.BlockSpec(memory_space=pl.ANY),
                  pl.BlockSpec(memory_space=pl.ANY)],
        out_specs=pl.BlockSpec((1, rows, lat), lambda b, pt: (b, 0, 0)),
        scratch_shapes=[pltpu.VMEM((2, n_pages, page, lat), F32),
                        pltpu.VMEM((2, n_pages, ROPE_B, page), F32),
                        pltpu.SemaphoreType.DMA((2,)),
                        pltpu.SemaphoreType.DMA((2,))])
    return pl.pallas_call(
        functools.partial(_mla_sample_kernel, n_pages=n_pages, n_new=t_new,
                          chunk_pages=n_pages // 4 if n_pages % 4 == 0 else n_pages),
        grid_spec=grid_spec,
        out_shape=jax.ShapeDtypeStruct((db, rows, lat), BF16),
        compiler_params=_cparams(("arbitrary",), 40),
        name="mla_sample_attention",
    )(page_table.reshape(-1), q, ckv_new, kpe_new, bnew, cache_ckv, cache_kpe)


def _gmlp_kernel(x_ref, g_ref, sh_ref, sc_ref, gate_ref, win_ref, lng_ref, lnb_ref, wmix_ref, bmix_ref, wout_ref,
                 o_ref, *v_out):
    dv = lng_ref.shape[-1]
    gw = dv // GROUPS_C
    ts = x_ref.shape[1]
    x = x_ref[0]
    h = _ada_norm(x, g_ref[...], sh_ref[0], sc_ref[0])
    y = _dot(h, win_ref[...])
    a = 0.5 * y * (1.0 + lax.erf(y * (1.0 / math.sqrt(2.0))))
    u, v = a[:, :dv], a[:, dv:]
    mu = jnp.mean(v, axis=-1, keepdims=True)
    vc = v - mu
    var = jnp.mean(vc * vc, axis=-1, keepdims=True)
    v = vc * lax.rsqrt(var + EPS) * lng_ref[...] + lnb_ref[...]
    if v_out:
        v_out[0][0] = v
    r = lax.broadcasted_iota(I32, (CHUNK_C, CHUNK_C), 0)
    c = lax.broadcasted_iota(I32, (CHUNK_C, CHUNK_C), 1)
    tril = r >= c
    wm = [jnp.where(tril, wmix_ref[gi], 0.0).astype(BF16) for gi in range(GROUPS_C)]
    vb = v.astype(BF16)
    outs = []
    for ci in range(ts // CHUNK_C):
        rows = slice(ci * CHUNK_C, (ci + 1) * CHUNK_C)
        mixed = jnp.concatenate(
            [jnp.dot(wm[gi], vb[rows, gi * gw:(gi + 1) * gw], preferred_element_type=F32) for gi in range(GROUPS_C)],
            axis=1) + bmix_ref[...]
        outs.append(_dot(u[rows] * mixed, wout_ref[...]))
    o_ref[0] = x + gate_ref[0] * jnp.concatenate(outs, axis=0)


def gmlp(x, g, shift, scale, gate, w_in, ln_g, ln_b, wmix, bmix, w_out, want_v, ts=256):
    b, s, d = x.shape
    dv = ln_g.shape[0]
    lng, lnb = ln_g.reshape(1, dv), ln_b.reshape(1, dv)
    outs = [jax.ShapeDtypeStruct(x.shape, F32)]
    out_specs = [_row_spec(x, ts)]
    if want_v:
        outs.append(jax.ShapeDtypeStruct((b, s, dv), F32))
        out_specs.append(pl.BlockSpec((1, ts, dv), lambda bb, j: (bb, j, 0)))
    res = pl.pallas_call(
        _gmlp_kernel,
        grid=(b, s // ts),
        in_specs=[_row_spec(x, ts), _full_spec(g), _mod_spec(shift, ts), _mod_spec(scale, ts), _mod_spec(gate, ts),
                  _full_spec(w_in), _full_spec(lng), _full_spec(lnb), _full_spec(wmix), _full_spec(bmix),
                  _full_spec(w_out)],
        out_specs=out_specs,
        out_shape=outs,
        compiler_params=_cparams(("arbitrary", "arbitrary"), 56),
        name="gmlp",
    )(x, g, shift, scale, gate, w_in, lng, lnb, wmix, bmix, w_out)
    return res if want_v else (res[0], None)


def _conv_kernel(x_ref, g_ref, sh_ref, sc_ref, gate_ref, win_ref, cw_ref, wout_ref, *rest, period):
    ts, d = x_ref.shape[1], x_ref.shape[2]
    if period:
        f1_ref, f2_ref, o_ref, zc_ref, zp_ref = rest
    else:
        o_ref, st_ref, zp_ref = rest
    x = x_ref[0]
    h = _ada_norm(x, g_ref[...], sh_ref[0], sc_ref[0])
    y = _dot(h, win_ref[...])
    gate_b, zc = y[:, :d], y[:, d:2 * d] * y[:, 2 * d:]
    if period:
        zp_ref[0:SUBLANES] = jnp.zeros((SUBLANES, d), F32)
    else:
        @pl.when(pl.program_id(1) == 0)
        def _():
            zp_ref[0:SUBLANES] = jnp.zeros((SUBLANES, d), F32)
    zp_ref[SUBLANES:] = zc
    r1 = zp_ref[SUBLANES - 1:SUBLANES - 1 + ts]
    r2 = zp_ref[SUBLANES - 2:SUBLANES - 2 + ts]
    if period:
        t = lax.broadcasted_iota(I32, (ts, 1), 0) % period
        r1 = jnp.where(t >= 1, r1, f1_ref[0])
        r2 = jnp.where(t >= 2, r2, f2_ref[0])
        zc_ref[0] = zc
    else:
        st_ref[0] = zc[ts - (CONV_W - 1):]
        zp_ref[0:SUBLANES] = zc[ts - SUBLANES:]
    cw = cw_ref[...]
    yc = cw[0:1] * r2 + cw[1:2] * r1 + cw[2:3] * zc
    o_ref[0] = x + gate_ref[0] * _dot(gate_b * yc, wout_ref[...])


def short_conv(x, g, shift, scale, gate, w_in, conv_w, w_out, fix=None, period=0, ts=512):
    b, s, d = x.shape
    assert CONV_W == 3
    ins = [x, g, shift, scale, gate, w_in, conv_w, w_out]
    specs = [_row_spec(x, ts), _full_spec(g), _mod_spec(shift, ts), _mod_spec(scale, ts), _mod_spec(gate, ts),
             _full_spec(w_in), _full_spec(conv_w), _full_spec(w_out)]
    if period:
        ins += list(fix)
        specs += [_row_spec(x, ts), _row_spec(x, ts)]
        outs = [jax.ShapeDtypeStruct(x.shape, F32), jax.ShapeDtypeStruct(x.shape, F32)]
        out_specs = [_row_spec(x, ts), _row_spec(x, ts)]
    else:
        outs = [jax.ShapeDtypeStruct(x.shape, F32), jax.ShapeDtypeStruct((b, CONV_W - 1, d), F32)]
        out_specs = [_row_spec(x, ts), pl.BlockSpec((1, CONV_W - 1, d), lambda bb, j: (bb, 0, 0))]
    return pl.pallas_call(
        functools.partial(_conv_kernel, period=period),
        grid=(b, s // ts),
        in_specs=specs,
        out_specs=out_specs,
        out_shape=outs,
        scratch_shapes=[pltpu.VMEM((SUBLANES + ts, d), F32)],
        compiler_params=_cparams(("arbitrary", "arbitrary"), 56),
        name="short_conv",
    )(*ins)


def _router_kernel(x_ref, g_ref, sh_ref, sc_ref, whi_ref, wlo_ref, bias_ref, ltri_ref, route_ref, cnt_ref, run_ref):
    first = (pl.program_id(0) == 0) & (pl.program_id(1) == 0)

    @pl.when(first)
    def _():
        run_ref[...] = jnp.zeros_like(run_ref)

    h = _ada_norm(x_ref[0], g_ref[...], sh_ref[0], sc_ref[0])
    h_hi = h.astype(BF16)
    h_lo = (h - h_hi.astype(F32)).astype(BF16)
    logits = (jnp.dot(h_hi, whi_ref[...], preferred_element_type=F32)
              + jnp.dot(h_lo, whi_ref[...], preferred_element_type=F32)
              + jnp.dot(h_hi, wlo_ref[...], preferred_element_type=F32)) + bias_ref[...]
    lane = lax.broadcasted_iota(I32, logits.shape, 1)
    low = -3e38
    is_g = lane < N_GROUPS_E
    lg = jnp.where(is_g, logits, low)
    mg = jnp.max(lg, axis=1, keepdims=True)
    g_top = jnp.min(jnp.where(lg == mg, lane, LANES), axis=1, keepdims=True)
    p_g = 1.0 / jnp.sum(jnp.where(is_g, jnp.exp(lg - mg), 0.0), axis=1, keepdims=True)
    e_lo = N_GROUPS_E + g_top * EXPERTS_PER_GROUP
    le = jnp.where((lane >= e_lo) & (lane < e_lo + EXPERTS_PER_GROUP), logits, low)
    m1 = jnp.max(le, axis=1, keepdims=True)
    i1 = jnp.min(jnp.where(le == m1, lane, LANES), axis=1, keepdims=True)
    le2 = jnp.where(lane == i1, low, le)
    m2 = jnp.max(le2, axis=1, keepdims=True)
    i2 = jnp.min(jnp.where(le2 == m2, lane, LANES), axis=1, keepdims=True)
    e2 = jnp.exp(m2 - m1)
    w1 = p_g / (1.0 + e2)
    w2 = p_g * e2 / (1.0 + e2)
    e1 = i1 - N_GROUPS_E
    e2i = i2 - N_GROUPS_E
    oh1 = (lane == e1).astype(F32)
    oh2 = (lane == e2i).astype(F32)
    cnt = oh1 + oh2
    base = jnp.dot(ltri_ref[...], cnt.astype(BF16), preferred_element_type=F32) + run_ref[...]
    r1 = jnp.sum(oh1 * base, axis=1, keepdims=True)
    r2 = jnp.sum(oh2 * base, axis=1, keepdims=True)
    before = run_ref[...]
    run = before + jnp.sum(cnt, axis=0, keepdims=True)
    run_ref[...] = run
    sub = lax.broadcasted_iota(I32, cnt_ref.shape, 0)
    cnt_ref[...] = jnp.where(sub == 0, before, jnp.where(sub == 1, run, 0.0))
    vals = (e1.astype(F32), e2i.astype(F32), w1, w2, r1, r2)
    route = jnp.zeros(logits.shape, F32)
    for n, val in enumerate(vals):
        route = jnp.where(lane == n, val, route)
    route_ref[...] = route


def moe_router(x, g, shift, scale, w_group, b_group, w_expert, b_expert, ts=512):
    b, s, d = x.shape
    n = b * s
    w = jnp.concatenate([w_group, w_expert], axis=1)
    w = jnp.pad(w, ((0, 0), (0, LANES - w.shape[1])))
    w_hi = w.astype(BF16)
    w_lo = (w - w_hi.astype(F32)).astype(BF16)
    bias = jnp.pad(jnp.concatenate([b_group, b_expert]), (0, LANES - N_GROUPS_E - N_EXPERTS)).reshape(1, LANES)
    ltri = jnp.asarray(np.tril(np.ones((ts, ts), np.float32), -1), BF16)
    nj = s // ts
    return pl.pallas_call(
        _router_kernel,
        grid=(b, nj),
        in_specs=[_row_spec(x, ts), _full_spec(g), _mod_spec(shift, ts), _mod_spec(scale, ts),
                  _full_spec(w_hi), _full_spec(w_lo), _full_spec(bias), _full_spec(ltri)],
        out_specs=[pl.BlockSpec((ts, LANES), lambda bb, j: (bb * nj + j, 0)),
                   pl.BlockSpec((SUBLANES, LANES), lambda bb, j: (bb * nj + j, 0))],
        out_shape=[jax.ShapeDtypeStruct((n, LANES), F32), jax.ShapeDtypeStruct((b * nj * SUBLANES, LANES), F32)],
        scratch_shapes=[pltpu.VMEM((1, LANES), F32)],
        compiler_params=_cparams(("arbitrary", "arbitrary"), 40),
        name="moe_router",
    )(x, g, shift, scale, w_hi, w_lo, bias, ltri)


RUN_PIECE_LOG2 = 6


def _run_pieces(n, piece):
    big = 1 << RUN_PIECE_LOG2
    n_big = n >> RUN_PIECE_LOG2

    def body(it, c):
        piece(it * big, big)
        return c

    lax.fori_loop(0, n_big, body, 0)
    base = n_big * big
    rem = n - base
    for bit in range(RUN_PIECE_LOG2 - 1, -1, -1):
        @pl.when(((rem >> bit) & 1) == 1)
        def _():
            piece(base + ((rem >> (bit + 1)) << (bit + 1)), 1 << bit)


def _tile_rows(ref, lead, p, size):
    return ref.at[lead + (pl.ds(pl.multiple_of(p * SUBLANES, SUBLANES), size * SUBLANES), slice(None))]


def _tile_runs(meta_ref):
    return [(meta_ref[0, 0, e], meta_ref[0, 0, N_EXPERTS + e], meta_ref[0, 0, 2 * N_EXPERTS + e])
            for e in range(N_EXPERTS)]


def _dispatch_kernel(seg_ref, meta_ref, lpos_ref, x_ref, g_ref, sh_ref, sc_ref, xs_hbm,
                     hbuf, gbuf, zero_ref, sem, zsem, *, ts):
    i = pl.program_id(0) * pl.num_programs(1) + pl.program_id(1)
    n = pl.num_programs(0) * pl.num_programs(1)
    slot = i % 2
    n_seg = seg_ref.shape[0] // 2

    def slot_wait(sl):
        pltpu.make_async_copy(gbuf.at[sl], xs_hbm.at[pl.ds(0, gbuf.shape[1]), :], sem.at[sl]).wait()

    @pl.when(i == 0)
    def _():
        zero_ref[...] = jnp.zeros_like(zero_ref)
        for phase in ("start", "wait"):
            def seg_body(sg, c):
                def fill(off, size):
                    cp = pltpu.make_async_copy(zero_ref.at[pl.ds(0, size * SUBLANES), :],
                                               _tile_rows(xs_hbm, (), seg_ref[sg] + off, size), zsem.at[0])
                    cp.start() if phase == "start" else cp.wait()
                _run_pieces(seg_ref[n_seg + sg], fill)
                return c
            lax.fori_loop(0, n_seg, seg_body, 0)

    @pl.when(i >= 2)
    def _():
        slot_wait(slot)

    h = _ada_norm(x_ref[0], g_ref[...], sh_ref[0], sc_ref[0])
    for s in range(SUBLANES):
        hbuf[pl.ds(s, ts, stride=SUBLANES), :] = h[:, s * LANES:(s + 1) * LANES]
    for r in range(ts):
        row = hbuf[r * SUBLANES:(r + 1) * SUBLANES, :]
        for k in range(TOPK_E):
            lp = lpos_ref[0, 0, TOPK_E * r + k]
            gbuf[slot, pl.ds(pl.multiple_of(lp * SUBLANES, SUBLANES), SUBLANES), :] = row
    for first, local, length in _tile_runs(meta_ref):
        def send(off, size, first=first, local=local):
            pltpu.make_async_copy(_tile_rows(gbuf, (slot,), local + off, size),
                                  _tile_rows(xs_hbm, (), first + off, size), sem.at[slot]).start()
        _run_pieces(length, send)

    @pl.when(i == n - 1)
    def _():
        slot_wait(slot)

        @pl.when(n >= 2)
        def _():
            slot_wait(1 - slot)


def moe_dispatch(x, g, shift, scale, lpos, meta, seg, n_rows, ts):
    b, s, d = x.shape
    assert d == SUBLANES * LANES
    nj = s // ts
    items = TOPK_E * ts
    grid_spec = pltpu.PrefetchScalarGridSpec(
        num_scalar_prefetch=1,
        grid=(b, nj),
        in_specs=[pl.BlockSpec((1, 1, 3 * N_EXPERTS), lambda bb, j, sg: (bb * nj + j, 0, 0), memory_space=pltpu.SMEM),
                  pl.BlockSpec((1, 1, items), lambda bb, j, sg: (bb * nj + j, 0, 0), memory_space=pltpu.SMEM),
                  pl.BlockSpec((1, ts, d), lambda bb, j, sg: (bb, j, 0)),
                  pl.BlockSpec(g.shape, lambda bb, j, sg: (0, 0)),
                  (pl.BlockSpec((1, 1, d), lambda bb, j, sg: (bb, 0, 0)) if shift.shape[1] == 1
                   else pl.BlockSpec((1, ts, d), lambda bb, j, sg: (bb, j, 0))),
                  (pl.BlockSpec((1, 1, d), lambda bb, j, sg: (bb, 0, 0)) if scale.shape[1] == 1
                   else pl.BlockSpec((1, ts, d), lambda bb, j, sg: (bb, j, 0)))],
        out_specs=pl.BlockSpec(memory_space=pl.ANY),
        scratch_shapes=[pltpu.VMEM((ts * SUBLANES, LANES), F32), pltpu.VMEM((2, items * SUBLANES, LANES), F32),
                        pltpu.VMEM(((1 << RUN_PIECE_LOG2) * SUBLANES, LANES), F32),
                        pltpu.SemaphoreType.DMA((2,)), pltpu.SemaphoreType.DMA((1,))])
    return pl.pallas_call(
        functools.partial(_dispatch_kernel, ts=ts),
        grid_spec=grid_spec,
        out_shape=jax.ShapeDtypeStruct((n_rows * SUBLANES, LANES), F32),
        compiler_params=_cparams(("arbitrary", "arbitrary"), 48),
        name="moe_dispatch",
    )(seg, meta.reshape(b * nj, 1, 3 * N_EXPERTS), lpos.reshape(b * nj, 1, items), x, g, shift, scale)


def _expert_kernel(be_ref, x_ref, wgu_ref, wd_ref, y_ref, *, rows):
    x = jnp.concatenate([x_ref[pl.ds(s, rows, stride=SUBLANES), :] for s in range(SUBLANES)], axis=1)
    ab = _dot(x, wgu_ref[0, 0])
    f = ab.shape[1] // 2
    a, bb = ab[:, :f], ab[:, f:]
    y = _dot(a * jax.nn.sigmoid(a) * bb, wd_ref[0, 0])
    for s in range(SUBLANES):
        y_ref[pl.ds(s, rows, stride=SUBLANES), :] = y[:, s * LANES:(s + 1) * LANES]


def moe_experts(x_rows, blk_e, w_gate_up, w_down, layer, rows):
    n_blk = blk_e.shape[0]
    _, ne, d, f2 = w_gate_up.shape
    grid_spec = pltpu.PrefetchScalarGridSpec(
        num_scalar_prefetch=1,
        grid=(n_blk,),
        in_specs=[pl.BlockSpec((rows * SUBLANES, LANES), lambda i, be: (i, 0)),
                  pl.BlockSpec((1, 1, d, f2), lambda i, be: (layer, be[i], 0, 0)),
                  pl.BlockSpec((1, 1, f2 // 2, d), lambda i, be: (layer, be[i], 0, 0))],
        out_specs=pl.BlockSpec((rows * SUBLANES, LANES), lambda i, be: (i, 0)))
    return pl.pallas_call(
        functools.partial(_expert_kernel, rows=rows),
        grid_spec=grid_spec,
        out_shape=jax.ShapeDtypeStruct(x_rows.shape, F32),
        compiler_params=_cparams(("arbitrary",), 40),
        name="moe_experts",
    )(blk_e, x_rows, w_gate_up, w_down)


def _combine_kernel(cur_ref, nxt_ref, lpos_ref, x_ref, gate_ref, route_ref, y_hbm, *rest, ts, final):
    if final:
        fg_ref, o_ref, ybuf, tbuf, sem = rest
    else:
        o_ref, ybuf, tbuf, sem = rest
    i = pl.program_id(0) * pl.num_programs(1) + pl.program_id(1)
    n = pl.num_programs(0) * pl.num_programs(1)
    slot = i % 2

    def fetch(meta_ref, sl):
        for first, local, length in _tile_runs(meta_ref):
            def recv(off, size, first=first, local=local):
                pltpu.make_async_copy(_tile_rows(y_hbm, (), first + off, size),
                                      _tile_rows(ybuf, (sl,), local + off, size), sem.at[sl]).start()
            _run_pieces(length, recv)

    @pl.when(i == 0)
    def _():
        fetch(cur_ref, 0)

    @pl.when(i + 1 < n)
    def _():
        fetch(nxt_ref, 1 - slot)

    pltpu.make_async_copy(y_hbm.at[pl.ds(0, ybuf.shape[1]), :], ybuf.at[slot], sem.at[slot]).wait()
    for r in range(ts):
        for k in range(TOPK_E):
            lp = lpos_ref[0, 0, TOPK_E * r + k]
            tbuf[(k * ts + r) * SUBLANES:(k * ts + r + 1) * SUBLANES, :] = (
                ybuf[slot, pl.ds(pl.multiple_of(lp * SUBLANES, SUBLANES), SUBLANES), :])
    ys = [jnp.concatenate([tbuf[pl.ds(k * ts * SUBLANES + s, ts, stride=SUBLANES), :] for s in range(SUBLANES)],
                          axis=1) for k in range(TOPK_E)]
    route = route_ref[...]
    y = route[:, 2:3] * ys[0] + route[:, 3:4] * ys[1]
    out = x_ref[0] + gate_ref[0] * y
    if final:
        out = _rms(out, fg_ref[...])
    o_ref[0] = out


def moe_combine(x, gate, route, y_rows, lpos, meta, ts, final_g=None):
    b, s, d = x.shape
    nj = s // ts
    n_tiles = b * nj
    items = TOPK_E * ts
    meta3 = meta.reshape(n_tiles, 1, 3 * N_EXPERTS)
    meta_blk = lambda f: pl.BlockSpec((1, 1, 3 * N_EXPERTS), f, memory_space=pltpu.SMEM)
    ins = [meta3, meta3, lpos.reshape(n_tiles, 1, items), x, gate, route, y_rows]
    specs = [meta_blk(lambda bb, j: (bb * nj + j, 0, 0)),
             meta_blk(lambda bb, j: (jnp.minimum(bb * nj + j + 1, n_tiles - 1), 0, 0)),
             pl.BlockSpec((1, 1, items), lambda bb, j: (bb * nj + j, 0, 0), memory_space=pltpu.SMEM),
             _row_spec(x, ts), _mod_spec(gate, ts),
             pl.BlockSpec((ts, LANES), lambda bb, j: (bb * nj + j, 0)),
             pl.BlockSpec(memory_space=pl.ANY)]
    if final_g is not None:
        ins.append(final_g)
        specs.append(_full_spec(final_g))
    return pl.pallas_call(
        functools.partial(_combine_kernel, ts=ts, final=final_g is not None),
        grid=(b, nj),
        in_specs=specs,
        out_specs=_row_spec(x, ts),
        out_shape=jax.ShapeDtypeStruct(x.shape, F32),
        scratch_shapes=[pltpu.VMEM((2, items * SUBLANES, LANES), F32), pltpu.VMEM((items * SUBLANES, LANES), F32),
                        pltpu.SemaphoreType.DMA((2,))],
        compiler_params=_cparams(("arbitrary", "arbitrary"), 48),
        name="moe_combine",
    )(*ins)


def _route_layout(route, tiles, rows, ts):
    n_tok = route.shape[0]
    n_items = n_tok * TOPK_E
    n_tiles = n_tok // ts
    eid = route[:, :TOPK_E].astype(I32).reshape(n_tiles, ts, TOPK_E)
    rank = route[:, 2 * TOPK_E:3 * TOPK_E].astype(I32).reshape(n_tiles, ts, TOPK_E)
    t3 = tiles.reshape(n_tiles, SUBLANES, LANES)
    before = t3[:, 0, :N_EXPERTS].astype(I32)
    after = t3[:, 1, :N_EXPERTS].astype(I32)
    length = after - before
    counts = after[-1]
    padded = (counts + rows - 1) // rows * rows
    pend = jnp.cumsum(padded)
    pstart = pend - padded
    local = jnp.cumsum(length, axis=1) - length
    experts = jnp.arange(N_EXPERTS, dtype=I32)
    shift = (local - before)[:, None, None, :]
    lpos = jnp.sum(jnp.where(eid[..., None] == experts, shift, 0), axis=-1) + rank
    meta = jnp.concatenate([pstart[None, :] + before, local, length], axis=1)
    n_blk = -(-(n_items + min(N_EXPERTS, n_items) * (rows - 1)) // rows)
    blk_lo = jnp.arange(n_blk, dtype=I32) * rows
    blk_e = jnp.minimum(jnp.sum((pend[None, :] <= blk_lo[:, None]).astype(I32), axis=1), N_EXPERTS - 1)
    seg = jnp.concatenate([pstart + counts, pend[-1:], padded - counts, n_blk * rows - pend[-1:]])
    return lpos.astype(I32), meta.astype(I32), blk_e.astype(I32), seg.astype(I32), n_blk * rows


def hier_moe(x, g, shift, scale, gate, w_group, b_group, w_expert, b_expert, w_gate_up, w_down, layer, rows,
             final_g=None, ts=512):
    route, tiles = moe_router(x, g, shift, scale, w_group, b_group, w_expert, b_expert, ts=ts)
    lpos, meta, blk_e, seg, n_rows = _route_layout(route, tiles, rows, ts)
    x_rows = moe_dispatch(x, g, shift, scale, lpos, meta, seg, n_rows, ts)
    y_rows = moe_experts(x_rows, blk_e, w_gate_up, w_down, layer, rows)
    return moe_combine(x, gate, route, y_rows, lpos, meta, ts, final_g=final_g)


def kernel(x_prompt, x_sample, cache_moba_k, cache_moba_v, cache_mla_ckv, cache_mla_kpe, state_conv, page_table, c_prompt, c_sample, norm_mix_g, norm_ffn_g, w_ada, b_ada, rel_bias, moba_w_qkv, moba_w_o, mla_w_in, mla_q_norm_g, mla_w_qb, mla_kv_norm_g, mla_w_kvb, mla_w_o, chunk_w_in, chunk_ln_g, chunk_ln_b, chunk_w_s, chunk_b_s, chunk_w_out, conv_w_in, conv_w, conv_w_out, moe_w_group, moe_b_group, moe_w_expert, moe_b_expert, moe_w_gate_up, moe_w_down, final_norm_g):
    bp, sp, d = x_prompt.shape
    db, t_new, _ = x_sample.shape
    ns = db * t_new
    depth = w_ada.shape[0]
    n_pages = page_table.shape[1]
    page = cache_moba_k.shape[1]
    past = n_pages * page

    mods = ada_mods(jnp.concatenate([c_prompt, c_sample], axis=0), w_ada, b_ada)

    def mod_p(i, k):
        return mods[i, :bp, k * d:(k + 1) * d].reshape(bp, 1, d)

    def mod_s(i, k):
        return jnp.repeat(mods[i, bp:, k * d:(k + 1) * d], t_new, axis=0).reshape(1, ns, d)

    xp = x_prompt
    xs = x_sample.reshape(1, ns, d)
    bf = lambda w: w.astype(BF16)
    outs = {}

    for i in range(depth):
        kind = i % 4
        g_mix = norm_mix_g[i].reshape(1, d)
        g_ffn = norm_ffn_g[i].reshape(1, d)
        mp = [mod_p(i, k) for k in range(6)]
        ms = [mod_s(i, k) for k in range(6)]
        if kind == 0:
            nq, nkv = HQ_A * DH_A, HKV_A * DH_A
            offs = [(0, nq), (nq, nkv), (nq + nkv, nkv)]
            w_qkv, w_o = bf(moba_w_qkv), bf(moba_w_o)
            q, k, v = norm_mod_matmul(xp, g_mix, mp[0], mp[1], w_qkv, offs, [F32] * 3, ts=512)
            o = moba_prompt_attention(q, k, v, rel_bias)
            xp = out_proj(xp, mp[2], o, w_o)
            outs['moba_k_prompt'] = k.reshape(bp, sp, HKV_A, DH_A)
            outs['moba_v_prompt'] = v.reshape(bp, sp, HKV_A, DH_A)
            q, k, v = norm_mod_matmul(xs, g_mix, ms[0], ms[1], w_qkv, offs, [F32] * 3, ts=ns)
            tok_minor = lambda c: jnp.transpose(c, (0, 2, 3, 1)).reshape(-1, nkv, page)
            o = moba_sample_attention(q.reshape(db, t_new, nq), k.reshape(db, t_new, nkv), v.reshape(db, t_new, nkv),
                                      tok_minor(cache_moba_k), tok_minor(cache_moba_v), page_table, rel_bias)
            xs = out_proj(xs, ms[2], o.reshape(1, ns, nq), w_o, ts=ns)
            outs['moba_k_sample'] = k.reshape(db, t_new, HKV_A, DH_A)
            outs['moba_v_sample'] = v.reshape(db, t_new, HKV_A, DH_A)
        elif kind == 1:
            w_v = bf(mla_w_kvb[:, :, NOPE_B:].transpose(1, 0, 2))
            w_o = bf(mla_w_o)
            proj = (mla_w_in, mla_q_norm_g, mla_w_qb, mla_kv_norm_g, mla_w_kvb)
            pos_p = jnp.broadcast_to(jnp.arange(sp, dtype=I32)[None], (bp, sp))
            qh, kv, ckv, kpe = mla_in(xp, g_mix, mp[0], mp[1], *proj, pos_p, ts=256)
            o_lat = mla_prompt_attention(qh, kv)
            xp = out_proj(xp, mp[2], o_lat, w_o, w_v=w_v, ts=256)
            outs['mla_ckv_prompt'], outs['mla_kpe_prompt'] = ckv, kpe
            pos_s = jnp.tile(past + jnp.arange(t_new, dtype=I32), db)[None]
            qh, _, ckv, kpe = mla_in(xs, g_mix, ms[0], ms[1], *proj, pos_s, ts=ns)
            lat = ckv.shape[-1]
            qs = qh.reshape(HEADS_B, db, t_new, lat + ROPE_B).transpose(1, 0, 2, 3).reshape(db, HEADS_B * t_new, -1)
            o_lat = mla_sample_attention(qs, ckv.reshape(db, t_new, lat), kpe.reshape(db, t_new, ROPE_B),
                                         cache_mla_ckv, jnp.transpose(cache_mla_kpe, (0, 2, 1)), page_table)
            o_lat = o_lat.reshape(db, HEADS_B, t_new, lat).transpose(0, 2, 1, 3).reshape(1, ns, HEADS_B * lat)
            xs = out_proj(xs, ms[2], o_lat, w_o, w_v=w_v, ts=ns)
            outs['mla_ckv_sample'] = ckv.reshape(db, t_new, lat)
            outs['mla_kpe_sample'] = kpe.reshape(db, t_new, ROPE_B)
        elif kind == 2:
            w_in, w_out = bf(chunk_w_in), bf(chunk_w_out)
            gw = chunk_ln_g.shape[0] // GROUPS_C
            bmix_p = jnp.repeat(chunk_b_s.T, gw, axis=1)
            xp, _ = gmlp(xp, g_mix, mp[0], mp[1], mp[2], w_in, chunk_ln_g, chunk_ln_b, chunk_w_s, bmix_p, w_out,
                         want_v=False)
            reps = CHUNK_C // t_new
            eye = jnp.eye(reps, dtype=F32)
            wmix_s = jnp.einsum('ab,gij->gaibj', eye, chunk_w_s[:, :t_new, :t_new]).reshape(GROUPS_C, CHUNK_C, CHUNK_C)
            bmix_s = jnp.repeat(jnp.tile(chunk_b_s[:, :t_new].T, (reps, 1)), gw, axis=1)
            xs, v_s = gmlp(xs, g_mix, ms[0], ms[1], ms[2], w_in, chunk_ln_g, chunk_ln_b, wmix_s, bmix_s, w_out,
                           want_v=True)
            outs['chunk_v_sample'] = v_s.reshape(db, t_new, -1)
        else:
            w_in, w_out = bf(conv_w_in), bf(conv_w_out)
            xp, st = short_conv(xp, g_mix, mp[0], mp[1], mp[2], w_in, conv_w, w_out)
            outs['conv_prompt'] = st
            f1 = jnp.repeat(state_conv[:, 1], t_new, axis=0).reshape(1, ns, d)
            f2 = jnp.tile(state_conv, (1, t_new // (CONV_W - 1), 1)).reshape(1, ns, d)
            xs, zc = short_conv(xs, g_mix, ms[0], ms[1], ms[2], w_in, conv_w, w_out, fix=(f1, f2), period=t_new,
                                ts=ns)
            outs['conv_sample'] = zc.reshape(db, t_new, d)[:, t_new - (CONV_W - 1):]
        fg = final_norm_g.reshape(1, d) if i == depth - 1 else None
        moe_w = (moe_w_group[i], moe_b_group[i], moe_w_expert[i], moe_b_expert[i], moe_w_gate_up, moe_w_down, i)
        xp = hier_moe(xp, g_ffn, mp[3], mp[4], mp[5], *moe_w, rows=256, final_g=fg)
        xs = hier_moe(xs, g_ffn, ms[3], ms[4], ms[5], *moe_w, rows=32, final_g=fg, ts=ns)

    return (xp, xs.reshape(db, t_new, d),
            outs['moba_k_prompt'], outs['moba_v_prompt'], outs['moba_k_sample'], outs['moba_v_sample'],
            outs['mla_ckv_prompt'], outs['mla_kpe_prompt'], outs['mla_ckv_sample'], outs['mla_kpe_sample'],
            outs['chunk_v_sample'], outs['conv_prompt'], outs['conv_sample'])
```

```python
import functools
import math

import numpy as np
import jax
import jax.numpy as jnp
from jax import lax
from jax.experimental import pallas as pl
from jax.experimental.pallas import tpu as pltpu

F32, BF16, I32 = jnp.float32, jnp.bfloat16, jnp.int32
EPS = 1e-6
NEG_INF = -1e30
LANES = 128
SUBLANES = 8

HQ_A, HKV_A, GRP_A, DH_A = 16, 4, 4, 64
BLOCK_A, TOPK_A = 256, 3
NUM_BUCKETS, MAX_DISTANCE = 32, 128
HEADS_B, NOPE_B, ROPE_B, V_B = 16, 64, 32, 64
ROPE_THETA = 10000.0
CHUNK_C, GROUPS_C = 128, 8
CONV_W = 3
N_GROUPS_E, EXPERTS_PER_GROUP, TOPK_E = 4, 8, 2
N_EXPERTS = N_GROUPS_E * EXPERTS_PER_GROUP

NT_DIMS = (((1,), (1,)), ((), ()))
HIGHEST = lax.Precision.HIGHEST


def _cparams(sem, vmem_mb):
    return pltpu.CompilerParams(dimension_semantics=sem, vmem_limit_bytes=vmem_mb << 20)


def _dot(a, b):
    return jnp.dot(a.astype(BF16), b.astype(BF16), preferred_element_type=F32)


def _dot_nt(a, b):
    return lax.dot_general(a.astype(BF16), b.astype(BF16), NT_DIMS, preferred_element_type=F32)


def _rms(x, g):
    return x * lax.rsqrt(jnp.mean(x * x, axis=-1, keepdims=True) + EPS) * g


def _ada_norm(x, g, shift, scale):
    return _rms(x, g) * (1.0 + scale) + shift


def _mod_spec(m, ts):
    d = m.shape[-1]
    if m.shape[1] == 1:
        return pl.BlockSpec((1, 1, d), lambda b, j: (b, 0, 0))
    return pl.BlockSpec((1, ts, d), lambda b, j: (b, j, 0))


def _full_spec(a):
    nd = a.ndim
    return pl.BlockSpec(a.shape, lambda *_: (0,) * nd)


def _row_spec(a, ts):
    return pl.BlockSpec((1, ts) + a.shape[2:], lambda b, j: (b, j) + (0,) * (a.ndim - 2))


def _ada_kernel(c_ref, w_ref, b_ref, op_ref, os_ref):
    c = c_ref[...]
    y = _dot(c * jax.nn.sigmoid(c), w_ref[0]) + b_ref[0]
    rows_p = op_ref.shape[2]
    op_ref[0, 0] = y[:rows_p]
    os_ref[0, 0] = y[rows_p:]


def ada_mods(c_all, rows_p, w_ada, b_ada):
    depth, d, n = w_ada.shape
    rows = c_all.shape[0]
    return pl.pallas_call(
        _ada_kernel,
        grid=(depth, n // d),
        in_specs=[pl.BlockSpec((rows, d), lambda i, j: (0, 0)),
                  pl.BlockSpec((1, d, d), lambda i, j: (i, 0, j)),
                  pl.BlockSpec((1, 1, d), lambda i, j: (i, 0, j))],
        out_specs=[pl.BlockSpec((1, 1, rows_p, d), lambda i, j: (i, j, 0, 0)),
                   pl.BlockSpec((1, 1, rows - rows_p, d), lambda i, j: (i, j, 0, 0))],
        out_shape=[jax.ShapeDtypeStruct((depth, n // d, rows_p, d), F32),
                   jax.ShapeDtypeStruct((depth, n // d, rows - rows_p, d), F32)],
        compiler_params=_cparams(("arbitrary", "arbitrary"), 40),
        name="ada_mods",
    )(c_all, w_ada, b_ada.reshape(depth, 1, n))


def _nmm_kernel(x_ref, g_ref, sh_ref, sc_ref, w_ref, *out_refs, offs):
    h = _ada_norm(x_ref[0], g_ref[...], sh_ref[0], sc_ref[0])
    y = _dot(h, w_ref[...])
    for o_ref, (lo, n) in zip(out_refs, offs):
        o_ref[0] = y[:, lo:lo + n].astype(o_ref.dtype)


def norm_mod_matmul(x, g, shift, scale, w, offs, dtypes, ts):
    b, s, d = x.shape
    outs = [jax.ShapeDtypeStruct((b, s, n), dt) for (_, n), dt in zip(offs, dtypes)]
    return pl.pallas_call(
        functools.partial(_nmm_kernel, offs=tuple(offs)),
        grid=(b, s // ts),
        in_specs=[_row_spec(x, ts), _full_spec(g), _mod_spec(shift, ts), _mod_spec(scale, ts), _full_spec(w)],
        out_specs=[pl.BlockSpec((1, ts, n), lambda bb, j: (bb, j, 0)) for _, n in offs],
        out_shape=outs,
        compiler_params=_cparams(("arbitrary", "arbitrary"), 48),
        name="norm_mod_matmul",
    )(x, g, shift, scale, w)


def _out_proj_kernel(x_ref, gate_ref, a_ref, *rest, n_heads):
    if n_heads:
        wv_ref, wo_ref, o_ref = rest
        a = a_ref[0]
        c = a.shape[1] // n_heads
        a = jnp.concatenate([_dot(a[:, h * c:(h + 1) * c], wv_ref[h]) for h in range(n_heads)], axis=1)
    else:
        wo_ref, o_ref = rest
        a = a_ref[0]
    o_ref[0] = x_ref[0] + gate_ref[0] * _dot(a, wo_ref[...])


def out_proj(x, gate, a, w_o, w_v=None, ts=512):
    b, s, d = x.shape
    ins = [x, gate, a] + ([w_v] if w_v is not None else []) + [w_o]
    specs = [_row_spec(x, ts), _mod_spec(gate, ts), _row_spec(a, ts)]
    specs += ([_full_spec(w_v)] if w_v is not None else []) + [_full_spec(w_o)]
    return pl.pallas_call(
        functools.partial(_out_proj_kernel, n_heads=0 if w_v is None else w_v.shape[0]),
        grid=(b, s // ts),
        in_specs=specs,
        out_specs=_row_spec(x, ts),
        out_shape=jax.ShapeDtypeStruct(x.shape, F32),
        compiler_params=_cparams(("arbitrary", "arbitrary"), 48),
        name="out_proj",
    )(*ins)


def _np_bucket(dist):
    n = np.maximum(dist, 0)
    max_exact = NUM_BUCKETS // 2
    ratio = np.log(np.maximum(n, 1).astype(np.float32) / max_exact) / math.log(MAX_DISTANCE / max_exact)
    large = np.minimum(max_exact + (ratio * (NUM_BUCKETS - max_exact)).astype(np.int32), NUM_BUCKETS - 1)
    return np.where(n < max_exact, n, large).astype(np.int32)


assert int(_np_bucket(np.array(BLOCK_A + 1))) == NUM_BUCKETS - 1


def _bias_table(rel_bias, dist):
    idx = _np_bucket(dist)
    onehot = (jnp.asarray(idx.reshape(-1, 1)) == jnp.arange(NUM_BUCKETS, dtype=I32)[None, :]).astype(F32)
    tab = jnp.einsum('bh,nb->hn', rel_bias, onehot, precision=HIGHEST)
    return tab.reshape((rel_bias.shape[1],) + idx.shape)


def _halves_max(s):
    return jnp.maximum(s[:, :LANES], s[:, LANES:])


def _two_pass_finish(s_ref, mx_ref, l_ref, acc_ref, n_chunks, value_chunk):
    m = jnp.max(mx_ref[...], axis=1, keepdims=True)
    mx_ref[...] = jnp.broadcast_to(m, mx_ref.shape)
    l_ref[...] = jnp.zeros_like(l_ref)
    acc_ref[...] = jnp.zeros_like(acc_ref)

    def body(j, c):
        mb = mx_ref[...]
        p = jnp.exp(s_ref[j] - jnp.concatenate([mb, mb], axis=1))
        l_ref[...] += p[:, :LANES] + p[:, LANES:]
        acc_ref[...] += jnp.dot(p.astype(BF16), value_chunk(j), preferred_element_type=F32)
        return c

    lax.fori_loop(0, n_chunks, body, 0)
    return acc_ref[...] * (1.0 / jnp.sum(l_ref[...], axis=1, keepdims=True))


def _moba_prompt_kernel(far_ref, q_ref, k_ref, v_ref, bown_ref, badj_ref, o_ref,
                        kmbd_ref, mb_ref, s_ref, mx_ref, l_ref, acc_ref, *, nb):
    i = pl.program_id(1)
    bs = BLOCK_A
    scale = DH_A ** -0.5

    @pl.when(i == 0)
    def _():
        kmbd_ref[...] = jnp.zeros_like(kmbd_ref)
        for j in range(nb):
            km = jnp.mean(k_ref[0, j * bs:(j + 1) * bs, :], axis=0, keepdims=True)
            for hq in range(HQ_A):
                kvh = hq // GRP_A
                kmbd_ref[j * HQ_A + hq:j * HQ_A + hq + 1, hq * DH_A:(hq + 1) * DH_A] = km[:, kvh * DH_A:(kvh + 1) * DH_A]

    q = q_ref[0]
    gates = lax.dot_general(q, kmbd_ref[...], NT_DIMS, precision=HIGHEST, preferred_element_type=F32)
    lane = lax.broadcasted_iota(I32, gates.shape, 1)
    jidx = lane // HQ_A
    past = jidx < i
    gm = jnp.where(past, gates, NEG_INF)
    rank = jnp.zeros(gates.shape, I32)
    for k in range(1, nb):
        other = pltpu.roll(gm, HQ_A * k, axis=1)
        jo = jnp.where(jidx >= k, jidx - k, jidx - k + nb)
        beats = (other > gm) | ((other == gm) & (jo < jidx))
        rank = rank + beats.astype(I32)
    mb = jnp.where(past & (rank < TOPK_A), 0.0, NEG_INF).astype(F32)
    for j in range(nb):
        mb_ref[j] = mb[:, j * HQ_A:(j + 1) * HQ_A]

    def kv_block(j, h):
        lo = pl.multiple_of(j * bs, bs)
        return (k_ref[0, pl.ds(lo, bs), h * DH_A:(h + 1) * DH_A].astype(BF16),
                v_ref[0, pl.ds(lo, bs), h * DH_A:(h + 1) * DH_A].astype(BF16))

    for h in range(HKV_A):
        heads = [GRP_A * h + g for g in range(GRP_A)]
        q4 = (jnp.concatenate([q[:, hq * DH_A:(hq + 1) * DH_A] for hq in heads], axis=0) * scale).astype(BF16)

        def mask_col(j, extra):
            mj = mb_ref[j]
            return jnp.concatenate([mj[:, hq:hq + 1] + (extra[n] if extra else 0.0)
                                    for n, hq in enumerate(heads)], axis=0)

        s = _dot_nt(q4, kv_block(i, h)[0]) + bown_ref[GRP_A * h:GRP_A * (h + 1)].reshape(GRP_A * bs, bs)
        s_ref[i] = s
        mx_ref[...] = _halves_max(s)

        def scores(j, bias):
            sj = _dot_nt(q4, kv_block(j, h)[0]) + bias
            s_ref[j] = sj
            mx_ref[...] = jnp.maximum(mx_ref[...], _halves_max(sj))

        @pl.when(i >= 1)
        def _():
            badj = badj_ref[GRP_A * h:GRP_A * (h + 1)].reshape(GRP_A * bs, bs)
            scores(i - 1, badj + mask_col(i - 1, None))

        far = [far_ref[hq] for hq in heads]

        def far_body(j, c):
            scores(j, mask_col(j, far))
            return c

        lax.fori_loop(0, jnp.maximum(i - 1, 0), far_body, 0)

        o = _two_pass_finish(s_ref, mx_ref, l_ref, acc_ref, i + 1, lambda j: kv_block(j, h)[1])
        for g, hq in enumerate(heads):
            o_ref[0, :, hq * DH_A:(hq + 1) * DH_A] = o[g * bs:(g + 1) * bs].astype(o_ref.dtype)


def moba_prompt_attention(q, k, v, rel_bias):
    b, s, _ = q.shape
    nb = s // BLOCK_A
    assert s % BLOCK_A == 0 and nb * HQ_A == LANES and nb >= TOPK_A
    t = np.arange(BLOCK_A)[:, None]
    sk = np.arange(BLOCK_A)[None, :]
    bown = jnp.where(jnp.asarray(t >= sk)[None], _bias_table(rel_bias, t - sk), NEG_INF)
    badj = _bias_table(rel_bias, BLOCK_A + t - sk)
    far = rel_bias[NUM_BUCKETS - 1]
    kv_spec = pl.BlockSpec((1, s, HKV_A * DH_A), lambda bb, i: (bb, 0, 0))
    return pl.pallas_call(
        functools.partial(_moba_prompt_kernel, nb=nb),
        grid=(b, nb),
        in_specs=[pl.BlockSpec(memory_space=pltpu.SMEM),
                  pl.BlockSpec((1, BLOCK_A, HQ_A * DH_A), lambda bb, i: (bb, i, 0)),
                  kv_spec, kv_spec, _full_spec(bown), _full_spec(badj)],
        out_specs=pl.BlockSpec((1, BLOCK_A, HQ_A * DH_A), lambda bb, i: (bb, i, 0)),
        out_shape=jax.ShapeDtypeStruct((b, s, HQ_A * DH_A), BF16),
        scratch_shapes=[pltpu.VMEM((nb * HQ_A, HQ_A * DH_A), F32),
                        pltpu.VMEM((nb, BLOCK_A, HQ_A), F32),
                        pltpu.VMEM((nb, GRP_A * BLOCK_A, BLOCK_A), F32),
                        pltpu.VMEM((GRP_A * BLOCK_A, LANES), F32),
                        pltpu.VMEM((GRP_A * BLOCK_A, LANES), F32),
                        pltpu.VMEM((GRP_A * BLOCK_A, DH_A), F32)],
        compiler_params=_cparams(("arbitrary", "arbitrary"), 48),
        name="moba_prompt_attention",
    )(far, q, k, v, bown, badj)


def _page_copies(cache_hbm, buf, sem, pt_ref, bb, slot, n_pages):
    return [pltpu.make_async_copy(cache_hbm.at[pt_ref[bb * n_pages + p]], buf.at[slot, p], sem.at[slot])
            for p in range(n_pages)]


def _pages_wait(cache_hbm, buf, sem, slot, n_pages):
    pltpu.make_async_copy(cache_hbm.at[pl.ds(0, n_pages)], buf.at[slot], sem.at[slot]).wait()


def _new_token_partials(qs, kn, vn, bnew, n_new):
    qf = qs.astype(BF16).astype(F32)
    kf = kn.astype(BF16).astype(F32)
    vf = vn.astype(BF16).astype(F32)
    cols = [jnp.sum(qf * kf[s:s + 1, :], axis=1, keepdims=True) + bnew[:, s:s + 1] for s in range(n_new)]
    m = functools.reduce(jnp.maximum, cols)
    ps = [jnp.exp(c - m) for c in cols]
    l = functools.reduce(jnp.add, ps)
    acc = functools.reduce(jnp.add, [p.astype(BF16).astype(F32) * vf[s:s + 1, :] for s, p in enumerate(ps)])
    return m, l, acc


def _moba_sample_kernel(pt_ref, q_ref, kn_ref, vn_ref, far_ref, blast_ref, bnew_ref, ck_hbm, cv_hbm, o_ref,
                        kbuf, vbuf, ksem, vsem, qbd_ref, *, n_pages, n_new):
    b = pl.program_id(0)
    slot = b % 2
    page = kbuf.shape[3]
    nb = n_pages * page // BLOCK_A
    ppb = n_pages // nb
    scale = DH_A ** -0.5

    def fetch(bb, sl):
        for c in _page_copies(ck_hbm, kbuf, ksem, pt_ref, bb, sl, n_pages):
            c.start()
        for c in _page_copies(cv_hbm, vbuf, vsem, pt_ref, bb, sl, n_pages):
            c.start()

    @pl.when(b == 0)
    def _():
        fetch(0, 0)

    @pl.when(b + 1 < pl.num_programs(0))
    def _():
        fetch(b + 1, 1 - slot)

    q = q_ref[0]
    qbd_ref[...] = jnp.zeros_like(qbd_ref)
    for hq in range(HQ_A):
        kvh = hq // GRP_A
        qbd_ref[hq * n_new:(hq + 1) * n_new, kvh * DH_A:(kvh + 1) * DH_A] = q[:, hq * DH_A:(hq + 1) * DH_A]
    qbd = qbd_ref[...]
    qs = qbd * scale

    carry = _new_token_partials(qs, kn_ref[0], vn_ref[0], bnew_ref[...], n_new)

    _pages_wait(ck_hbm, kbuf, ksem, slot, n_pages)
    rows = qbd.shape[0]
    qsb = qs.astype(BF16)
    raw = [jnp.dot(qsb, kbuf[slot, p].astype(BF16), preferred_element_type=F32) for p in range(n_pages)]

    lane = lax.broadcasted_iota(I32, (rows, LANES), 1)
    gcur = jnp.full((rows, LANES), NEG_INF, F32)
    for j in range(nb):
        blk = functools.reduce(jnp.add, raw[j * ppb:(j + 1) * ppb])
        gcur = jnp.where(lane == j, jnp.sum(blk, axis=1, keepdims=True) * (1.0 / BLOCK_A), gcur)
    sel = jnp.zeros(gcur.shape, jnp.bool_)
    for _ in range(TOPK_A):
        mx = jnp.max(gcur, axis=1, keepdims=True)
        idx = jnp.min(jnp.where(gcur == mx, lane, LANES), axis=1, keepdims=True)
        pick = lane == idx
        sel = sel | (pick & (mx > 0.5 * NEG_INF))
        gcur = jnp.where(pick, -3e38, gcur)
    selbias = jnp.where(sel, 0.0, NEG_INF).astype(F32)

    _pages_wait(cv_hbm, vbuf, vsem, slot, n_pages)
    far = far_ref[...]

    def page_bias(p):
        j = p // ppb
        col = selbias[:, j:j + 1]
        if j == nb - 1:
            lo = (p % ppb) * page
            return blast_ref[:, lo:lo + page] + col
        return jnp.broadcast_to(col + far, (rows, page))

    scores = [raw[p] + page_bias(p) for p in range(n_pages)]
    m0, l0, acc0 = carry
    m = jnp.maximum(m0, jnp.max(functools.reduce(jnp.maximum, scores), axis=1, keepdims=True))
    alpha = jnp.exp(m0 - m)
    lsum = jnp.zeros((rows, page), F32)
    acc = alpha * acc0
    for p in range(n_pages):
        pr = jnp.exp(scores[p] - m)
        lsum = lsum + pr
        acc = acc + _dot_nt(pr, vbuf[slot, p])
    l = alpha * l0 + jnp.sum(lsum, axis=1, keepdims=True)

    o = acc * (1.0 / l)
    for hq in range(HQ_A):
        kvh = hq // GRP_A
        o_ref[0, :, hq * DH_A:(hq + 1) * DH_A] = (
            o[hq * n_new:(hq + 1) * n_new, kvh * DH_A:(kvh + 1) * DH_A].astype(o_ref.dtype))


def moba_sample_attention(q, k_new, v_new, cache_k, cache_v, page_table, rel_bias):
    db, t_new, _ = q.shape
    n_pages = page_table.shape[1]
    page = cache_k.shape[2]
    past = n_pages * page
    assert past % BLOCK_A == 0 and BLOCK_A % page == 0 and t_new <= BLOCK_A
    nb = past // BLOCK_A
    assert TOPK_A <= nb <= LANES
    tt = np.arange(t_new)[:, None]
    blast = _bias_table(rel_bias, BLOCK_A + tt - np.arange(BLOCK_A)[None, :]).reshape(HQ_A * t_new, BLOCK_A)
    ts = np.arange(t_new)[None, :]
    bnew = jnp.where(jnp.asarray(tt >= ts)[None], _bias_table(rel_bias, tt - ts), NEG_INF)
    bnew = bnew.reshape(HQ_A * t_new, t_new)
    far = jnp.repeat(rel_bias[NUM_BUCKETS - 1], t_new).reshape(HQ_A * t_new, 1)
    rows = HQ_A * t_new
    kw = HKV_A * DH_A
    grid_spec = pltpu.PrefetchScalarGridSpec(
        num_scalar_prefetch=1,
        grid=(db,),
        in_specs=[pl.BlockSpec((1, t_new, HQ_A * DH_A), lambda b, pt: (b, 0, 0)),
                  pl.BlockSpec((1, t_new, kw), lambda b, pt: (b, 0, 0)),
                  pl.BlockSpec((1, t_new, kw), lambda b, pt: (b, 0, 0)),
                  pl.BlockSpec((rows, 1), lambda b, pt: (0, 0)),
                  pl.BlockSpec((rows, BLOCK_A), lambda b, pt: (0, 0)),
                  pl.BlockSpec((rows, t_new), lambda b, pt: (0, 0)),
                  pl.BlockSpec(memory_space=pl.ANY),
                  pl.BlockSpec(memory_space=pl.ANY)],
        out_specs=pl.BlockSpec((1, t_new, HQ_A * DH_A), lambda b, pt: (b, 0, 0)),
        scratch_shapes=[pltpu.VMEM((2, n_pages, kw, page), F32),
                        pltpu.VMEM((2, n_pages, kw, page), F32),
                        pltpu.SemaphoreType.DMA((2,)),
                        pltpu.SemaphoreType.DMA((2,)),
                        pltpu.VMEM((rows, kw), F32)])
    return pl.pallas_call(
        functools.partial(_moba_sample_kernel, n_pages=n_pages, n_new=t_new),
        grid_spec=grid_spec,
        out_shape=jax.ShapeDtypeStruct((db, t_new, HQ_A * DH_A), BF16),
        compiler_params=_cparams(("arbitrary",), 56),
        name="moba_sample_attention",
    )(page_table.reshape(-1), q, k_new, v_new, far, blast, bnew, cache_k, cache_v)


def _mla_in_kernel(x_ref, g_ref, sh_ref, sc_ref, win_ref, qg_ref, wqb_ref, kvg_ref, wn_ref,
                   cq_ref, sq_ref, q_ref, kv_ref, ckv_ref, kpe_ref):
    nope_w = HEADS_B * NOPE_B
    rope_w = HEADS_B * ROPE_B
    lat = kv_ref.shape[-1] - ROPE_B
    qa_w = qg_ref.shape[-1]
    scale = (NOPE_B + ROPE_B) ** -0.5
    h = _ada_norm(x_ref[0], g_ref[...], sh_ref[0], sc_ref[0])
    y = _dot(h, win_ref[...])
    qn = _rms(y[:, :qa_w], qg_ref[...])
    ckv = _rms(y[:, qa_w:qa_w + lat], kvg_ref[...])
    cos = cq_ref[0]
    sin = sq_ref[0]
    k_lo = qa_w + lat
    kpe = y[:, k_lo:k_lo + ROPE_B] * cos[:, :ROPE_B] + y[:, k_lo + LANES:k_lo + LANES + ROPE_B] * sin[:, :ROPE_B]
    ckv_ref[0] = ckv
    kpe_ref[0] = kpe
    kv_ref[0, :, :lat] = ckv.astype(BF16)
    kv_ref[0, :, lat:] = kpe.astype(BF16)
    qq = _dot(qn, wqb_ref[...])
    q_pe = (qq[:, nope_w:nope_w + rope_w] * cos + qq[:, nope_w + rope_w:] * sin) * scale
    for hd in range(HEADS_B):
        q_lat = _dot(qq[:, hd * NOPE_B:(hd + 1) * NOPE_B], wn_ref[hd]) * scale
        q_ref[0, hd, :, :lat] = q_lat.astype(BF16)
        q_ref[0, hd, :, lat:] = q_pe[:, hd * ROPE_B:(hd + 1) * ROPE_B].astype(BF16)


def _rope_swap_cols(w, n_rope):
    half = n_rope // 2
    return jnp.concatenate([w[..., half:], w[..., :half]], axis=-1)


def mla_in(x, g, shift, scale, w_in, q_norm_g, w_qb, kv_norm_g, w_kvb, pos, ts):
    b, s, d = x.shape
    qa_w = q_norm_g.shape[0]
    lat = kv_norm_g.shape[0]
    kpe_w = w_in[:, qa_w + lat:]
    pad = jnp.zeros((d, LANES - ROPE_B), w_in.dtype)
    w_in_ext = jnp.concatenate([w_in[:, :qa_w + lat], kpe_w, pad, _rope_swap_cols(kpe_w, ROPE_B), pad], axis=1).astype(BF16)
    wq = w_qb.reshape(qa_w, HEADS_B, NOPE_B + ROPE_B)
    wq_rope = wq[:, :, NOPE_B:]
    w_qb_ext = jnp.concatenate([wq[:, :, :NOPE_B].reshape(qa_w, -1), wq_rope.reshape(qa_w, -1),
                                _rope_swap_cols(wq_rope, ROPE_B).reshape(qa_w, -1)], axis=1).astype(BF16)
    w_nope = w_kvb[:, :, :NOPE_B].transpose(1, 2, 0).astype(BF16)
    half = ROPE_B // 2
    inv = ROPE_THETA ** (-jnp.arange(half, dtype=F32) / half)
    ang = pos.astype(F32)[..., None] * inv
    cos, sin = jnp.cos(ang), jnp.sin(ang)
    cos_t = jnp.tile(jnp.concatenate([cos, cos], axis=-1), (1, 1, HEADS_B))
    sin_t = jnp.tile(jnp.concatenate([-sin, sin], axis=-1), (1, 1, HEADS_B))
    outs = [jax.ShapeDtypeStruct((b, HEADS_B, s, lat + ROPE_B), BF16),
            jax.ShapeDtypeStruct((b, s, lat + ROPE_B), BF16),
            jax.ShapeDtypeStruct((b, s, lat), F32),
            jax.ShapeDtypeStruct((b, s, ROPE_B), F32)]
    qg = q_norm_g.reshape(1, -1)
    kvg = kv_norm_g.reshape(1, -1)
    return pl.pallas_call(
        _mla_in_kernel,
        grid=(b, s // ts),
        in_specs=[_row_spec(x, ts), _full_spec(g), _mod_spec(shift, ts), _mod_spec(scale, ts),
                  _full_spec(w_in_ext), _full_spec(qg), _full_spec(w_qb_ext), _full_spec(kvg), _full_spec(w_nope),
                  _row_spec(cos_t, ts), _row_spec(sin_t, ts)],
        out_specs=[pl.BlockSpec((1, HEADS_B, ts, lat + ROPE_B), lambda bb, j: (bb, 0, j, 0)),
                   pl.BlockSpec((1, ts, lat + ROPE_B), lambda bb, j: (bb, j, 0)),
                   pl.BlockSpec((1, ts, lat), lambda bb, j: (bb, j, 0)),
                   pl.BlockSpec((1, ts, ROPE_B), lambda bb, j: (bb, j, 0))],
        out_shape=outs,
        compiler_params=_cparams(("arbitrary", "arbitrary"), 48),
        name="mla_in",
    )(x, g, shift, scale, w_in_ext, qg, w_qb_ext, kvg, w_nope, cos_t, sin_t)


def _mla_prompt_kernel(q_ref, kv_ref, o_ref, s_ref, mx_ref, l_ref, acc_ref, *, tq, hc):
    i = pl.program_id(1)
    lat = acc_ref.shape[-1]
    rows = hc * tq
    row_t = lax.broadcasted_iota(I32, (rows, tq), 0) % tq
    col_s = lax.broadcasted_iota(I32, (rows, tq), 1)
    causal = col_s <= row_t

    def kv_chunk(j):
        return kv_ref[0, pl.ds(pl.multiple_of(j * tq, tq), tq), :]

    for c in range(HEADS_B // hc):
        q = q_ref[0, c * hc:(c + 1) * hc].reshape(rows, q_ref.shape[-1])

        s = jnp.where(causal, lax.dot_general(q, kv_chunk(i), NT_DIMS, preferred_element_type=F32), NEG_INF)
        s_ref[i] = s
        mx_ref[...] = _halves_max(s)

        def body(j, cc):
            sj = lax.dot_general(q, kv_chunk(j), NT_DIMS, preferred_element_type=F32)
            s_ref[j] = sj
            mx_ref[...] = jnp.maximum(mx_ref[...], _halves_max(sj))
            return cc

        lax.fori_loop(0, i, body, 0)
        o = _two_pass_finish(s_ref, mx_ref, l_ref, acc_ref, i + 1, lambda j: kv_chunk(j)[:, :lat])
        for hl in range(hc):
            hd = c * hc + hl
            o_ref[0, :, hd * lat:(hd + 1) * lat] = o[hl * tq:(hl + 1) * tq].astype(o_ref.dtype)


def mla_prompt_attention(q, kv, hc=4):
    b, nh, s, w = q.shape
    lat = w - ROPE_B
    tq = 2 * LANES
    assert s % tq == 0
    return pl.pallas_call(
        functools.partial(_mla_prompt_kernel, tq=tq, hc=hc),
        grid=(b, s // tq),
        in_specs=[pl.BlockSpec((1, nh, tq, w), lambda bb, i: (bb, 0, i, 0)),
                  pl.BlockSpec((1, s, w), lambda bb, i: (bb, 0, 0))],
        out_specs=pl.BlockSpec((1, tq, nh * lat), lambda bb, i: (bb, i, 0)),
        out_shape=jax.ShapeDtypeStruct((b, s, nh * lat), BF16),
        scratch_shapes=[pltpu.VMEM((s // tq, hc * tq, tq), F32),
                        pltpu.VMEM((hc * tq, LANES), F32), pltpu.VMEM((hc * tq, LANES), F32),
                        pltpu.VMEM((hc * tq, lat), F32)],
        compiler_params=_cparams(("arbitrary", "arbitrary"), 48),
        name="mla_prompt_attention",
    )(q, kv)


def _mla_sample_kernel(pt_ref, q_ref, cn_ref, pn_ref, bnew_ref, cc_hbm, cp_hbm, o_ref,
                       cbuf, pbuf, csem, psem, *, n_pages, n_new, chunk_pages):
    b = pl.program_id(0)
    slot = b % 2
    lat = cbuf.shape[-1]
    page = cbuf.shape[2]

    def fetch(bb, sl):
        for c in _page_copies(cc_hbm, cbuf, csem, pt_ref, bb, sl, n_pages):
            c.start()
        for c in _page_copies(cp_hbm, pbuf, psem, pt_ref, bb, sl, n_pages):
            c.start()

    @pl.when(b == 0)
    def _():
        fetch(0, 0)

    @pl.when(b + 1 < pl.num_programs(0))
    def _():
        fetch(b + 1, 1 - slot)

    q = q_ref[0]
    ql, qp = q[:, :lat], q[:, lat:]
    qf, qpf = ql.astype(F32), qp.astype(F32)
    cn = cn_ref[0].astype(BF16).astype(F32)
    pn = pn_ref[0].astype(BF16).astype(F32)
    bnew = bnew_ref[...]
    cols = [jnp.sum(qf * cn[s:s + 1, :], axis=1, keepdims=True) + jnp.sum(qpf * pn[s:s + 1, :], axis=1, keepdims=True)
            + bnew[:, s:s + 1] for s in range(n_new)]
    _pages_wait(cc_hbm, cbuf, csem, slot, n_pages)
    _pages_wait(cp_hbm, pbuf, psem, slot, n_pages)

    def chunk(buf, c):
        blk = buf[slot, c * chunk_pages:(c + 1) * chunk_pages]
        return blk.reshape(chunk_pages * page, blk.shape[-1]).astype(BF16)

    def rope_scores(c):
        return jnp.concatenate([jnp.dot(qp, pbuf[slot, p].astype(BF16), preferred_element_type=F32)
                                for p in range(c * chunk_pages, (c + 1) * chunk_pages)], axis=1)

    n_chunks = n_pages // chunk_pages
    scores = [_dot_nt(ql, chunk(cbuf, c)) + rope_scores(c) for c in range(n_chunks)]
    m = functools.reduce(jnp.maximum, cols + [jnp.max(s, axis=1, keepdims=True) for s in scores])
    ps = [jnp.exp(c - m) for c in cols]
    l = functools.reduce(jnp.add, ps)
    acc = functools.reduce(jnp.add, [p.astype(BF16).astype(F32) * cn[s:s + 1, :] for s, p in enumerate(ps)])
    for c, s in enumerate(scores):
        p = jnp.exp(s - m)
        l = l + jnp.sum(p, axis=1, keepdims=True)
        acc = acc + jnp.dot(p.astype(BF16), chunk(cbuf, c), preferred_element_type=F32)
    o_ref[0] = (acc * (1.0 / l)).astype(o_ref.dtype)


def mla_sample_attention(q, ckv_new, kpe_new, cache_ckv, cache_kpe, page_table):
    db, rows, w = q.shape
    t_new = ckv_new.shape[1]
    lat = ckv_new.shape[2]
    n_pages = page_table.shape[1]
    page = cache_ckv.shape[1]
    tt = np.arange(t_new)[:, None]
    ts = np.arange(t_new)[None, :]
    bnew = jnp.asarray(np.tile(np.where(tt >= ts, 0.0, NEG_INF).astype(np.float32), (rows // t_new, 1)))
    grid_spec = pltpu.PrefetchScalarGridSpec(
        num_scalar_prefetch=1,
        grid=(db,),
        in_specs=[pl.BlockSpec((1, rows, w), lambda b, pt: (b, 0, 0)),
                  pl.BlockSpec((1, t_new, lat), lambda b, pt: (b, 0, 0)),
                  pl.BlockSpec((1, t_new, ROPE_B), lambda b, pt: (b, 0, 0)),
                  pl.BlockSpec((rows, t_new), lambda b, pt: (0, 0)),
                  pl.BlockSpec(memory_space=pl.ANY),
                  pl.BlockSpec(memory_space=pl.ANY)],
        out_specs=pl.BlockSpec((1, rows, lat), lambda b, pt: (b, 0, 0)),
        scratch_shapes=[pltpu.VMEM((2, n_pages, page, lat), F32),
                        pltpu.VMEM((2, n_pages, ROPE_B, page), F32),
                        pltpu.SemaphoreType.DMA((2,)),
                        pltpu.SemaphoreType.DMA((2,))])
    return pl.pallas_call(
        functools.partial(_mla_sample_kernel, n_pages=n_pages, n_new=t_new,
                          chunk_pages=n_pages // 4 if n_pages % 4 == 0 else n_pages),
        grid_spec=grid_spec,
        out_shape=jax.ShapeDtypeStruct((db, rows, lat), BF16),
        compiler_params=_cparams(("arbitrary",), 40),
        name="mla_sample_attention",
    )(page_table.reshape(-1), q, ckv_new, kpe_new, bnew, cache_ckv, cache_kpe)


def _gmlp_kernel(x_ref, g_ref, sh_ref, sc_ref, gate_ref, win_ref, lng_ref, lnb_ref, wmix_ref, bmix_ref, wout_ref,
                 o_ref, *v_out):
    dv = lng_ref.shape[-1]
    gw = dv // GROUPS_C
    ts = x_ref.shape[1]
    x = x_ref[0]
    h = _ada_norm(x, g_ref[...], sh_ref[0], sc_ref[0])
    y = _dot(h, win_ref[...])
    a = 0.5 * y * (1.0 + lax.erf(y * (1.0 / math.sqrt(2.0))))
    u, v = a[:, :dv], a[:, dv:]
    mu = jnp.mean(v, axis=-1, keepdims=True)
    vc = v - mu
    var = jnp.mean(vc * vc, axis=-1, keepdims=True)
    v = vc * lax.rsqrt(var + EPS) * lng_ref[...] + lnb_ref[...]
    if v_out:
        v_out[0][0] = v
    r = lax.broadcasted_iota(I32, (CHUNK_C, CHUNK_C), 0)
    c = lax.broadcasted_iota(I32, (CHUNK_C, CHUNK_C), 1)
    tril = r >= c
    wm = [jnp.where(tril, wmix_ref[gi], 0.0).astype(BF16) for gi in range(GROUPS_C)]
    vb = v.astype(BF16)
    outs = []
    for ci in range(ts // CHUNK_C):
        rows = slice(ci * CHUNK_C, (ci + 1) * CHUNK_C)
        mixed = jnp.concatenate(
            [jnp.dot(wm[gi], vb[rows, gi * gw:(gi + 1) * gw], preferred_element_type=F32) for gi in range(GROUPS_C)],
            axis=1) + bmix_ref[...]
        outs.append(_dot(u[rows] * mixed, wout_ref[...]))
    o_ref[0] = x + gate_ref[0] * jnp.concatenate(outs, axis=0)


def gmlp(x, g, shift, scale, gate, w_in, ln_g, ln_b, wmix, bmix, w_out, want_v, ts=256):
    b, s, d = x.shape
    dv = ln_g.shape[0]
    lng, lnb = ln_g.reshape(1, dv), ln_b.reshape(1, dv)
    outs = [jax.ShapeDtypeStruct(x.shape, F32)]
    out_specs = [_row_spec(x, ts)]
    if want_v:
        outs.append(jax.ShapeDtypeStruct((b, s, dv), F32))
        out_specs.append(pl.BlockSpec((1, ts, dv), lambda bb, j: (bb, j, 0)))
    res = pl.pallas_call(
        _gmlp_kernel,
        grid=(b, s // ts),
        in_specs=[_row_spec(x, ts), _full_spec(g), _mod_spec(shift, ts), _mod_spec(scale, ts), _mod_spec(gate, ts),
                  _full_spec(w_in), _full_spec(lng), _full_spec(lnb), _full_spec(wmix), _full_spec(bmix),
                  _full_spec(w_out)],
        out_specs=out_specs,
        out_shape=outs,
        compiler_params=_cparams(("arbitrary", "arbitrary"), 56),
        name="gmlp",
    )(x, g, shift, scale, gate, w_in, lng, lnb, wmix, bmix, w_out)
    return res if want_v else (res[0], None)


def _conv_kernel(x_ref, g_ref, sh_ref, sc_ref, gate_ref, win_ref, cw_ref, wout_ref, *rest, period):
    ts, d = x_ref.shape[1], x_ref.shape[2]
    if period:
        f1_ref, f2_ref, o_ref, zc_ref, zp_ref = rest
    else:
        o_ref, st_ref, zp_ref = rest
    x = x_ref[0]
    h = _ada_norm(x, g_ref[...], sh_ref[0], sc_ref[0])
    y = _dot(h, win_ref[...])
    gate_b, zc = y[:, :d], y[:, d:2 * d] * y[:, 2 * d:]
    if period:
        zp_ref[0:SUBLANES] = jnp.zeros((SUBLANES, d), F32)
    else:
        @pl.when(pl.program_id(1) == 0)
        def _():
            zp_ref[0:SUBLANES] = jnp.zeros((SUBLANES, d), F32)
    zp_ref[SUBLANES:] = zc
    r1 = zp_ref[SUBLANES - 1:SUBLANES - 1 + ts]
    r2 = zp_ref[SUBLANES - 2:SUBLANES - 2 + ts]
    if period:
        t = lax.broadcasted_iota(I32, (ts, 1), 0) % period
        r1 = jnp.where(t >= 1, r1, f1_ref[0])
        r2 = jnp.where(t >= 2, r2, f2_ref[0])
        zc_ref[0] = zc
    else:
        st_ref[0] = zc[ts - (CONV_W - 1):]
        zp_ref[0:SUBLANES] = zc[ts - SUBLANES:]
    cw = cw_ref[...]
    yc = cw[0:1] * r2 + cw[1:2] * r1 + cw[2:3] * zc
    o_ref[0] = x + gate_ref[0] * _dot(gate_b * yc, wout_ref[...])


def short_conv(x, g, shift, scale, gate, w_in, conv_w, w_out, fix=None, period=0, ts=512):
    b, s, d = x.shape
    assert CONV_W == 3
    ins = [x, g, shift, scale, gate, w_in, conv_w, w_out]
    specs = [_row_spec(x, ts), _full_spec(g), _mod_spec(shift, ts), _mod_spec(scale, ts), _mod_spec(gate, ts),
             _full_spec(w_in), _full_spec(conv_w), _full_spec(w_out)]
    if period:
        ins += list(fix)
        specs += [_row_spec(x, ts), _row_spec(x, ts)]
        outs = [jax.ShapeDtypeStruct(x.shape, F32), jax.ShapeDtypeStruct(x.shape, F32)]
        out_specs = [_row_spec(x, ts), _row_spec(x, ts)]
    else:
        outs = [jax.ShapeDtypeStruct(x.shape, F32), jax.ShapeDtypeStruct((b, CONV_W - 1, d), F32)]
        out_specs = [_row_spec(x, ts), pl.BlockSpec((1, CONV_W - 1, d), lambda bb, j: (bb, 0, 0))]
    return pl.pallas_call(
        functools.partial(_conv_kernel, period=period),
        grid=(b, s // ts),
        in_specs=specs,
        out_specs=out_specs,
        out_shape=outs,
        scratch_shapes=[pltpu.VMEM((SUBLANES + ts, d), F32)],
        compiler_params=_cparams(("arbitrary", "arbitrary"), 56),
        name="short_conv",
    )(*ins)


def _router_kernel(x_ref, g_ref, sh_ref, sc_ref, whi_ref, wlo_ref, bias_ref, ltri_ref, route_ref, cnt_ref, run_ref):
    first = (pl.program_id(0) == 0) & (pl.program_id(1) == 0)

    @pl.when(first)
    def _():
        run_ref[...] = jnp.zeros_like(run_ref)

    h = _ada_norm(x_ref[0], g_ref[...], sh_ref[0], sc_ref[0])
    h_hi = h.astype(BF16)
    h_lo = (h - h_hi.astype(F32)).astype(BF16)
    logits = (jnp.dot(h_hi, whi_ref[...], preferred_element_type=F32)
              + jnp.dot(h_lo, whi_ref[...], preferred_element_type=F32)
              + jnp.dot(h_hi, wlo_ref[...], preferred_element_type=F32)) + bias_ref[...]
    lane = lax.broadcasted_iota(I32, logits.shape, 1)
    low = -3e38
    is_g = lane < N_GROUPS_E
    lg = jnp.where(is_g, logits, low)
    mg = jnp.max(lg, axis=1, keepdims=True)
    g_top = jnp.min(jnp.where(lg == mg, lane, LANES), axis=1, keepdims=True)
    p_g = 1.0 / jnp.sum(jnp.where(is_g, jnp.exp(lg - mg), 0.0), axis=1, keepdims=True)
    e_lo = N_GROUPS_E + g_top * EXPERTS_PER_GROUP
    le = jnp.where((lane >= e_lo) & (lane < e_lo + EXPERTS_PER_GROUP), logits, low)
    m1 = jnp.max(le, axis=1, keepdims=True)
    i1 = jnp.min(jnp.where(le == m1, lane, LANES), axis=1, keepdims=True)
    le2 = jnp.where(lane == i1, low, le)
    m2 = jnp.max(le2, axis=1, keepdims=True)
    i2 = jnp.min(jnp.where(le2 == m2, lane, LANES), axis=1, keepdims=True)
    e2 = jnp.exp(m2 - m1)
    w1 = p_g / (1.0 + e2)
    w2 = p_g * e2 / (1.0 + e2)
    e1 = i1 - N_GROUPS_E
    e2i = i2 - N_GROUPS_E
    oh1 = (lane == e1).astype(F32)
    oh2 = (lane == e2i).astype(F32)
    cnt = oh1 + oh2
    base = jnp.dot(ltri_ref[...], cnt.astype(BF16), preferred_element_type=F32) + run_ref[...]
    r1 = jnp.sum(oh1 * base, axis=1, keepdims=True)
    r2 = jnp.sum(oh2 * base, axis=1, keepdims=True)
    before = run_ref[...]
    run = before + jnp.sum(cnt, axis=0, keepdims=True)
    run_ref[...] = run
    sub = lax.broadcasted_iota(I32, cnt_ref.shape, 0)
    cnt_ref[...] = jnp.where(sub == 0, before, jnp.where(sub == 1, run, 0.0))
    vals = (e1.astype(F32), e2i.astype(F32), w1, w2, r1, r2)
    route = jnp.zeros(logits.shape, F32)
    for n, val in enumerate(vals):
        route = jnp.where(lane == n, val, route)
    route_ref[...] = route


def moe_router(x, g, shift, scale, w_group, b_group, w_expert, b_expert, ts=512):
    b, s, d = x.shape
    n = b * s
    w = jnp.concatenate([w_group, w_expert], axis=1)
    w = jnp.pad(w, ((0, 0), (0, LANES - w.shape[1])))
    w_hi = w.astype(BF16)
    w_lo = (w - w_hi.astype(F32)).astype(BF16)
    bias = jnp.pad(jnp.concatenate([b_group, b_expert]), (0, LANES - N_GROUPS_E - N_EXPERTS)).reshape(1, LANES)
    ltri = jnp.asarray(np.tril(np.ones((ts, ts), np.float32), -1), BF16)
    nj = s // ts
    return pl.pallas_call(
        _router_kernel,
        grid=(b, nj),
        in_specs=[_row_spec(x, ts), _full_spec(g), _mod_spec(shift, ts), _mod_spec(scale, ts),
                  _full_spec(w_hi), _full_spec(w_lo), _full_spec(bias), _full_spec(ltri)],
        out_specs=[pl.BlockSpec((ts, LANES), lambda bb, j: (bb * nj + j, 0)),
                   pl.BlockSpec((SUBLANES, LANES), lambda bb, j: (bb * nj + j, 0))],
        out_shape=[jax.ShapeDtypeStruct((n, LANES), F32), jax.ShapeDtypeStruct((b * nj * SUBLANES, LANES), F32)],
        scratch_shapes=[pltpu.VMEM((1, LANES), F32)],
        compiler_params=_cparams(("arbitrary", "arbitrary"), 40),
        name="moe_router",
    )(x, g, shift, scale, w_hi, w_lo, bias, ltri)


RUN_PIECE_LOG2 = 6


def _run_pieces(n, piece):
    big = 1 << RUN_PIECE_LOG2
    n_big = n >> RUN_PIECE_LOG2

    def body(it, c):
        piece(it * big, big)
        return c

    lax.fori_loop(0, n_big, body, 0)
    base = n_big * big
    rem = n - base
    for bit in range(RUN_PIECE_LOG2 - 1, -1, -1):
        @pl.when(((rem >> bit) & 1) == 1)
        def _():
            piece(base + ((rem >> (bit + 1)) << (bit + 1)), 1 << bit)


def _tile_rows(ref, lead, p, size):
    return ref.at[lead + (pl.ds(pl.multiple_of(p * SUBLANES, SUBLANES), size * SUBLANES), slice(None))]


def _tile_runs(meta_ref):
    return [(meta_ref[0, 0, e], meta_ref[0, 0, N_EXPERTS + e], meta_ref[0, 0, 2 * N_EXPERTS + e])
            for e in range(N_EXPERTS)]


def _dispatch_kernel(seg_ref, meta_ref, lpos_ref, x_ref, g_ref, sh_ref, sc_ref, xs_hbm,
                     hbuf, gbuf, zero_ref, sem, zsem, *, ts):
    i = pl.program_id(0) * pl.num_programs(1) + pl.program_id(1)
    n = pl.num_programs(0) * pl.num_programs(1)
    slot = i % 2
    n_seg = seg_ref.shape[0] // 2

    def slot_wait(sl):
        pltpu.make_async_copy(gbuf.at[sl], xs_hbm.at[pl.ds(0, gbuf.shape[1]), :], sem.at[sl]).wait()

    @pl.when(i == 0)
    def _():
        zero_ref[...] = jnp.zeros_like(zero_ref)
        for phase in ("start", "wait"):
            def seg_body(sg, c):
                def fill(off, size):
                    cp = pltpu.make_async_copy(zero_ref.at[pl.ds(0, size * SUBLANES), :],
                                               _tile_rows(xs_hbm, (), seg_ref[sg] + off, size), zsem.at[0])
                    cp.start() if phase == "start" else cp.wait()
                _run_pieces(seg_ref[n_seg + sg], fill)
                return c
            lax.fori_loop(0, n_seg, seg_body, 0)

    @pl.when(i >= 2)
    def _():
        slot_wait(slot)

    h = _ada_norm(x_ref[0], g_ref[...], sh_ref[0], sc_ref[0])
    for s in range(SUBLANES):
        hbuf[pl.ds(s, ts, stride=SUBLANES), :] = h[:, s * LANES:(s + 1) * LANES]
    for r in range(ts):
        row = hbuf[r * SUBLANES:(r + 1) * SUBLANES, :]
        for k in range(TOPK_E):
            lp = lpos_ref[0, 0, TOPK_E * r + k]
            gbuf[slot, pl.ds(pl.multiple_of(lp * SUBLANES, SUBLANES), SUBLANES), :] = row
    for first, local, length in _tile_runs(meta_ref):
        def send(off, size, first=first, local=local):
            pltpu.make_async_copy(_tile_rows(gbuf, (slot,), local + off, size),
                                  _tile_rows(xs_hbm, (), first + off, size), sem.at[slot]).start()
        _run_pieces(length, send)

    @pl.when(i == n - 1)
    def _():
        slot_wait(slot)

        @pl.when(n >= 2)
        def _():
            slot_wait(1 - slot)


def moe_dispatch(x, g, shift, scale, lpos, meta, seg, n_rows, ts):
    b, s, d = x.shape
    assert d == SUBLANES * LANES
    nj = s // ts
    items = TOPK_E * ts
    grid_spec = pltpu.PrefetchScalarGridSpec(
        num_scalar_prefetch=1,
        grid=(b, nj),
        in_specs=[pl.BlockSpec((1, 1, 3 * N_EXPERTS), lambda bb, j, sg: (bb * nj + j, 0, 0), memory_space=pltpu.SMEM),
                  pl.BlockSpec((1, 1, items), lambda bb, j, sg: (bb * nj + j, 0, 0), memory_space=pltpu.SMEM),
                  pl.BlockSpec((1, ts, d), lambda bb, j, sg: (bb, j, 0)),
                  pl.BlockSpec(g.shape, lambda bb, j, sg: (0, 0)),
                  (pl.BlockSpec((1, 1, d), lambda bb, j, sg: (bb, 0, 0)) if shift.shape[1] == 1
                   else pl.BlockSpec((1, ts, d), lambda bb, j, sg: (bb, j, 0))),
                  (pl.BlockSpec((1, 1, d), lambda bb, j, sg: (bb, 0, 0)) if scale.shape[1] == 1
                   else pl.BlockSpec((1, ts, d), lambda bb, j, sg: (bb, j, 0)))],
        out_specs=pl.BlockSpec(memory_space=pl.ANY),
        scratch_shapes=[pltpu.VMEM((ts * SUBLANES, LANES), F32), pltpu.VMEM((2, items * SUBLANES, LANES), F32),
                        pltpu.VMEM(((1 << RUN_PIECE_LOG2) * SUBLANES, LANES), F32),
                        pltpu.SemaphoreType.DMA((2,)), pltpu.SemaphoreType.DMA((1,))])
    return pl.pallas_call(
        functools.partial(_dispatch_kernel, ts=ts),
        grid_spec=grid_spec,
        out_shape=jax.ShapeDtypeStruct((n_rows * SUBLANES, LANES), F32),
        compiler_params=_cparams(("arbitrary", "arbitrary"), 48),
        name="moe_dispatch",
    )(seg, meta.reshape(b * nj, 1, 3 * N_EXPERTS), lpos.reshape(b * nj, 1, items), x, g, shift, scale)


def _expert_kernel(be_ref, x_ref, wgu_ref, wd_ref, y_ref, *, rows):
    x = jnp.concatenate([x_ref[pl.ds(s, rows, stride=SUBLANES), :] for s in range(SUBLANES)], axis=1)
    ab = _dot(x, wgu_ref[0, 0])
    f = ab.shape[1] // 2
    a, bb = ab[:, :f], ab[:, f:]
    y = _dot(a * jax.nn.sigmoid(a) * bb, wd_ref[0, 0])
    for s in range(SUBLANES):
        y_ref[pl.ds(s, rows, stride=SUBLANES), :] = y[:, s * LANES:(s + 1) * LANES]


def moe_experts(x_rows, blk_e, w_gate_up, w_down, layer, rows):
    n_blk = blk_e.shape[0]
    _, ne, d, f2 = w_gate_up.shape
    grid_spec = pltpu.PrefetchScalarGridSpec(
        num_scalar_prefetch=1,
        grid=(n_blk,),
        in_specs=[pl.BlockSpec((rows * SUBLANES, LANES), lambda i, be: (i, 0)),
                  pl.BlockSpec((1, 1, d, f2), lambda i, be: (layer, be[i], 0, 0)),
                  pl.BlockSpec((1, 1, f2 // 2, d), lambda i, be: (layer, be[i], 0, 0))],
        out_specs=pl.BlockSpec((rows * SUBLANES, LANES), lambda i, be: (i, 0)))
    return pl.pallas_call(
        functools.partial(_expert_kernel, rows=rows),
        grid_spec=grid_spec,
        out_shape=jax.ShapeDtypeStruct(x_rows.shape, F32),
        compiler_params=_cparams(("arbitrary",), 40),
        name="moe_experts",
    )(blk_e, x_rows, w_gate_up, w_down)


def _combine_kernel(cur_ref, nxt_ref, lpos_ref, x_ref, gate_ref, route_ref, y_hbm, *rest, ts, final):
    if final:
        fg_ref, o_ref, ybuf, tbuf, sem = rest
    else:
        o_ref, ybuf, tbuf, sem = rest
    i = pl.program_id(0) * pl.num_programs(1) + pl.program_id(1)
    n = pl.num_programs(0) * pl.num_programs(1)
    slot = i % 2

    def fetch(meta_ref, sl):
        for first, local, length in _tile_runs(meta_ref):
            def recv(off, size, first=first, local=local):
                pltpu.make_async_copy(_tile_rows(y_hbm, (), first + off, size),
                                      _tile_rows(ybuf, (sl,), local + off, size), sem.at[sl]).start()
            _run_pieces(length, recv)

    @pl.when(i == 0)
    def _():
        fetch(cur_ref, 0)

    @pl.when(i + 1 < n)
    def _():
        fetch(nxt_ref, 1 - slot)

    pltpu.make_async_copy(y_hbm.at[pl.ds(0, ybuf.shape[1]), :], ybuf.at[slot], sem.at[slot]).wait()
    for r in range(ts):
        for k in range(TOPK_E):
            lp = lpos_ref[0, 0, TOPK_E * r + k]
            tbuf[(k * ts + r) * SUBLANES:(k * ts + r + 1) * SUBLANES, :] = (
                ybuf[slot, pl.ds(pl.multiple_of(lp * SUBLANES, SUBLANES), SUBLANES), :])
    ys = [jnp.concatenate([tbuf[pl.ds(k * ts * SUBLANES + s, ts, stride=SUBLANES), :] for s in range(SUBLANES)],
                          axis=1) for k in range(TOPK_E)]
    route = route_ref[...]
    y = route[:, 2:3] * ys[0] + route[:, 3:4] * ys[1]
    out = x_ref[0] + gate_ref[0] * y
    if final:
        out = _rms(out, fg_ref[...])
    o_ref[0] = out


def moe_combine(x, gate, route, y_rows, lpos, meta, ts, final_g=None):
    b, s, d = x.shape
    nj = s // ts
    n_tiles = b * nj
    items = TOPK_E * ts
    meta3 = meta.reshape(n_tiles, 1, 3 * N_EXPERTS)
    meta_blk = lambda f: pl.BlockSpec((1, 1, 3 * N_EXPERTS), f, memory_space=pltpu.SMEM)
    ins = [meta3, meta3, lpos.reshape(n_tiles, 1, items), x, gate, route, y_rows]
    specs = [meta_blk(lambda bb, j: (bb * nj + j, 0, 0)),
             meta_blk(lambda bb, j: (jnp.minimum(bb * nj + j + 1, n_tiles - 1), 0, 0)),
             pl.BlockSpec((1, 1, items), lambda bb, j: (bb * nj + j, 0, 0), memory_space=pltpu.SMEM),
             _row_spec(x, ts), _mod_spec(gate, ts),
             pl.BlockSpec((ts, LANES), lambda bb, j: (bb * nj + j, 0)),
             pl.BlockSpec(memory_space=pl.ANY)]
    if final_g is not None:
        ins.append(final_g)
        specs.append(_full_spec(final_g))
    return pl.pallas_call(
        functools.partial(_combine_kernel, ts=ts, final=final_g is not None),
        grid=(b, nj),
        in_specs=specs,
        out_specs=_row_spec(x, ts),
        out_shape=jax.ShapeDtypeStruct(x.shape, F32),
        scratch_shapes=[pltpu.VMEM((2, items * SUBLANES, LANES), F32), pltpu.VMEM((items * SUBLANES, LANES), F32),
                        pltpu.SemaphoreType.DMA((2,))],
        compiler_params=_cparams(("arbitrary", "arbitrary"), 48),
        name="moe_combine",
    )(*ins)


def _route_layout(route, tiles, rows, ts):
    n_tok = route.shape[0]
    n_items = n_tok * TOPK_E
    n_tiles = n_tok // ts
    eid = route[:, :TOPK_E].astype(I32).reshape(n_tiles, ts, TOPK_E)
    rank = route[:, 2 * TOPK_E:3 * TOPK_E].astype(I32).reshape(n_tiles, ts, TOPK_E)
    t3 = tiles.reshape(n_tiles, SUBLANES, LANES)
    before = t3[:, 0, :N_EXPERTS].astype(I32)
    after = t3[:, 1, :N_EXPERTS].astype(I32)
    length = after - before
    counts = after[-1]
    padded = (counts + rows - 1) // rows * rows
    pend = jnp.cumsum(padded)
    pstart = pend - padded
    local = jnp.cumsum(length, axis=1) - length
    experts = jnp.arange(N_EXPERTS, dtype=I32)
    shift = (local - before)[:, None, None, :]
    lpos = jnp.sum(jnp.where(eid[..., None] == experts, shift, 0), axis=-1) + rank
    meta = jnp.concatenate([pstart[None, :] + before, local, length], axis=1)
    n_blk = -(-(n_items + min(N_EXPERTS, n_items) * (rows - 1)) // rows)
    blk_lo = jnp.arange(n_blk, dtype=I32) * rows
    blk_e = jnp.minimum(jnp.sum((pend[None, :] <= blk_lo[:, None]).astype(I32), axis=1), N_EXPERTS - 1)
    seg = jnp.concatenate([pstart + counts, pend[-1:], padded - counts, n_blk * rows - pend[-1:]])
    return lpos.astype(I32), meta.astype(I32), blk_e.astype(I32), seg.astype(I32), n_blk * rows


def hier_moe(x, g, shift, scale, gate, w_group, b_group, w_expert, b_expert, w_gate_up, w_down, layer, rows,
             final_g=None, ts=512):
    route, tiles = moe_router(x, g, shift, scale, w_group, b_group, w_expert, b_expert, ts=ts)
    lpos, meta, blk_e, seg, n_rows = _route_layout(route, tiles, rows, ts)
    x_rows = moe_dispatch(x, g, shift, scale, lpos, meta, seg, n_rows, ts)
    y_rows = moe_experts(x_rows, blk_e, w_gate_up, w_down, layer, rows)
    return moe_combine(x, gate, route, y_rows, lpos, meta, ts, final_g=final_g)


def kernel(x_prompt, x_sample, cache_moba_k, cache_moba_v, cache_mla_ckv, cache_mla_kpe, state_conv, page_table, c_prompt, c_sample, norm_mix_g, norm_ffn_g, w_ada, b_ada, rel_bias, moba_w_qkv, moba_w_o, mla_w_in, mla_q_norm_g, mla_w_qb, mla_kv_norm_g, mla_w_kvb, mla_w_o, chunk_w_in, chunk_ln_g, chunk_ln_b, chunk_w_s, chunk_b_s, chunk_w_out, conv_w_in, conv_w, conv_w_out, moe_w_group, moe_b_group, moe_w_expert, moe_b_expert, moe_w_gate_up, moe_w_down, final_norm_g):
    bp, sp, d = x_prompt.shape
    db, t_new, _ = x_sample.shape
    ns = db * t_new
    depth = w_ada.shape[0]
    n_pages = page_table.shape[1]
    page = cache_moba_k.shape[1]
    past = n_pages * page

    c_all = jnp.concatenate([c_prompt, jnp.repeat(c_sample, t_new, axis=0)], axis=0)
    mods_p, mods_s = ada_mods(c_all, bp, w_ada, b_ada)

    def mod_p(i, k):
        return mods_p[i, k].reshape(bp, 1, d)

    def mod_s(i, k):
        return mods_s[i, k].reshape(1, ns, d)

    xp = x_prompt
    xs = x_sample.reshape(1, ns, d)
    bf = lambda w: w.astype(BF16)
    outs = {}

    for i in range(depth):
        kind = i % 4
        g_mix = norm_mix_g[i].reshape(1, d)
        g_ffn = norm_ffn_g[i].reshape(1, d)
        mp = [mod_p(i, k) for k in range(6)]
        ms = [mod_s(i, k) for k in range(6)]
        if kind == 0:
            nq, nkv = HQ_A * DH_A, HKV_A * DH_A
            offs = [(0, nq), (nq, nkv), (nq + nkv, nkv)]
            w_qkv, w_o = bf(moba_w_qkv), bf(moba_w_o)
            q, k, v = norm_mod_matmul(xp, g_mix, mp[0], mp[1], w_qkv, offs, [F32] * 3, ts=512)
            o = moba_prompt_attention(q, k, v, rel_bias)
            xp = out_proj(xp, mp[2], o, w_o)
            outs['moba_k_prompt'] = k.reshape(bp, sp, HKV_A, DH_A)
            outs['moba_v_prompt'] = v.reshape(bp, sp, HKV_A, DH_A)
            q, k, v = norm_mod_matmul(xs, g_mix, ms[0], ms[1], w_qkv, offs, [F32] * 3, ts=ns)
            tok_minor = lambda c: jnp.transpose(c, (0, 2, 3, 1)).reshape(-1, nkv, page)
            o = moba_sample_attention(q.reshape(db, t_new, nq), k.reshape(db, t_new, nkv), v.reshape(db, t_new, nkv),
                                      tok_minor(cache_moba_k), tok_minor(cache_moba_v), page_table, rel_bias)
            xs = out_proj(xs, ms[2], o.reshape(1, ns, nq), w_o, ts=ns)
            outs['moba_k_sample'] = k.reshape(db, t_new, HKV_A, DH_A)
            outs['moba_v_sample'] = v.reshape(db, t_new, HKV_A, DH_A)
        elif kind == 1:
            w_v = bf(mla_w_kvb[:, :, NOPE_B:].transpose(1, 0, 2))
            w_o = bf(mla_w_o)
            proj = (mla_w_in, mla_q_norm_g, mla_w_qb, mla_kv_norm_g, mla_w_kvb)
            pos_p = jnp.broadcast_to(jnp.arange(sp, dtype=I32)[None], (bp, sp))
            qh, kv, ckv, kpe = mla_in(xp, g_mix, mp[0], mp[1], *proj, pos_p, ts=256)
            o_lat = mla_prompt_attention(qh, kv)
            xp = out_proj(xp, mp[2], o_lat, w_o, w_v=w_v, ts=256)
            outs['mla_ckv_prompt'], outs['mla_kpe_prompt'] = ckv, kpe
            pos_s = jnp.tile(past + jnp.arange(t_new, dtype=I32), db)[None]
            qh, _, ckv, kpe = mla_in(xs, g_mix, ms[0], ms[1], *proj, pos_s, ts=ns)
            lat = ckv.shape[-1]
            qs = qh.reshape(HEADS_B, db, t_new, lat + ROPE_B).transpose(1, 0, 2, 3).reshape(db, HEADS_B * t_new, -1)
            o_lat = mla_sample_attention(qs, ckv.reshape(db, t_new, lat), kpe.reshape(db, t_new, ROPE_B),
                                         cache_mla_ckv, jnp.transpose(cache_mla_kpe, (0, 2, 1)), page_table)
            o_lat = o_lat.reshape(db, HEADS_B, t_new, lat).transpose(0, 2, 1, 3).reshape(1, ns, HEADS_B * lat)
            xs = out_proj(xs, ms[2], o_lat, w_o, w_v=w_v, ts=ns)
            outs['mla_ckv_sample'] = ckv.reshape(db, t_new, lat)
            outs['mla_kpe_sample'] = kpe.reshape(db, t_new, ROPE_B)
        elif kind == 2:
            w_in, w_out = bf(chunk_w_in), bf(chunk_w_out)
            gw = chunk_ln_g.shape[0] // GROUPS_C
            bmix_p = jnp.repeat(chunk_b_s.T, gw, axis=1)
            xp, _ = gmlp(xp, g_mix, mp[0], mp[1], mp[2], w_in, chunk_ln_g, chunk_ln_b, chunk_w_s, bmix_p, w_out,
                         want_v=False)
            reps = CHUNK_C // t_new
            eye = jnp.eye(reps, dtype=F32)
            wmix_s = jnp.einsum('ab,gij->gaibj', eye, chunk_w_s[:, :t_new, :t_new]).reshape(GROUPS_C, CHUNK_C, CHUNK_C)
            bmix_s = jnp.repeat(jnp.tile(chunk_b_s[:, :t_new].T, (reps, 1)), gw, axis=1)
            xs, v_s = gmlp(xs, g_mix, ms[0], ms[1], ms[2], w_in, chunk_ln_g, chunk_ln_b, wmix_s, bmix_s, w_out,
                           want_v=True)
            outs['chunk_v_sample'] = v_s.reshape(db, t_new, -1)
        else:
            w_in, w_out = bf(conv_w_in), bf(conv_w_out)
            xp, st = short_conv(xp, g_mix, mp[0], mp[1], mp[2], w_in, conv_w, w_out)
            outs['conv_prompt'] = st
            f1 = jnp.repeat(state_conv[:, 1], t_new, axis=0).reshape(1, ns, d)
            f2 = jnp.tile(state_conv, (1, t_new // (CONV_W - 1), 1)).reshape(1, ns, d)
            xs, zc = short_conv(xs, g_mix, ms[0], ms[1], ms[2], w_in, conv_w, w_out, fix=(f1, f2), period=t_new,
                                ts=ns)
            outs['conv_sample'] = zc.reshape(db, t_new, d)[:, t_new - (CONV_W - 1):]
        fg = final_norm_g.reshape(1, d) if i == depth - 1 else None
        moe_w = (moe_w_group[i], moe_b_group[i], moe_w_expert[i], moe_b_expert[i], moe_w_gate_up, moe_w_down, i)
        xp = hier_moe(xp, g_ffn, mp[3], mp[4], mp[5], *moe_w, rows=256, final_g=fg)
        xs = hier_moe(xs, g_ffn, ms[3], ms[4], ms[5], *moe_w, rows=32, final_g=fg, ts=ns)

    return (xp, xs.reshape(db, t_new, d),
            outs['moba_k_prompt'], outs['moba_v_prompt'], outs['moba_k_sample'], outs['moba_v_sample'],
            outs['mla_ckv_prompt'], outs['mla_kpe_prompt'], outs['mla_ckv_sample'], outs['mla_kpe_sample'],
            outs['chunk_v_sample'], outs['conv_prompt'], outs['conv_sample'])
```
